```python
import math
import jax, jax.numpy as jnp
from jax import lax
import numpy as np

D_MODEL = 4096
BATCH = 4
SEQ = 4096
DEPTH = 1
DEC_BATCH = 1
DEC_SEQ = 16384
PAST_LEN = 128

HEAD_DIM = 128
ATT_HEADS = 16
ATT_KV_HEADS = 4
ATT_GROUP = ATT_HEADS // ATT_KV_HEADS
WINDOW = 128
BLOCK = 128
REL_BUCKETS = 32
REL_MAX_DIST = 128
RET_HEADS = 8
RET_DK = 128
RET_DV = 256
RET_CHUNK = 128
ROPE_BASE = 10000.0
N_EXPERTS = 128
TOP_K = 8
N_GROUPS = 8
TOPK_GROUPS = 4
D_EXPERT = 1024
D_SHARED = 1024
ROUTED_SCALE = 2.5
MOE_BLOCK = 128
LN_EPS = 1e-5
GN_EPS = 1e-5
ALPHA = (2.0 * DEPTH) ** 0.25
BETA = (8.0 * DEPTH) ** -0.25

ATT_Q = ATT_HEADS * HEAD_DIM
ATT_KV = ATT_KV_HEADS * HEAD_DIM
RET_QK = RET_HEADS * RET_DK
RET_V = RET_HEADS * RET_DV
_WIDTHS = [ATT_Q, ATT_KV, ATT_KV, RET_QK, RET_QK, RET_V, RET_V, D_MODEL, D_MODEL]
IN_WIDTH = sum(_WIDTHS)
SPLITS = [int(v) for v in np.cumsum(_WIDTHS)[:-1]]

kernel_name = "hybrid_swa_retention_moe_encoder"

F32 = jnp.float32


def _layernorm(x, w, b):
    xf = x.astype(F32)
    mu = jnp.mean(xf, axis=-1, keepdims=True)
    var = jnp.mean(jnp.square(xf - mu), axis=-1, keepdims=True)
    y = (xf - mu) * lax.rsqrt(var + LN_EPS)
    return (y * w.astype(F32) + b.astype(F32)).astype(x.dtype)


def _rope(x, pos):
    half = x.shape[-1] // 2
    inv = ROPE_BASE ** (-jnp.arange(half, dtype=F32) * 2.0 / x.shape[-1])
    ang = pos[:, None] * inv[None, :]
    cos = jnp.cos(ang)[:, None, :]
    sin = jnp.sin(ang)[:, None, :]
    x1, x2 = x[..., :half], x[..., half:]
    return jnp.concatenate([x1 * cos - x2 * sin, x2 * cos + x1 * sin], axis=-1)


def _t5_bucket(rel):
    nb = REL_BUCKETS // 2
    max_exact = nb // 2
    ret = (rel > 0).astype(jnp.int32) * nb
    n = jnp.abs(rel)
    nf = jnp.maximum(n, max_exact).astype(F32)
    large = max_exact + (jnp.log(nf / max_exact) / math.log(REL_MAX_DIST / max_exact) * (nb - max_exact)).astype(jnp.int32)
    large = jnp.minimum(large, nb - 1)
    return ret + jnp.where(n < max_exact, n, large)


def _window_attention(q, k, v, sink, rel_table):
    B, S = q.shape[0], q.shape[1]
    nb = S // BLOCK
    qb = q.reshape(B, nb, BLOCK, ATT_KV_HEADS, ATT_GROUP, HEAD_DIM)
    pad = ((0, 0), (BLOCK, BLOCK), (0, 0), (0, 0))
    kp = jnp.pad(k, pad).reshape(B, nb + 2, BLOCK, ATT_KV_HEADS, HEAD_DIM)
    vp = jnp.pad(v, pad).reshape(B, nb + 2, BLOCK, ATT_KV_HEADS, HEAD_DIM)
    kb = jnp.concatenate([kp[:, :-2], kp[:, 1:-1], kp[:, 2:]], axis=2)
    vb = jnp.concatenate([vp[:, :-2], vp[:, 1:-1], vp[:, 2:]], axis=2)
    rel = jnp.arange(3 * BLOCK)[None, :] - BLOCK - jnp.arange(BLOCK)[:, None]
    bias = rel_table.astype(F32)[_t5_bucket(rel)]
    bias = jnp.transpose(bias, (2, 0, 1)).reshape(ATT_KV_HEADS, ATT_GROUP, BLOCK, 3 * BLOCK)
    key_pos = jnp.arange(nb)[:, None] * BLOCK + jnp.arange(3 * BLOCK)[None, :] - BLOCK
    valid = (jnp.abs(rel) <= WINDOW)[None] & ((key_pos >= 0) & (key_pos < S))[:, None, :]
    s = jnp.einsum('bnqhgd,bnkhd->bnhgqk', qb, kb).astype(F32) * (HEAD_DIM ** -0.5) + bias
    s = jnp.where(valid[None, :, None, None], s, -jnp.inf)
    sink_l = jnp.broadcast_to(sink.astype(F32).reshape(1, 1, ATT_KV_HEADS, ATT_GROUP, 1, 1), s.shape[:-1] + (1,))
    p = jax.nn.softmax(jnp.concatenate([s, sink_l], axis=-1), axis=-1)[..., :-1]
    o = jnp.einsum('bnhgqk,bnkhd->bnqhgd', p.astype(v.dtype), vb)
    return o.reshape(B, S, ATT_Q)


def _retention_dir(q, k, v, log_gamma, strict):
    B, S = q.shape[0], q.shape[1]
    C = RET_CHUNK
    n = S // C
    pos = jnp.arange(C, dtype=F32)
    diff = pos[:, None] - pos[None, :]
    mask = (diff > 0) if strict else (diff >= 0)
    decay = jnp.where(mask[None], jnp.exp(jnp.maximum(diff, 0.0)[None] * log_gamma[:, None, None]), 0.0)
    q_dec = jnp.exp((pos[:, None] + 1.0) * log_gamma[None, :])
    k_dec = jnp.exp((C - 1.0 - pos)[:, None] * log_gamma[None, :])
    c_dec = jnp.exp(C * log_gamma)

    def to_chunks(t):
        return jnp.moveaxis(t.reshape((B, n, C) + t.shape[2:]), 1, 0)

    def step(state, inp):
        qc, kc, vc = inp
        intra = jnp.einsum('bqhd,bkhd->bhqk', qc, kc) * decay
        out = (jnp.einsum('bhqk,bkhe->bqhe', intra, vc)
               + jnp.einsum('bqhd,bhde->bqhe', qc * q_dec[None, :, :, None], state))
        state = state * c_dec[None, :, None, None] + jnp.einsum('bkhd,bkhe->bhde', kc * k_dec[None, :, :, None], vc)
        return state, out

    init = jnp.zeros((B, RET_HEADS, RET_DK, RET_DV), F32)
    _, out = lax.scan(step, init, (to_chunks(q), to_chunks(k), to_chunks(v)))
    return jnp.moveaxis(out, 0, 1).reshape(B, S, RET_HEADS, RET_DV)


def _bidir_retention(q, k, v, theta_fwd, theta_bwd):
    lg_f = -jnp.exp(theta_fwd.astype(F32))
    lg_b = -jnp.exp(theta_bwd.astype(F32))
    fwd = _retention_dir(q, k, v, lg_f, False)
    bwd = jnp.flip(_retention_dir(jnp.flip(q, 1), jnp.flip(k, 1), jnp.flip(v, 1), lg_b, True), 1)
    return fwd + bwd


def _route(h, w_router, e_bias):
    T = h.shape[0]
    scores = jax.nn.sigmoid(h.astype(F32) @ w_router.astype(F32))
    biased = scores + e_bias.astype(F32)[None, :]
    grp = biased.reshape(T, N_GROUPS, N_EXPERTS // N_GROUPS)
    grp_score = jnp.sum(lax.top_k(grp, 2)[0], axis=-1)
    _, top_g = lax.top_k(grp_score, TOPK_GROUPS)
    gmask = jnp.any(top_g[:, :, None] == jnp.arange(N_GROUPS)[None, None, :], axis=1)
    emask = jnp.repeat(gmask, N_EXPERTS // N_GROUPS, axis=1)
    _, idx = lax.top_k(jnp.where(emask, biased, -jnp.inf), TOP_K)
    w = jnp.take_along_axis(scores, idx, axis=1)
    w = w / jnp.sum(w, axis=-1, keepdims=True) * ROUTED_SCALE
    return idx.astype(jnp.int32), w


def _routed_experts(h, idx, gate, w_gate, w_up, w_down):
    T, D = h.shape
    A = T * TOP_K
    flat_e = idx.reshape(-1)
    flat_t = jnp.arange(A, dtype=jnp.int32) // TOP_K
    flat_g = gate.reshape(-1)
    order = jnp.argsort(flat_e)
    e_s, t_s, g_s = flat_e[order], flat_t[order], flat_g[order]
    counts = jnp.bincount(flat_e, length=N_EXPERTS).astype(jnp.int32)
    padded = (counts + MOE_BLOCK - 1) // MOE_BLOCK * MOE_BLOCK
    pad_end = jnp.cumsum(padded)
    pad_start = pad_end - padded
    start = jnp.cumsum(counts) - counts
    slot = pad_start[e_s] + jnp.arange(A, dtype=jnp.int32) - start[e_s]
    n_blocks = -(-A // MOE_BLOCK) + N_EXPERTS
    P = n_blocks * MOE_BLOCK
    slot_tok = jnp.full((P,), T, jnp.int32).at[slot].set(t_s)
    slot_gate = jnp.zeros((P,), F32).at[slot].set(g_s.astype(F32))
    block_e = jnp.minimum(jnp.searchsorted(pad_end, jnp.arange(n_blocks, dtype=jnp.int32) * MOE_BLOCK, side='right'),
                          N_EXPERTS - 1).astype(jnp.int32)
    h_pad = jnp.concatenate([h, jnp.zeros((1, D), h.dtype)], axis=0)

    def step(acc, inp):
        tok, g, e = inp
        xb = h_pad[tok]
        hid = jax.nn.silu(xb @ w_gate[e]) * (xb @ w_up[e])
        yb = (hid @ w_down[e]).astype(F32) * g[:, None]
        return acc.at[tok].add(yb), None

    acc, _ = lax.scan(step, jnp.zeros((T + 1, D), F32),
                      (slot_tok.reshape(n_blocks, MOE_BLOCK), slot_gate.reshape(n_blocks, MOE_BLOCK), block_e))
    return acc[:T].astype(h.dtype)


def _encoder(x, rel_table, w_in, attn_sink, ret_decay, ret_gn_w, w_att_out, w_ret_out, w_o, ln1_w, ln1_b,
             w_router, router_bias, w_exp_gate, w_exp_up, w_exp_down, w_sh_gate, w_sh_up, w_sh_down, ln2_w, ln2_b):
    B, S, D = x.shape
    pos = jnp.arange(S, dtype=F32)
    for l in range(DEPTH):
        proj = x @ w_in[l]
        aq, ak, av, rq, rk, rv, rg, ga, gr = jnp.split(proj, SPLITS, axis=-1)
        att = _window_attention(aq.reshape(B, S, ATT_HEADS, HEAD_DIM), ak.reshape(B, S, ATT_KV_HEADS, HEAD_DIM),
                                av.reshape(B, S, ATT_KV_HEADS, HEAD_DIM), attn_sink[l], rel_table)
        rq = _rope(rq.reshape(B, S, RET_HEADS, RET_DK).astype(F32), pos)
        rk = _rope(rk.reshape(B, S, RET_HEADS, RET_DK).astype(F32), pos) * (RET_DK ** -0.5)
        rv = rv.reshape(B, S, RET_HEADS, RET_DV).astype(F32)
        ret = _bidir_retention(rq, rk, rv, ret_decay[l, 0], ret_decay[l, 1])
        mu = jnp.mean(ret, axis=-1, keepdims=True)
        var = jnp.mean(jnp.square(ret - mu), axis=-1, keepdims=True)
        ret = ((ret - mu) * lax.rsqrt(var + GN_EPS)).reshape(B, S, RET_V) * ret_gn_w[l].astype(F32)
        ret = (jax.nn.silu(rg.astype(F32)) * ret).astype(x.dtype)
        merged = jax.nn.sigmoid(ga) * (att @ w_att_out[l]) + jax.nn.sigmoid(gr) * (ret @ w_ret_out[l])
        x = _layernorm(ALPHA * x + merged @ w_o[l], ln1_w[l], ln1_b[l])
        h = x.reshape(B * S, D)
        idx, gate = _route(h, w_router[l], router_bias[l])
        routed = _routed_experts(h, idx, gate, w_exp_gate[l], w_exp_up[l], w_exp_down[l])
        shared = (jax.nn.silu(h @ w_sh_gate[l]) * (h @ w_sh_up[l])) @ w_sh_down[l]
        x = _layernorm(ALPHA * x + (routed + shared).reshape(B, S, D), ln2_w[l], ln2_b[l])
    return x


def setup_inputs(seed: int = 0) -> dict:
    key = jax.random.key(seed)
    ks = jax.random.split(key, 24)

    def nrm(k, shape, scale):
        return jax.random.normal(k, shape, F32) * scale

    hh = np.arange(RET_HEADS, dtype=np.float64)
    base_theta = jnp.asarray(np.log(-np.log1p(-2.0 ** (-5.0 - hh))).astype(np.float32))
    return {
        "x_prompt": nrm(ks[0], (BATCH, SEQ, D_MODEL), 1.0),
        "x_sample": nrm(ks[1], (DEC_BATCH, DEC_SEQ, D_MODEL), 1.0),
        "rel_table": nrm(ks[2], (REL_BUCKETS, ATT_HEADS), 0.5),
        "w_in": nrm(ks[3], (DEPTH, D_MODEL, IN_WIDTH), D_MODEL ** -0.5),
        "attn_sink": nrm(ks[4], (DEPTH, ATT_HEADS), 0.5),
        "ret_decay": base_theta[None, None, :] + nrm(ks[5], (DEPTH, 2, RET_HEADS), 0.1),
        "ret_gn_w": 1.0 + nrm(ks[6], (DEPTH, RET_V), 0.02),
        "w_att_out": nrm(ks[7], (DEPTH, ATT_Q, D_MODEL), ATT_Q ** -0.5 * BETA),
        "w_ret_out": nrm(ks[8], (DEPTH, RET_V, D_MODEL), RET_V ** -0.5 * BETA),
        "w_o": nrm(ks[9], (DEPTH, D_MODEL, D_MODEL), D_MODEL ** -0.5 * BETA),
        "ln1_w": 1.0 + nrm(ks[10], (DEPTH, D_MODEL), 0.02),
        "ln1_b": nrm(ks[11], (DEPTH, D_MODEL), 0.02),
        "w_router": nrm(ks[12], (DEPTH, D_MODEL, N_EXPERTS), D_MODEL ** -0.5),
        "router_bias": nrm(ks[13], (DEPTH, N_EXPERTS), 0.01),
        "w_exp_gate": nrm(ks[14], (DEPTH, N_EXPERTS, D_MODEL, D_EXPERT), D_MODEL ** -0.5),
        "w_exp_up": nrm(ks[15], (DEPTH, N_EXPERTS, D_MODEL, D_EXPERT), D_MODEL ** -0.5),
        "w_exp_down": nrm(ks[16], (DEPTH, N_EXPERTS, D_EXPERT, D_MODEL), D_EXPERT ** -0.5 * BETA),
        "w_sh_gate": nrm(ks[17], (DEPTH, D_MODEL, D_SHARED), D_MODEL ** -0.5),
        "w_sh_up": nrm(ks[18], (DEPTH, D_MODEL, D_SHARED), D_MODEL ** -0.5),
        "w_sh_down": nrm(ks[19], (DEPTH, D_SHARED, D_MODEL), D_SHARED ** -0.5 * BETA),
        "ln2_w": 1.0 + nrm(ks[20], (DEPTH, D_MODEL), 0.02),
        "ln2_b": nrm(ks[21], (DEPTH, D_MODEL), 0.02),
    }


def reference(x_prompt, x_sample, rel_table, w_in, attn_sink, ret_decay, ret_gn_w, w_att_out, w_ret_out, w_o,
              ln1_w, ln1_b, w_router, router_bias, w_exp_gate, w_exp_up, w_exp_down, w_sh_gate, w_sh_up, w_sh_down,
              ln2_w, ln2_b):
    y_prompt = _encoder(x_prompt, rel_table, w_in, attn_sink, ret_decay, ret_gn_w, w_att_out, w_ret_out, w_o,
                        ln1_w, ln1_b, w_router, router_bias, w_exp_gate, w_exp_up, w_exp_down,
                        w_sh_gate, w_sh_up, w_sh_down, ln2_w, ln2_b)
    y_sample = _encoder(x_sample, rel_table, w_in, attn_sink, ret_decay, ret_gn_w, w_att_out, w_ret_out, w_o,
                        ln1_w, ln1_b, w_router, router_bias, w_exp_gate, w_exp_up, w_exp_down,
                        w_sh_gate, w_sh_up, w_sh_down, ln2_w, ln2_b)
    return (y_prompt, y_sample)
```

```python
import functools
import math

import numpy as np
import jax
import jax.numpy as jnp
from jax import lax
from jax.experimental import pallas as pl
from jax.experimental.pallas import tpu as pltpu

F32 = jnp.float32
BF16 = jnp.bfloat16
I32 = jnp.int32

HEAD_DIM = 128
ATT_HEADS = 16
ATT_KV_HEADS = 4
WINDOW = 128
BLOCK = 128
REL_BUCKETS = 32
REL_MAX_DIST = 128
RET_HEADS = 8
RET_DK = 128
RET_DV = 256
RET_CHUNK = 128
ROPE_BASE = 10000.0
N_EXPERTS = 128
TOP_K = 8
N_GROUPS = 8
TOPK_GROUPS = 4
ROUTED_SCALE = 2.5
LN_EPS = 1e-5
GN_EPS = 1e-5
DEPTH = 1
ALPHA = (2.0 * DEPTH) ** 0.25

V7X_VMEM_BUDGET_BYTES = 56 * 1024 * 1024
LANES = 128

NT_DIMS = (((1,), (1,)), ((), ()))
TN_DIMS = (((0,), (0,)), ((), ()))


def _tile(dim, pref):
    t = min(dim, pref)
    assert dim % t == 0, (dim, pref)
    return t


def _params(sem, vmem=V7X_VMEM_BUDGET_BYTES):
    return pltpu.CompilerParams(dimension_semantics=sem, vmem_limit_bytes=vmem)


def _silu(v):
    return v * jax.nn.sigmoid(v)


def _matmul_kernel(x_ref, w_ref, o_ref):
    o_ref[...] = jnp.dot(x_ref[...], w_ref[...], preferred_element_type=F32).astype(o_ref.dtype)


def _matmul(x, w, out_dtype, tm=1024, tn=1024):
    M, K = x.shape
    N = w.shape[1]
    tm, tn = _tile(M, tm), _tile(N, tn)
    return pl.pallas_call(
        _matmul_kernel,
        grid=(M // tm, N // tn),
        in_specs=[pl.BlockSpec((tm, K), lambda i, j: (i, 0)),
                  pl.BlockSpec((K, tn), lambda i, j: (0, j))],
        out_specs=pl.BlockSpec((tm, tn), lambda i, j: (i, j)),
        out_shape=jax.ShapeDtypeStruct((M, N), out_dtype),
        compiler_params=_params(("parallel", "arbitrary")),
        name="proj_matmul",
    )(x, w)


def _attn_kernel(first_ref, last_ref, sink_ref, q_ref, kp_ref, kc_ref, kn_ref, vp_ref, vc_ref, vn_ref,
                 bias_ref, o_ref, *, group, scale):
    n = pl.program_id(0)
    h = pl.program_id(1)
    neg_inf = jnp.float32(-jnp.inf)
    pen_p = jnp.where(first_ref[n] == 1, neg_inf, jnp.float32(0.0))
    pen_n = jnp.where(last_ref[n] == 1, neg_inf, jnp.float32(0.0))
    kp, kc, kn = (r[...].astype(BF16) for r in (kp_ref, kc_ref, kn_ref))
    vp, vc, vn = (r[...].astype(BF16) for r in (vp_ref, vc_ref, vn_ref))
    for g in range(group):
        q = q_ref[:, g * HEAD_DIM:(g + 1) * HEAD_DIM].astype(BF16)
        sp = lax.dot_general(q, kp, NT_DIMS, preferred_element_type=F32) * scale + bias_ref[g, :, 0:BLOCK] + pen_p
        sc = lax.dot_general(q, kc, NT_DIMS, preferred_element_type=F32) * scale + bias_ref[g, :, BLOCK:2 * BLOCK]
        sn = lax.dot_general(q, kn, NT_DIMS, preferred_element_type=F32) * scale + bias_ref[g, :, 2 * BLOCK:3 * BLOCK] + pen_n
        sink = sink_ref[h * group + g]
        m = jnp.maximum(jnp.maximum(jnp.max(sp, axis=1, keepdims=True), jnp.max(sc, axis=1, keepdims=True)),
                        jnp.max(sn, axis=1, keepdims=True))
        m = jnp.maximum(m, sink)
        ep, ec, en = jnp.exp(sp - m), jnp.exp(sc - m), jnp.exp(sn - m)
        denom = (jnp.sum(ep, axis=1, keepdims=True) + jnp.sum(ec, axis=1, keepdims=True)
                 + jnp.sum(en, axis=1, keepdims=True) + jnp.exp(sink - m))
        r = 1.0 / denom
        o = (jnp.dot((ep * r).astype(BF16), vp, preferred_element_type=F32)
             + jnp.dot((ec * r).astype(BF16), vc, preferred_element_type=F32)
             + jnp.dot((en * r).astype(BF16), vn, preferred_element_type=F32))
        o_ref[:, g * HEAD_DIM:(g + 1) * HEAD_DIM] = o.astype(o_ref.dtype)


def _t5_bucket(rel):
    nb = REL_BUCKETS // 2
    max_exact = nb // 2
    ret = (rel > 0).astype(I32) * nb
    n = jnp.abs(rel)
    nf = jnp.maximum(n, max_exact).astype(F32)
    large = max_exact + (jnp.log(nf / max_exact) / math.log(REL_MAX_DIST / max_exact) * (nb - max_exact)).astype(I32)
    large = jnp.minimum(large, nb - 1)
    return ret + jnp.where(n < max_exact, n, large)


def _attention(proj, first, last, sink, rel_table, col_q, col_k, col_v):
    T = proj.shape[0]
    nb = T // BLOCK
    group = ATT_HEADS // ATT_KV_HEADS
    qw = group * HEAD_DIM
    assert col_q % qw == 0 and col_k % HEAD_DIM == 0 and col_v % HEAD_DIM == 0
    qo, ko, vo = col_q // qw, col_k // HEAD_DIM, col_v // HEAD_DIM
    rel = jnp.arange(3 * BLOCK)[None, :] - BLOCK - jnp.arange(BLOCK)[:, None]
    bias = rel_table.astype(F32)[_t5_bucket(rel)]
    bias = jnp.where((jnp.abs(rel) <= WINDOW)[:, :, None], bias, -jnp.inf)
    bias = jnp.transpose(bias, (2, 0, 1))

    def kv_spec(off, shift):
        def imap(n, h, first, last):
            return (jnp.clip(n + shift, 0, nb - 1), off + h)
        return pl.BlockSpec((BLOCK, HEAD_DIM), imap)

    grid_spec = pltpu.PrefetchScalarGridSpec(
        num_scalar_prefetch=2,
        grid=(nb, ATT_KV_HEADS),
        in_specs=[pl.BlockSpec(memory_space=pltpu.SMEM),
                  pl.BlockSpec((BLOCK, qw), lambda n, h, first, last: (n, qo + h)),
                  kv_spec(ko, -1), kv_spec(ko, 0), kv_spec(ko, 1),
                  kv_spec(vo, -1), kv_spec(vo, 0), kv_spec(vo, 1),
                  pl.BlockSpec((group, BLOCK, 3 * BLOCK), lambda n, h, first, last: (h, 0, 0))],
        out_specs=pl.BlockSpec((BLOCK, qw), lambda n, h, first, last: (n, h)),
    )
    return pl.pallas_call(
        functools.partial(_attn_kernel, group=group, scale=HEAD_DIM ** -0.5),
        grid_spec=grid_spec,
        out_shape=jax.ShapeDtypeStruct((T, ATT_HEADS * HEAD_DIM), BF16),
        compiler_params=_params(("parallel", "arbitrary")),
        name="window_attention",
    )(first, last, sink.astype(F32), proj, proj, proj, proj, proj, proj, proj, bias)


def _rope(x, cos2, sin2):
    return x * cos2 + pltpu.roll(x, RET_DK // 2, axis=1) * sin2


def _ret_bwd_kernel(last_ref, posblk_ref, cdec_ref, q_ref, k_ref, v_ref, cos_ref, sin_ref, qb_ref, kb_ref, o_ref, state_ref):
    h = pl.program_id(0)
    c = pl.program_id(1)
    n = pl.num_programs(1) - 1 - c

    @pl.when(last_ref[n] == 1)
    def _():
        state_ref[...] = jnp.zeros_like(state_ref)

    cos2, sin2 = cos_ref[...], sin_ref[...]
    q = _rope(q_ref[...], cos2, sin2)
    k = _rope(k_ref[...], cos2, sin2) * (RET_DK ** -0.5)
    vb = v_ref[...].astype(BF16)
    state = state_ref[...]
    o_ref[...] = jnp.dot((q * qb_ref[0]).astype(BF16), state.astype(BF16), preferred_element_type=F32)
    state_ref[...] = state * cdec_ref[1, h] + lax.dot_general((k * kb_ref[0]).astype(BF16), vb, TN_DIMS,
                                                              preferred_element_type=F32)


def _ret_fwd_kernel(first_ref, posblk_ref, cdec_ref, q_ref, k_ref, v_ref, g_ref, cos_ref, sin_ref, dmat_ref, qf_ref, kf_ref,
                    ob_ref, gnw_ref, o_ref, state_ref):
    h = pl.program_id(0)
    n = pl.program_id(1)

    @pl.when(first_ref[n] == 1)
    def _():
        state_ref[...] = jnp.zeros_like(state_ref)

    cos2, sin2 = cos_ref[...], sin_ref[...]
    q = _rope(q_ref[...], cos2, sin2)
    k = _rope(k_ref[...], cos2, sin2) * (RET_DK ** -0.5)
    vb = v_ref[...].astype(BF16)
    state = state_ref[...]
    intra = lax.dot_general(q.astype(BF16), k.astype(BF16), NT_DIMS, preferred_element_type=F32) * dmat_ref[0]
    out = (jnp.dot(intra.astype(BF16), vb, preferred_element_type=F32)
           + jnp.dot((q * qf_ref[0]).astype(BF16), state.astype(BF16), preferred_element_type=F32)
           + ob_ref[...])
    state_ref[...] = state * cdec_ref[0, h] + lax.dot_general((k * kf_ref[0]).astype(BF16), vb, TN_DIMS,
                                                              preferred_element_type=F32)
    mu = jnp.mean(out, axis=1, keepdims=True)
    cen = out - mu
    var = jnp.mean(cen * cen, axis=1, keepdims=True)
    y = cen * lax.rsqrt(var + GN_EPS) * gnw_ref[...]
    o_ref[...] = (_silu(g_ref[...]) * y).astype(o_ref.dtype)


def _retention(proj, first, last, posblk, ret_decay, ret_gn_w, s_max, col_q, col_k, col_v, col_g):
    T = proj.shape[0]
    C = RET_CHUNK
    nc = T // C
    H = RET_HEADS
    qo, ko, vo, go = col_q // RET_DK, col_k // RET_DK, col_v // RET_DV, col_g // RET_DV
    assert col_q % RET_DK == 0 and col_k % RET_DK == 0 and col_v % RET_DV == 0 and col_g % RET_DV == 0

    half = RET_DK // 2
    inv = ROPE_BASE ** (-jnp.arange(half, dtype=F32) * 2.0 / RET_DK)
    ang = jnp.arange(s_max, dtype=F32)[:, None] * inv[None, :]
    cos2 = jnp.concatenate([jnp.cos(ang), jnp.cos(ang)], axis=1)
    sin2 = jnp.concatenate([-jnp.sin(ang), jnp.sin(ang)], axis=1)

    lg = -jnp.exp(ret_decay.astype(F32))
    pos = jnp.arange(C, dtype=F32)
    diff = pos[:, None] - pos[None, :]
    dec_f = jnp.where((diff >= 0)[None], jnp.exp(jnp.maximum(diff, 0.0)[None] * lg[0][:, None, None]), 0.0)
    dec_b = jnp.where((diff < 0)[None], jnp.exp(jnp.maximum(-diff, 0.0)[None] * lg[1][:, None, None]), 0.0)
    dmat = dec_f + dec_b

    def rows(tab):
        return jnp.broadcast_to(tab[:, :, None], (H, C, RET_DK))

    qf = rows(jnp.exp((pos[None, :] + 1.0) * lg[0][:, None]))
    kf = rows(jnp.exp((C - 1.0 - pos)[None, :] * lg[0][:, None]))
    qb = rows(jnp.exp((C - pos)[None, :] * lg[1][:, None]))
    kb = rows(jnp.exp(pos[None, :] * lg[1][:, None]))
    cdec = jnp.exp(C * lg)

    smem = pl.BlockSpec(memory_space=pltpu.SMEM)

    def bwd_specs():
        rev = lambda c: nc - 1 - c
        return [
            smem,
            pl.BlockSpec((C, RET_DK), lambda h, c, last, posb: (rev(c), qo + h)),
            pl.BlockSpec((C, RET_DK), lambda h, c, last, posb: (rev(c), ko + h)),
            pl.BlockSpec((C, RET_DV), lambda h, c, last, posb: (rev(c), vo + h)),
            pl.BlockSpec((C, RET_DK), lambda h, c, last, posb: (posb[rev(c)], 0)),
            pl.BlockSpec((C, RET_DK), lambda h, c, last, posb: (posb[rev(c)], 0)),
            pl.BlockSpec((1, C, RET_DK), lambda h, c, last, posb: (h, 0, 0)),
            pl.BlockSpec((1, C, RET_DK), lambda h, c, last, posb: (h, 0, 0)),
        ], pl.BlockSpec((C, RET_DV), lambda h, c, last, posb: (rev(c), h))

    in_specs, out_spec = bwd_specs()
    out_b = pl.pallas_call(
        _ret_bwd_kernel,
        grid_spec=pltpu.PrefetchScalarGridSpec(
            num_scalar_prefetch=2, grid=(H, nc), in_specs=in_specs, out_specs=out_spec,
            scratch_shapes=[pltpu.VMEM((RET_DK, RET_DV), F32)]),
        out_shape=jax.ShapeDtypeStruct((T, H * RET_DV), F32),
        compiler_params=_params(("parallel", "arbitrary")),
        name="retention_backward",
    )(last, posblk, cdec, proj, proj, proj, cos2, sin2, qb, kb)

    fwd_specs = [
        smem,
        pl.BlockSpec((C, RET_DK), lambda h, n, first, posb: (n, qo + h)),
        pl.BlockSpec((C, RET_DK), lambda h, n, first, posb: (n, ko + h)),
        pl.BlockSpec((C, RET_DV), lambda h, n, first, posb: (n, vo + h)),
        pl.BlockSpec((C, RET_DV), lambda h, n, first, posb: (n, go + h)),
        pl.BlockSpec((C, RET_DK), lambda h, n, first, posb: (posb[n], 0)),
        pl.BlockSpec((C, RET_DK), lambda h, n, first, posb: (posb[n], 0)),
        pl.BlockSpec((1, C, C), lambda h, n, first, posb: (h, 0, 0)),
        pl.BlockSpec((1, C, RET_DK), lambda h, n, first, posb: (h, 0, 0)),
        pl.BlockSpec((1, C, RET_DK), lambda h, n, first, posb: (h, 0, 0)),
        pl.BlockSpec((C, RET_DV), lambda h, n, first, posb: (n, h)),
        pl.BlockSpec((1, RET_DV), lambda h, n, first, posb: (0, h)),
    ]
    return pl.pallas_call(
        _ret_fwd_kernel,
        grid_spec=pltpu.PrefetchScalarGridSpec(
            num_scalar_prefetch=2, grid=(H, nc), in_specs=fwd_specs,
            out_specs=pl.BlockSpec((C, RET_DV), lambda h, n, first, posb: (n, h)),
            scratch_shapes=[pltpu.VMEM((RET_DK, RET_DV), F32)]),
        out_shape=jax.ShapeDtypeStruct((T, H * RET_DV), BF16),
        compiler_params=_params(("parallel", "arbitrary")),
        name="retention_forward",
    )(first, posblk, cdec, proj, proj, proj, proj, cos2, sin2, dmat, qf, kf, out_b,
      ret_gn_w.astype(F32).reshape(1, -1))


def _merge_kernel(att_ref, wa_ref, ret_ref, wr_ref, ga_ref, gr_ref, o_ref):
    a = jnp.dot(att_ref[...], wa_ref[...], preferred_element_type=F32)
    r = jnp.dot(ret_ref[...], wr_ref[...], preferred_element_type=F32)
    o_ref[...] = (jax.nn.sigmoid(ga_ref[...]) * a + jax.nn.sigmoid(gr_ref[...]) * r).astype(o_ref.dtype)


def _merge(att, ret, proj, w_att_out, w_ret_out, col_ga, col_gr, tm=1024, tn=512):
    T = att.shape[0]
    D = w_att_out.shape[1]
    tm, tn = _tile(T, tm), _tile(D, tn)
    assert col_ga % tn == 0 and col_gr % tn == 0
    ao, ro = col_ga // tn, col_gr // tn
    return pl.pallas_call(
        _merge_kernel,
        grid=(T // tm, D // tn),
        in_specs=[pl.BlockSpec((tm, att.shape[1]), lambda i, j: (i, 0)),
                  pl.BlockSpec((w_att_out.shape[0], tn), lambda i, j: (0, j)),
                  pl.BlockSpec((tm, ret.shape[1]), lambda i, j: (i, 0)),
                  pl.BlockSpec((w_ret_out.shape[0], tn), lambda i, j: (0, j)),
                  pl.BlockSpec((tm, tn), lambda i, j: (i, ao + j)),
                  pl.BlockSpec((tm, tn), lambda i, j: (i, ro + j))],
        out_specs=pl.BlockSpec((tm, tn), lambda i, j: (i, j)),
        out_shape=jax.ShapeDtypeStruct((T, D), BF16),
        compiler_params=_params(("parallel", "arbitrary")),
        name="branch_merge",
    )(att, w_att_out, ret, w_ret_out, proj, proj)


def _layernorm_rows(z, w, b):
    mu = jnp.mean(z, axis=1, keepdims=True)
    cen = z - mu
    var = jnp.mean(cen * cen, axis=1, keepdims=True)
    return cen * lax.rsqrt(var + LN_EPS) * w + b


def _wo_ln_kernel(m_ref, w_ref, x_ref, lw_ref, lb_ref, o_ref, acc_ref):
    k = pl.program_id(1)

    @pl.when(k == 0)
    def _():
        acc_ref[...] = jnp.zeros_like(acc_ref)

    acc_ref[...] += jnp.dot(m_ref[...], w_ref[...], preferred_element_type=F32)

    @pl.when(k == pl.num_programs(1) - 1)
    def _():
        o_ref[...] = _layernorm_rows(ALPHA * x_ref[...] + acc_ref[...], lw_ref[...], lb_ref[...])


def _wo_layernorm(merged, w_o, x, ln_w, ln_b, tm=256, tk=512):
    T, D = x.shape
    K = merged.shape[1]
    tm, tk = _tile(T, tm), _tile(K, tk)
    return pl.pallas_call(
        _wo_ln_kernel,
        grid=(T // tm, K // tk),
        in_specs=[pl.BlockSpec((tm, tk), lambda i, k: (i, k)),
                  pl.BlockSpec((tk, D), lambda i, k: (k, 0)),
                  pl.BlockSpec((tm, D), lambda i, k: (i, 0)),
                  pl.BlockSpec((1, D), lambda i, k: (0, 0)),
                  pl.BlockSpec((1, D), lambda i, k: (0, 0))],
        out_specs=pl.BlockSpec((tm, D), lambda i, k: (i, 0)),
        out_shape=jax.ShapeDtypeStruct((T, D), F32),
        scratch_shapes=[pltpu.VMEM((tm, D), F32)],
        compiler_params=_params(("parallel", "arbitrary")),
        name="out_proj_layernorm",
    )(merged, w_o, x, ln_w.astype(F32).reshape(1, -1), ln_b.astype(F32).reshape(1, -1))


def _router_kernel(x_ref, wh_ref, wl_ref, bias_ref, idx_ref, rank_ref, gate_ref, cnt_ref, xb_ref, carry_ref):
    i = pl.program_id(0)
    tm = x_ref.shape[0]
    E = N_EXPERTS
    gsz = E // N_GROUPS
    neg_inf = jnp.float32(-jnp.inf)

    @pl.when(i == 0)
    def _():
        carry_ref[...] = jnp.zeros_like(carry_ref)

    x = x_ref[...]
    xh = x.astype(BF16)
    xb_ref[...] = xh
    xl = (x - xh.astype(F32)).astype(BF16)
    logits = (lax.dot_general(wh_ref[...], xh, NT_DIMS, preferred_element_type=F32)
              + lax.dot_general(wh_ref[...], xl, NT_DIMS, preferred_element_type=F32)
              + lax.dot_general(wl_ref[...], xh, NT_DIMS, preferred_element_type=F32))
    scores = jax.nn.sigmoid(logits)
    biased = scores + bias_ref[...]

    row = lax.broadcasted_iota(I32, (gsz, tm), 0)
    gscore = []
    for g in range(N_GROUPS):
        blk = biased[g * gsz:(g + 1) * gsz, :]
        m1 = jnp.max(blk, axis=0, keepdims=True)
        first = jnp.min(jnp.where(blk == m1, row, gsz), axis=0, keepdims=True)
        m2 = jnp.max(jnp.where(row == first, neg_inf, blk), axis=0, keepdims=True)
        gscore.append(m1 + m2)
    gid = lax.broadcasted_iota(I32, (E, tm), 0) // gsz
    keep = jnp.zeros((E, tm), I32)
    for g in range(N_GROUPS):
        beaten = jnp.zeros((1, tm), I32)
        for o in range(N_GROUPS):
            if o == g:
                continue
            wins = (gscore[o] >= gscore[g]) if o < g else (gscore[o] > gscore[g])
            beaten = beaten + wins.astype(I32)
        keep = jnp.where(gid == g, (beaten < TOPK_GROUPS).astype(I32), keep)
    masked = jnp.where(keep > 0, biased, neg_inf)

    eidx = lax.broadcasted_iota(I32, (E, tm), 0)
    sel = jnp.zeros((E, tm), jnp.bool_)
    picks, weights = [], []
    for _ in range(TOP_K):
        m = jnp.max(masked, axis=0, keepdims=True)
        pick = jnp.min(jnp.where(masked == m, eidx, E), axis=0, keepdims=True)
        onehot = eidx == pick
        weights.append(jnp.sum(jnp.where(onehot, scores, 0.0), axis=0, keepdims=True))
        picks.append(pick)
        masked = jnp.where(onehot, neg_inf, masked)
        sel = sel | onehot
    wsum = weights[0]
    for w in weights[1:]:
        wsum = wsum + w

    self32 = sel.astype(F32)
    tri = (lax.broadcasted_iota(I32, (tm, tm), 0) < lax.broadcasted_iota(I32, (tm, tm), 1)).astype(F32).astype(BF16)
    prefix = jnp.dot(self32.astype(BF16), tri, preferred_element_type=F32) + carry_ref[...]
    for kk in range(TOP_K):
        onehot = eidx == picks[kk]
        idx_ref[kk:kk + 1, :] = picks[kk]
        rank_ref[kk:kk + 1, :] = jnp.sum(jnp.where(onehot, prefix, 0.0), axis=0, keepdims=True).astype(I32)
        gate_ref[kk:kk + 1, :] = weights[kk] / wsum * ROUTED_SCALE
    carry_ref[...] = carry_ref[...] + jnp.sum(self32, axis=1, keepdims=True)
    cnt_ref[...] = carry_ref[...]


def _router(x1, w_router, router_bias, tm=256):
    T, D = x1.shape
    E = N_EXPERTS
    tm = _tile(T, tm)
    wt = w_router.astype(F32).T
    wh = wt.astype(BF16)
    wl = (wt - wh.astype(F32)).astype(BF16)
    return pl.pallas_call(
        _router_kernel,
        grid=(T // tm,),
        in_specs=[pl.BlockSpec((tm, D), lambda i: (i, 0)),
                  pl.BlockSpec((E, D), lambda i: (0, 0)),
                  pl.BlockSpec((E, D), lambda i: (0, 0)),
                  pl.BlockSpec((E, 1), lambda i: (0, 0))],
        out_specs=[pl.BlockSpec((TOP_K, tm), lambda i: (0, i)),
                   pl.BlockSpec((TOP_K, tm), lambda i: (0, i)),
                   pl.BlockSpec((TOP_K, tm), lambda i: (0, i)),
                   pl.BlockSpec((E, 1), lambda i: (0, 0)),
                   pl.BlockSpec((tm, D), lambda i: (i, 0))],
        out_shape=[jax.ShapeDtypeStruct((TOP_K, T), I32),
                   jax.ShapeDtypeStruct((TOP_K, T), I32),
                   jax.ShapeDtypeStruct((TOP_K, T), F32),
                   jax.ShapeDtypeStruct((E, 1), F32),
                   jax.ShapeDtypeStruct((T, D), BF16)],
        scratch_shapes=[pltpu.VMEM((E, 1), F32)],
        compiler_params=_params(("arbitrary",)),
        name="router",
    )(x1, wh, wl, router_bias.astype(F32).reshape(E, 1))


def _slot_kernel(start_ref, idx_ref, rank_ref, slot_ref):
    idx = idx_ref[...]
    acc = rank_ref[...]
    for e in range(N_EXPERTS):
        acc = acc + jnp.where(idx == e, start_ref[e], 0)
    slot_ref[...] = acc


def _slots(pad_start, idx, rank, tm=2048):
    T = idx.shape[1]
    tm = _tile(T, tm)
    return pl.pallas_call(
        _slot_kernel,
        grid=(T // tm,),
        in_specs=[pl.BlockSpec(memory_space=pltpu.SMEM),
                  pl.BlockSpec((TOP_K, tm), lambda i: (0, i)),
                  pl.BlockSpec((TOP_K, tm), lambda i: (0, i))],
        out_specs=pl.BlockSpec((TOP_K, tm), lambda i: (0, i)),
        out_shape=jax.ShapeDtypeStruct((TOP_K, T), I32),
        compiler_params=_params(("parallel",)),
        name="slot_index",
    )(pad_start, idx, rank)


SUBLANES = 8


def _zero_fill_plan(npad, start, te, fn):
    head = jnp.minimum(npad, (-start) & (SUBLANES - 1))
    for r in range(SUBLANES - 1):
        @pl.when(r < head)
        def _(r=r):
            fn(start + r, 1)

    pos = start + head
    rem = npad - head
    bit = te // 2
    while bit >= SUBLANES:
        take = (rem & bit) != 0

        @pl.when(take)
        def _(pos=pos, bit=bit):
            fn(pl.multiple_of(pos, SUBLANES), bit)

        pos = pos + jnp.where(take, bit, 0)
        bit //= 2


def _dispatch_kernel(fill_start_ref, fill_len_ref, slot_ref, x_ref, xs_ref, zero_ref, sem, zsem, *, te):
    i = pl.program_id(0)
    tm = x_ref.shape[0]

    def row_copy(a):
        t = a // TOP_K
        return pltpu.make_async_copy(x_ref.at[pl.ds(t, 1)], xs_ref.at[pl.ds(slot_ref[a], 1)], sem)

    def start_rows(a, carry):
        row_copy(a).start()
        return carry

    lax.fori_loop(0, tm * TOP_K, start_rows, 0)

    @pl.when(i == 0)
    def _():
        zero_ref[...] = jnp.zeros_like(zero_ref)

        def zero_copy(pos, n):
            return pltpu.make_async_copy(zero_ref.at[pl.ds(0, n)], xs_ref.at[pl.ds(pos, n)], zsem)

        def start_fill(e, carry):
            _zero_fill_plan(fill_len_ref[e], fill_start_ref[e], te, lambda pos, n: zero_copy(pos, n).start())
            return carry

        def wait_fill(e, carry):
            _zero_fill_plan(fill_len_ref[e], fill_start_ref[e], te, lambda pos, n: zero_copy(pos, n).wait())
            return carry

        lax.fori_loop(0, N_EXPERTS, start_fill, 0)
        lax.fori_loop(0, N_EXPERTS, wait_fill, 0)

    def wait_rows(a, carry):
        row_copy(a).wait()
        return carry

    lax.fori_loop(0, tm * TOP_K, wait_rows, 0)


def _dispatch(x1, slot_flat, fill_start, fill_len, n_rows, te, tm=256):
    T, D = x1.shape
    tm = _tile(T, tm)
    return pl.pallas_call(
        functools.partial(_dispatch_kernel, te=te),
        grid_spec=pltpu.PrefetchScalarGridSpec(
            num_scalar_prefetch=2,
            grid=(T // tm,),
            in_specs=[pl.BlockSpec((tm * TOP_K,), lambda i, fs, fl: (i,), memory_space=pltpu.SMEM),
                      pl.BlockSpec((tm, D), lambda i, fs, fl: (i, 0))],
            out_specs=pl.BlockSpec(memory_space=pl.ANY),
            scratch_shapes=[pltpu.VMEM((te, D), x1.dtype),
                            pltpu.SemaphoreType.DMA(()),
                            pltpu.SemaphoreType.DMA(())]),
        out_shape=jax.ShapeDtypeStruct((n_rows, D), x1.dtype),
        compiler_params=_params(("arbitrary",)),
        name="moe_dispatch",
    )(fill_start, fill_len, slot_flat, x1)


def _ffn_step(xb, wg, wu, wd, o_ref, j):
    hid = _silu(jnp.dot(xb, wg, preferred_element_type=F32)) * jnp.dot(xb, wu, preferred_element_type=F32)
    y = jnp.dot(hid.astype(BF16), wd, preferred_element_type=F32)

    @pl.when(j == 0)
    def _():
        o_ref[...] = y

    @pl.when(j > 0)
    def _():
        o_ref[...] += y


def _expert_kernel(blk_e_ref, nvalid_ref, x_ref, wg_ref, wu_ref, wd_ref, o_ref, xb_ref):
    b = pl.program_id(0)
    j = pl.program_id(1)

    @pl.when(b < nvalid_ref[0])
    def _():
        @pl.when(j == 0)
        def _():
            xb_ref[...] = x_ref[...].astype(BF16)

        _ffn_step(xb_ref[...], wg_ref[0], wu_ref[0], wd_ref[0], o_ref, j)


def _routed_ffn(xs, blk_e, nvalid, wg, wu, wd, te, tj=256):
    P, D = xs.shape
    DE = wg.shape[2]
    tj = _tile(DE, tj)
    nj = DE // tj
    nblk = P // te

    def bclamp(b, nv):
        return jnp.minimum(b, nv[0] - 1)

    def jclamp(b, j, nv):
        return jnp.where(b < nv[0], j, nj - 1)

    return pl.pallas_call(
        _expert_kernel,
        grid_spec=pltpu.PrefetchScalarGridSpec(
            num_scalar_prefetch=2,
            grid=(nblk, nj),
            in_specs=[pl.BlockSpec((te, D), lambda b, j, be, nv: (bclamp(b, nv), 0)),
                      pl.BlockSpec((1, D, tj), lambda b, j, be, nv: (be[bclamp(b, nv)], 0, jclamp(b, j, nv))),
                      pl.BlockSpec((1, D, tj), lambda b, j, be, nv: (be[bclamp(b, nv)], 0, jclamp(b, j, nv))),
                      pl.BlockSpec((1, tj, D), lambda b, j, be, nv: (be[bclamp(b, nv)], jclamp(b, j, nv), 0))],
            out_specs=pl.BlockSpec((te, D), lambda b, j, be, nv: (bclamp(b, nv), 0)),
            scratch_shapes=[pltpu.VMEM((te, D), BF16)]),
        out_shape=jax.ShapeDtypeStruct((P, D), F32),
        compiler_params=_params(("arbitrary", "arbitrary")),
        name="routed_experts",
    )(blk_e, nvalid, xs, wg, wu, wd)


def _shared_kernel(x_ref, wg_ref, wu_ref, wd_ref, o_ref):
    _ffn_step(x_ref[...], wg_ref[...], wu_ref[...], wd_ref[...], o_ref, pl.program_id(1))


def _shared_ffn(xb, wg, wu, wd, tm=512, tj=256):
    T, D = xb.shape
    DS = wg.shape[1]
    tm, tj = _tile(T, tm), _tile(DS, tj)
    return pl.pallas_call(
        _shared_kernel,
        grid=(T // tm, DS // tj),
        in_specs=[pl.BlockSpec((tm, D), lambda i, j: (i, 0)),
                  pl.BlockSpec((D, tj), lambda i, j: (0, j)),
                  pl.BlockSpec((D, tj), lambda i, j: (0, j)),
                  pl.BlockSpec((tj, D), lambda i, j: (j, 0))],
        out_specs=pl.BlockSpec((tm, D), lambda i, j: (i, 0)),
        out_shape=jax.ShapeDtypeStruct((T, D), F32),
        compiler_params=_params(("parallel", "arbitrary")),
        name="shared_expert",
    )(xb, wg, wu, wd)


def _combine_kernel(slot_ref, slot_next_ref, gate_ref, x_ref, sh_ref, lw_ref, lb_ref, ys_ref, o_ref, buf_ref, sem):
    i = pl.program_id(0)
    nsteps = pl.num_programs(0)
    tm = x_ref.shape[0]
    cur = i % 2

    def row_copy(srefs, a, buf):
        t = a // TOP_K
        kk = a % TOP_K
        return pltpu.make_async_copy(ys_ref.at[pl.ds(srefs[a], 1)], buf_ref.at[buf, kk, pl.ds(t, 1)], sem.at[buf])

    def start_all(srefs, buf):
        def body(a, carry):
            row_copy(srefs, a, buf).start()
            return carry
        lax.fori_loop(0, tm * TOP_K, body, 0)

    @pl.when(i == 0)
    def _():
        start_all(slot_ref, 0)

    @pl.when(i + 1 < nsteps)
    def _():
        start_all(slot_next_ref, 1 - cur)

    def wait_body(a, carry):
        row_copy(slot_ref, a, cur).wait()
        return carry

    lax.fori_loop(0, tm * TOP_K, wait_body, 0)

    gates = gate_ref[...]
    routed = gates[:, 0:1] * buf_ref[cur, 0]
    for kk in range(1, TOP_K):
        routed = routed + gates[:, kk:kk + 1] * buf_ref[cur, kk]
    z = ALPHA * x_ref[...] + (routed + sh_ref[...])
    o_ref[...] = _layernorm_rows(z, lw_ref[...], lb_ref[...])


def _combine(ys, slot_flat, gate_tk, x1, shared, ln_w, ln_b, tm=64):
    T, D = x1.shape
    tm = _tile(T, tm)
    nsteps = T // tm
    return pl.pallas_call(
        _combine_kernel,
        grid=(nsteps,),
        in_specs=[pl.BlockSpec((tm * TOP_K,), lambda i: (i,), memory_space=pltpu.SMEM),
                  pl.BlockSpec((tm * TOP_K,), lambda i: (jnp.minimum(i + 1, nsteps - 1),), memory_space=pltpu.SMEM),
                  pl.BlockSpec((tm, TOP_K), lambda i: (i, 0)),
                  pl.BlockSpec((tm, D), lambda i: (i, 0)),
                  pl.BlockSpec((tm, D), lambda i: (i, 0)),
                  pl.BlockSpec((1, D), lambda i: (0, 0)),
                  pl.BlockSpec((1, D), lambda i: (0, 0)),
                  pl.BlockSpec(memory_space=pl.ANY)],
        out_specs=pl.BlockSpec((tm, D), lambda i: (i, 0)),
        out_shape=jax.ShapeDtypeStruct((T, D), F32),
        scratch_shapes=[pltpu.VMEM((2, TOP_K, tm, D), F32), pltpu.SemaphoreType.DMA((2,))],
        compiler_params=_params(("arbitrary",)),
        name="moe_combine_layernorm",
    )(slot_flat, slot_flat, gate_tk, x1, shared, ln_w.astype(F32).reshape(1, -1), ln_b.astype(F32).reshape(1, -1),
      ys)


def _seq_tables(groups):
    first, last, pos = [], [], []
    for nseq, slen in groups:
        nb = slen // BLOCK
        for _ in range(nseq):
            for b in range(nb):
                first.append(int(b == 0))
                last.append(int(b == nb - 1))
                pos.append(b)
    return (jnp.asarray(np.array(first, np.int32)), jnp.asarray(np.array(last, np.int32)),
            jnp.asarray(np.array(pos, np.int32)))


def _expert_block_rows(d_model):
    return 512 if d_model >= 2048 else 128


def kernel(x_prompt, x_sample, rel_table, w_in, attn_sink, ret_decay, ret_gn_w, w_att_out, w_ret_out, w_o, ln1_w,
           ln1_b, w_router, router_bias, w_exp_gate, w_exp_up, w_exp_down, w_sh_gate, w_sh_up, w_sh_down, ln2_w,
           ln2_b):
    B1, S1, D = x_prompt.shape
    B2, S2, _ = x_sample.shape
    assert S1 % BLOCK == 0 and S2 % BLOCK == 0 and BLOCK == RET_CHUNK
    x = jnp.concatenate([x_prompt.reshape(B1 * S1, D), x_sample.reshape(B2 * S2, D)], axis=0)
    T = x.shape[0]
    first, last, posblk = _seq_tables(((B1, S1), (B2, S2)))

    att_q, att_kv = ATT_HEADS * HEAD_DIM, ATT_KV_HEADS * HEAD_DIM
    ret_qk, ret_v = RET_HEADS * RET_DK, RET_HEADS * RET_DV
    cols = np.concatenate([[0], np.cumsum([att_q, att_kv, att_kv, ret_qk, ret_qk, ret_v, ret_v, D, D])])
    c_aq, c_ak, c_av, c_rq, c_rk, c_rv, c_rg, c_ga, c_gr = (int(c) for c in cols[:-1])
    assert int(cols[-1]) == w_in.shape[2]

    for l in range(DEPTH):
        proj = _matmul(x.astype(BF16), w_in[l].astype(BF16), F32)
        att = _attention(proj, first, last, attn_sink[l], rel_table, c_aq, c_ak, c_av)
        ret = _retention(proj, first, last, posblk, ret_decay[l], ret_gn_w[l], max(S1, S2), c_rq, c_rk, c_rv, c_rg)
        merged = _merge(att, ret, proj, w_att_out[l].astype(BF16), w_ret_out[l].astype(BF16), c_ga, c_gr)
        x1 = _wo_layernorm(merged, w_o[l].astype(BF16), x, ln1_w[l], ln1_b[l])

        idx, rank, gate, counts, x1b = _router(x1, w_router[l], router_bias[l])
        te = _expert_block_rows(D)
        cnt = counts[:, 0].astype(I32)
        padded = (cnt + te - 1) // te * te
        pad_end = jnp.cumsum(padded)
        pad_start = pad_end - padded
        nblk = (T * TOP_K) // te + N_EXPERTS
        blk_e = jnp.minimum(jnp.searchsorted(pad_end, jnp.arange(nblk, dtype=I32) * te, side='right'),
                            N_EXPERTS - 1).astype(I32)
        nvalid = (pad_end[-1:] // te).astype(I32)
        slot = _slots(pad_start.astype(I32), idx, rank)
        slot_flat = slot.T.reshape(-1)

        xs = _dispatch(x1, slot_flat, (pad_start + cnt).astype(I32), (padded - cnt).astype(I32), nblk * te, te)
        ys = _routed_ffn(xs, blk_e, nvalid, w_exp_gate[l].astype(BF16), w_exp_up[l].astype(BF16),
                         w_exp_down[l].astype(BF16), te)
        shared = _shared_ffn(x1b, w_sh_gate[l].astype(BF16), w_sh_up[l].astype(BF16), w_sh_down[l].astype(BF16))
        x = _combine(ys, slot_flat, gate.T, x1, shared, ln2_w[l], ln2_b[l])

    n1 = B1 * S1
    return (x[:n1].reshape(B1, S1, D), x[n1:].reshape(B2, S2, D))
```

```python
import functools
import math

import numpy as np
import jax
import jax.numpy as jnp
from jax import lax
from jax.experimental import pallas as pl
from jax.experimental.pallas import tpu as pltpu

F32 = jnp.float32
BF16 = jnp.bfloat16
I32 = jnp.int32

HEAD_DIM = 128
ATT_HEADS = 16
ATT_KV_HEADS = 4
WINDOW = 128
BLOCK = 128
REL_BUCKETS = 32
REL_MAX_DIST = 128
RET_HEADS = 8
RET_DK = 128
RET_DV = 256
RET_CHUNK = 128
ROPE_BASE = 10000.0
N_EXPERTS = 128
TOP_K = 8
N_GROUPS = 8
TOPK_GROUPS = 4
ROUTED_SCALE = 2.5
LN_EPS = 1e-5
GN_EPS = 1e-5
DEPTH = 1
ALPHA = (2.0 * DEPTH) ** 0.25

V7X_VMEM_BUDGET_BYTES = 56 * 1024 * 1024
LANES = 128
SUBLANES = 8

NT_DIMS = (((1,), (1,)), ((), ()))
TN_DIMS = (((0,), (0,)), ((), ()))


def _tile(dim, pref):
    t = min(dim, pref)
    assert dim % t == 0, (dim, pref)
    return t


def _params(sem, vmem=V7X_VMEM_BUDGET_BYTES):
    return pltpu.CompilerParams(dimension_semantics=sem, vmem_limit_bytes=vmem)


def _silu(v):
    return v * jax.nn.sigmoid(v)


def _slab_rows(d_model):
    rows = d_model // (2 * LANES)
    assert rows % SUBLANES == 0
    return rows


def _slab_pitch(rows):
    return rows if (rows // SUBLANES) % 2 == 1 else rows + SUBLANES


def _pack_pair(lo, hi):
    lo_bits = lax.bitcast_convert_type(lo.astype(BF16).astype(F32), I32)
    hi_bits = lax.bitcast_convert_type(hi.astype(BF16).astype(F32), I32)
    return hi_bits | lax.shift_right_logical(lo_bits, 16)


def _unpack_pair(w):
    lo = lax.bitcast_convert_type(lax.shift_left(w, 16), F32)
    hi = lax.bitcast_convert_type(w & jnp.int32(-65536), F32)
    return lo, hi


def _store_slabs(slab_ref, col, n_tok, rows, pitch):
    for r in range(rows):
        slab_ref[pl.ds(r, n_tok, stride=pitch), :] = _pack_pair(col(r), col(rows + r))
    for r in range(rows, pitch):
        slab_ref[pl.ds(r, n_tok, stride=pitch), :] = jnp.zeros((n_tok, LANES), I32)


def _matmul_kernel(x_ref, w_ref, o_ref):
    o_ref[...] = jnp.dot(x_ref[...], w_ref[...], preferred_element_type=F32).astype(o_ref.dtype)


def _matmul(x, w, out_dtype, tm=1024, tn=1024):
    M, K = x.shape
    N = w.shape[1]
    tm, tn = _tile(M, tm), _tile(N, tn)
    return pl.pallas_call(
        _matmul_kernel,
        grid=(M // tm, N // tn),
        in_specs=[pl.BlockSpec((tm, K), lambda i, j: (i, 0)),
                  pl.BlockSpec((K, tn), lambda i, j: (0, j))],
        out_specs=pl.BlockSpec((tm, tn), lambda i, j: (i, j)),
        out_shape=jax.ShapeDtypeStruct((M, N), out_dtype),
        compiler_params=_params(("arbitrary", "arbitrary")),
        name="proj_matmul",
    )(x, w)


def _attn_kernel(first_ref, last_ref, sink_ref, q_ref, kp_ref, kc_ref, kn_ref, vp_ref, vc_ref, vn_ref,
                 bias_ref, o_ref, *, kv_heads, group, scale):
    n = pl.program_id(0)
    neg_inf = jnp.float32(-jnp.inf)
    pen_p = jnp.where(first_ref[n] == 1, neg_inf, jnp.float32(0.0))
    pen_n = jnp.where(last_ref[n] == 1, neg_inf, jnp.float32(0.0))
    lane = lax.broadcasted_iota(I32, (1, 3 * BLOCK), 1)
    pen = jnp.where(lane < BLOCK, pen_p, jnp.where(lane >= 2 * BLOCK, pen_n, jnp.float32(0.0)))
    for h in range(kv_heads):
        cols = slice(h * HEAD_DIM, (h + 1) * HEAD_DIM)
        kcat = jnp.concatenate([kp_ref[:, cols], kc_ref[:, cols], kn_ref[:, cols]], axis=0).astype(BF16)
        vcat = jnp.concatenate([vp_ref[:, cols], vc_ref[:, cols], vn_ref[:, cols]], axis=0).astype(BF16)
        for g in range(group):
            hd = h * group + g
            qcols = slice(hd * HEAD_DIM, (hd + 1) * HEAD_DIM)
            q = q_ref[:, qcols].astype(BF16)
            s = lax.dot_general(q, kcat, NT_DIMS, preferred_element_type=F32) * scale + bias_ref[hd] + pen
            sink = sink_ref[hd]
            m = jnp.maximum(jnp.max(s, axis=1, keepdims=True), sink)
            e = jnp.exp(s - m)
            denom = jnp.sum(e, axis=1, keepdims=True) + jnp.exp(sink - m)
            p = (e * (1.0 / denom)).astype(BF16)
            o_ref[:, qcols] = jnp.dot(p, vcat, preferred_element_type=F32).astype(o_ref.dtype)


def _t5_bucket(rel):
    nb = REL_BUCKETS // 2
    max_exact = nb // 2
    ret = (rel > 0).astype(I32) * nb
    n = jnp.abs(rel)
    nf = jnp.maximum(n, max_exact).astype(F32)
    large = max_exact + (jnp.log(nf / max_exact) / math.log(REL_MAX_DIST / max_exact) * (nb - max_exact)).astype(I32)
    large = jnp.minimum(large, nb - 1)
    return ret + jnp.where(n < max_exact, n, large)


def _attention(proj, first, last, sink, rel_table, col_q, col_k, col_v):
    T = proj.shape[0]
    nb = T // BLOCK
    group = ATT_HEADS // ATT_KV_HEADS
    qw, kw = ATT_HEADS * HEAD_DIM, ATT_KV_HEADS * HEAD_DIM
    assert col_q % qw == 0 and col_k % kw == 0 and col_v % kw == 0
    qo, ko, vo = col_q // qw, col_k // kw, col_v // kw
    rel = jnp.arange(3 * BLOCK)[None, :] - BLOCK - jnp.arange(BLOCK)[:, None]
    bias = rel_table.astype(F32)[_t5_bucket(rel)]
    bias = jnp.where((jnp.abs(rel) <= WINDOW)[:, :, None], bias, -jnp.inf)
    bias = jnp.transpose(bias, (2, 0, 1))

    def kv_spec(off, shift):
        def imap(n, first, last):
            return (jnp.clip(n + shift, 0, nb - 1), off)
        return pl.BlockSpec((BLOCK, kw), imap)

    grid_spec = pltpu.PrefetchScalarGridSpec(
        num_scalar_prefetch=2,
        grid=(nb,),
        in_specs=[pl.BlockSpec(memory_space=pltpu.SMEM),
                  pl.BlockSpec((BLOCK, qw), lambda n, first, last: (n, qo)),
                  kv_spec(ko, -1), kv_spec(ko, 0), kv_spec(ko, 1),
                  kv_spec(vo, -1), kv_spec(vo, 0), kv_spec(vo, 1),
                  pl.BlockSpec((ATT_HEADS, BLOCK, 3 * BLOCK), lambda n, first, last: (0, 0, 0))],
        out_specs=pl.BlockSpec((BLOCK, qw), lambda n, first, last: (n, 0)),
    )
    return pl.pallas_call(
        functools.partial(_attn_kernel, kv_heads=ATT_KV_HEADS, group=group, scale=HEAD_DIM ** -0.5),
        grid_spec=grid_spec,
        out_shape=jax.ShapeDtypeStruct((T, qw), BF16),
        compiler_params=_params(("arbitrary",)),
        name="window_attention",
    )(first, last, sink.astype(F32), proj, proj, proj, proj, proj, proj, proj, bias)


def _rope(x, cos2, sin2):
    return x * cos2 + pltpu.roll(x, RET_DK // 2, axis=1) * sin2


def _ret_bwd_kernel(last_ref, posblk_ref, cdec_ref, q_ref, k_ref, v_ref, cos_ref, sin_ref, qb_ref, kb_ref, o_ref,
                    state_ref, *, chunks):
    h = pl.program_id(0)
    n = pl.num_programs(1) - 1 - pl.program_id(1)
    C = RET_CHUNK

    @pl.when(last_ref[n] == 1)
    def _():
        state_ref[...] = jnp.zeros_like(state_ref)

    state = state_ref[...]
    cdec = cdec_ref[1, h]
    for c in range(chunks - 1, -1, -1):
        sl = slice(c * C, (c + 1) * C)
        cos2, sin2 = cos_ref[sl, :], sin_ref[sl, :]
        q = _rope(q_ref[sl, :], cos2, sin2)
        k = _rope(k_ref[sl, :], cos2, sin2) * (RET_DK ** -0.5)
        vb = v_ref[sl, :].astype(BF16)
        o_ref[sl, :] = jnp.dot((q * qb_ref[0]).astype(BF16), state.astype(BF16), preferred_element_type=F32)
        state = state * cdec + lax.dot_general((k * kb_ref[0]).astype(BF16), vb, TN_DIMS,
                                               preferred_element_type=F32)
    state_ref[...] = state


def _ret_fwd_kernel(first_ref, posblk_ref, cdec_ref, q_ref, k_ref, v_ref, g_ref, cos_ref, sin_ref, dmat_ref, qf_ref,
                    kf_ref, ob_ref, gnw_ref, o_ref, state_ref, *, chunks):
    h = pl.program_id(0)
    n = pl.program_id(1)
    C = RET_CHUNK

    @pl.when(first_ref[n] == 1)
    def _():
        state_ref[...] = jnp.zeros_like(state_ref)

    state = state_ref[...]
    cdec = cdec_ref[0, h]
    for c in range(chunks):
        sl = slice(c * C, (c + 1) * C)
        cos2, sin2 = cos_ref[sl, :], sin_ref[sl, :]
        q = _rope(q_ref[sl, :], cos2, sin2)
        k = _rope(k_ref[sl, :], cos2, sin2) * (RET_DK ** -0.5)
        vb = v_ref[sl, :].astype(BF16)
        intra = lax.dot_general(q.astype(BF16), k.astype(BF16), NT_DIMS, preferred_element_type=F32) * dmat_ref[0]
        out = (jnp.dot(intra.astype(BF16), vb, preferred_element_type=F32)
               + jnp.dot((q * qf_ref[0]).astype(BF16), state.astype(BF16), preferred_element_type=F32)
               + ob_ref[sl, :])
        state = state * cdec + lax.dot_general((k * kf_ref[0]).astype(BF16), vb, TN_DIMS,
                                               preferred_element_type=F32)
        mu = jnp.mean(out, axis=1, keepdims=True)
        cen = out - mu
        var = jnp.mean(cen * cen, axis=1, keepdims=True)
        y = cen * lax.rsqrt(var + GN_EPS) * gnw_ref[...]
        o_ref[sl, :] = (_silu(g_ref[sl, :]) * y).astype(o_ref.dtype)
    state_ref[...] = state


def _retention(proj, first, last, posblk, chunks, ret_decay, ret_gn_w, s_max, col_q, col_k, col_v, col_g):
    T = proj.shape[0]
    C = RET_CHUNK
    R = chunks * C
    nc = T // R
    H = RET_HEADS
    qo, ko, vo, go = col_q // RET_DK, col_k // RET_DK, col_v // RET_DV, col_g // RET_DV
    assert col_q % RET_DK == 0 and col_k % RET_DK == 0 and col_v % RET_DV == 0 and col_g % RET_DV == 0

    half = RET_DK // 2
    inv = ROPE_BASE ** (-jnp.arange(half, dtype=F32) * 2.0 / RET_DK)
    ang = jnp.arange(s_max, dtype=F32)[:, None] * inv[None, :]
    cos2 = jnp.concatenate([jnp.cos(ang), jnp.cos(ang)], axis=1)
    sin2 = jnp.concatenate([-jnp.sin(ang), jnp.sin(ang)], axis=1)

    lg = -jnp.exp(ret_decay.astype(F32))
    pos = jnp.arange(C, dtype=F32)
    diff = pos[:, None] - pos[None, :]
    dec_f = jnp.where((diff >= 0)[None], jnp.exp(jnp.maximum(diff, 0.0)[None] * lg[0][:, None, None]), 0.0)
    dec_b = jnp.where((diff < 0)[None], jnp.exp(jnp.maximum(-diff, 0.0)[None] * lg[1][:, None, None]), 0.0)
    dmat = dec_f + dec_b

    def rows(tab):
        return jnp.broadcast_to(tab[:, :, None], (H, C, RET_DK))

    qf = rows(jnp.exp((pos[None, :] + 1.0) * lg[0][:, None]))
    kf = rows(jnp.exp((C - 1.0 - pos)[None, :] * lg[0][:, None]))
    qb = rows(jnp.exp((C - pos)[None, :] * lg[1][:, None]))
    kb = rows(jnp.exp(pos[None, :] * lg[1][:, None]))
    cdec = jnp.exp(C * lg)

    smem = pl.BlockSpec(memory_space=pltpu.SMEM)
    rev = lambda c: nc - 1 - c
    bwd_specs = [
        smem,
        pl.BlockSpec((R, RET_DK), lambda h, c, last, posb: (rev(c), qo + h)),
        pl.BlockSpec((R, RET_DK), lambda h, c, last, posb: (rev(c), ko + h)),
        pl.BlockSpec((R, RET_DV), lambda h, c, last, posb: (rev(c), vo + h)),
        pl.BlockSpec((R, RET_DK), lambda h, c, last, posb: (posb[rev(c)], 0)),
        pl.BlockSpec((R, RET_DK), lambda h, c, last, posb: (posb[rev(c)], 0)),
        pl.BlockSpec((1, C, RET_DK), lambda h, c, last, posb: (h, 0, 0)),
        pl.BlockSpec((1, C, RET_DK), lambda h, c, last, posb: (h, 0, 0)),
    ]
    out_b = pl.pallas_call(
        functools.partial(_ret_bwd_kernel, chunks=chunks),
        grid_spec=pltpu.PrefetchScalarGridSpec(
            num_scalar_prefetch=2, grid=(H, nc), in_specs=bwd_specs,
            out_specs=pl.BlockSpec((R, RET_DV), lambda h, c, last, posb: (rev(c), h)),
            scratch_shapes=[pltpu.VMEM((RET_DK, RET_DV), F32)]),
        out_shape=jax.ShapeDtypeStruct((T, H * RET_DV), F32),
        compiler_params=_params(("arbitrary", "arbitrary")),
        name="retention_backward",
    )(last, posblk, cdec, proj, proj, proj, cos2, sin2, qb, kb)

    fwd_specs = [
        smem,
        pl.BlockSpec((R, RET_DK), lambda h, n, first, posb: (n, qo + h)),
        pl.BlockSpec((R, RET_DK), lambda h, n, first, posb: (n, ko + h)),
        pl.BlockSpec((R, RET_DV), lambda h, n, first, posb: (n, vo + h)),
        pl.BlockSpec((R, RET_DV), lambda h, n, first, posb: (n, go + h)),
        pl.BlockSpec((R, RET_DK), lambda h, n, first, posb: (posb[n], 0)),
        pl.BlockSpec((R, RET_DK), lambda h, n, first, posb: (posb[n], 0)),
        pl.BlockSpec((1, C, C), lambda h, n, first, posb: (h, 0, 0)),
        pl.BlockSpec((1, C, RET_DK), lambda h, n, first, posb: (h, 0, 0)),
        pl.BlockSpec((1, C, RET_DK), lambda h, n, first, posb: (h, 0, 0)),
        pl.BlockSpec((R, RET_DV), lambda h, n, first, posb: (n, h)),
        pl.BlockSpec((1, RET_DV), lambda h, n, first, posb: (0, h)),
    ]
    return pl.pallas_call(
        functools.partial(_ret_fwd_kernel, chunks=chunks),
        grid_spec=pltpu.PrefetchScalarGridSpec(
            num_scalar_prefetch=2, grid=(H, nc), in_specs=fwd_specs,
            out_specs=pl.BlockSpec((R, RET_DV), lambda h, n, first, posb: (n, h)),
            scratch_shapes=[pltpu.VMEM((RET_DK, RET_DV), F32)]),
        out_shape=jax.ShapeDtypeStruct((T, H * RET_DV), BF16),
        compiler_params=_params(("arbitrary", "arbitrary")),
        name="retention_forward",
    )(first, posblk, cdec, proj, proj, proj, proj, cos2, sin2, dmat, qf, kf, out_b,
      ret_gn_w.astype(F32).reshape(1, -1))


def _merge_kernel(att_ref, wa_ref, ret_ref, wr_ref, ga_ref, gr_ref, o_ref):
    a = jnp.dot(att_ref[...], wa_ref[...], preferred_element_type=F32)
    r = jnp.dot(ret_ref[...], wr_ref[...], preferred_element_type=F32)
    o_ref[...] = (jax.nn.sigmoid(ga_ref[...]) * a + jax.nn.sigmoid(gr_ref[...]) * r).astype(o_ref.dtype)


def _merge(att, ret, proj, w_att_out, w_ret_out, col_ga, col_gr, tm=1024, tn=512):
    T = att.shape[0]
    D = w_att_out.shape[1]
    tm, tn = _tile(T, tm), _tile(D, tn)
    assert col_ga % tn == 0 and col_gr % tn == 0
    ao, ro = col_ga // tn, col_gr // tn
    return pl.pallas_call(
        _merge_kernel,
        grid=(T // tm, D // tn),
        in_specs=[pl.BlockSpec((tm, att.shape[1]), lambda i, j: (i, 0)),
                  pl.BlockSpec((w_att_out.shape[0], tn), lambda i, j: (0, j)),
                  pl.BlockSpec((tm, ret.shape[1]), lambda i, j: (i, 0)),
                  pl.BlockSpec((w_ret_out.shape[0], tn), lambda i, j: (0, j)),
                  pl.BlockSpec((tm, tn), lambda i, j: (i, ao + j)),
                  pl.BlockSpec((tm, tn), lambda i, j: (i, ro + j))],
        out_specs=pl.BlockSpec((tm, tn), lambda i, j: (i, j)),
        out_shape=jax.ShapeDtypeStruct((T, D), BF16),
        compiler_params=_params(("arbitrary", "arbitrary")),
        name="branch_merge",
    )(att, w_att_out, ret, w_ret_out, proj, proj)


def _layernorm_chunks(zs, d_model, w_of, b_of):
    total = zs[0].sum(axis=1, keepdims=True)
    for z in zs[1:]:
        total = total + z.sum(axis=1, keepdims=True)
    mu = total / d_model
    cens = [z - mu for z in zs]
    sq = (cens[0] * cens[0]).sum(axis=1, keepdims=True)
    for cen in cens[1:]:
        sq = sq + (cen * cen).sum(axis=1, keepdims=True)
    rstd = lax.rsqrt(sq / d_model + LN_EPS)
    return [cen * rstd * w_of(c) + b_of(c) for c, cen in enumerate(cens)]


def _wo_ln_kernel(m_ref, w_ref, x_ref, lw_ref, lb_ref, o_ref, slab_ref, z_ref, *, rows, pitch):
    n = pl.program_id(1)
    nn, tm, tn = z_ref.shape
    z_ref[n] = ALPHA * x_ref[...] + jnp.dot(m_ref[...], w_ref[...], preferred_element_type=F32)

    @pl.when(n == nn - 1)
    def _():
        ys = _layernorm_chunks([z_ref[c] for c in range(nn)], nn * tn,
                               lambda c: lw_ref[:, c * tn:(c + 1) * tn], lambda c: lb_ref[:, c * tn:(c + 1) * tn])
        for c in range(nn):
            o_ref[:, c * tn:(c + 1) * tn] = ys[c]
        per = tn // LANES

        def col(j):
            return ys[j // per][:, (j % per) * LANES:(j % per + 1) * LANES]

        _store_slabs(slab_ref, col, tm, rows, pitch)


def _wo_layernorm(merged, w_o, x, ln_w, ln_b, tm=256, tn=512):
    T, D = x.shape
    K = merged.shape[1]
    tm, tn = _tile(T, tm), _tile(D, tn)
    rows = _slab_rows(D)
    pitch = _slab_pitch(rows)
    return pl.pallas_call(
        functools.partial(_wo_ln_kernel, rows=rows, pitch=pitch),
        grid=(T // tm, D // tn),
        in_specs=[pl.BlockSpec((tm, K), lambda i, n: (i, 0)),
                  pl.BlockSpec((K, tn), lambda i, n: (0, n)),
                  pl.BlockSpec((tm, tn), lambda i, n: (i, n)),
                  pl.BlockSpec((1, D), lambda i, n: (0, 0)),
                  pl.BlockSpec((1, D), lambda i, n: (0, 0))],
        out_specs=[pl.BlockSpec((tm, D), lambda i, n: (i, 0)),
                   pl.BlockSpec((tm * pitch, LANES), lambda i, n: (i, 0))],
        out_shape=[jax.ShapeDtypeStruct((T, D), F32),
                   jax.ShapeDtypeStruct((T * pitch, LANES), I32)],
        scratch_shapes=[pltpu.VMEM((D // tn, tm, tn), F32)],
        compiler_params=_params(("arbitrary", "arbitrary")),
        name="out_proj_layernorm",
    )(merged, w_o, x, ln_w.astype(F32).reshape(1, -1), ln_b.astype(F32).reshape(1, -1))


def _router_kernel(x_ref, wh_ref, wl_ref, bias_ref, idx_ref, rank_ref, gate_ref, cnt_ref, xb_ref, carry_ref):
    i = pl.program_id(0)
    tm = x_ref.shape[0]
    E = N_EXPERTS
    gsz = E // N_GROUPS
    neg_inf = jnp.float32(-jnp.inf)

    @pl.when(i == 0)
    def _():
        carry_ref[...] = jnp.zeros_like(carry_ref)

    x = x_ref[...]
    xh = x.astype(BF16)
    xb_ref[...] = xh
    xl = (x - xh.astype(F32)).astype(BF16)
    logits = (lax.dot_general(wh_ref[...], xh, NT_DIMS, preferred_element_type=F32)
              + lax.dot_general(wh_ref[...], xl, NT_DIMS, preferred_element_type=F32)
              + lax.dot_general(wl_ref[...], xh, NT_DIMS, preferred_element_type=F32))
    scores = jax.nn.sigmoid(logits)
    biased = scores + bias_ref[...]

    row = lax.broadcasted_iota(I32, (gsz, tm), 0)
    gscore = []
    for g in range(N_GROUPS):
        blk = biased[g * gsz:(g + 1) * gsz, :]
        m1 = jnp.max(blk, axis=0, keepdims=True)
        first = jnp.min(jnp.where(blk == m1, row, gsz), axis=0, keepdims=True)
        m2 = jnp.max(jnp.where(row == first, neg_inf, blk), axis=0, keepdims=True)
        gscore.append(m1 + m2)
    gid = lax.broadcasted_iota(I32, (E, tm), 0) // gsz
    keep = jnp.zeros((E, tm), I32)
    for g in range(N_GROUPS):
        beaten = jnp.zeros((1, tm), I32)
        for o in range(N_GROUPS):
            if o == g:
                continue
            wins = (gscore[o] >= gscore[g]) if o < g else (gscore[o] > gscore[g])
            beaten = beaten + wins.astype(I32)
        keep = jnp.where(gid == g, (beaten < TOPK_GROUPS).astype(I32), keep)
    masked = jnp.where(keep > 0, biased, neg_inf)

    eidx = lax.broadcasted_iota(I32, (E, tm), 0)
    sel = jnp.zeros((E, tm), jnp.bool_)
    picks, weights = [], []
    for _ in range(TOP_K):
        m = jnp.max(masked, axis=0, keepdims=True)
        pick = jnp.min(jnp.where(masked == m, eidx, E), axis=0, keepdims=True)
        onehot = eidx == pick
        weights.append(jnp.sum(jnp.where(onehot, scores, 0.0), axis=0, keepdims=True))
        picks.append(pick)
        masked = jnp.where(onehot, neg_inf, masked)
        sel = sel | onehot
    wsum = weights[0]
    for w in weights[1:]:
        wsum = wsum + w

    self32 = sel.astype(F32)
    tri = (lax.broadcasted_iota(I32, (tm, tm), 0) < lax.broadcasted_iota(I32, (tm, tm), 1)).astype(F32).astype(BF16)
    prefix = jnp.dot(self32.astype(BF16), tri, preferred_element_type=F32) + carry_ref[...]
    for kk in range(TOP_K):
        onehot = eidx == picks[kk]
        idx_ref[kk:kk + 1, :] = picks[kk]
        rank_ref[kk:kk + 1, :] = jnp.sum(jnp.where(onehot, prefix, 0.0), axis=0, keepdims=True).astype(I32)
        gate_ref[kk:kk + 1, :] = weights[kk] / wsum * ROUTED_SCALE
    carry_ref[...] = carry_ref[...] + jnp.sum(self32, axis=1, keepdims=True)
    cnt_ref[...] = carry_ref[...]


def _router(x1, w_router, router_bias, tm=256):
    T, D = x1.shape
    E = N_EXPERTS
    tm = _tile(T, tm)
    wt = w_router.astype(F32).T
    wh = wt.astype(BF16)
    wl = (wt - wh.astype(F32)).astype(BF16)
    return pl.pallas_call(
        _router_kernel,
        grid=(T // tm,),
        in_specs=[pl.BlockSpec((tm, D), lambda i: (i, 0)),
                  pl.BlockSpec((E, D), lambda i: (0, 0)),
                  pl.BlockSpec((E, D), lambda i: (0, 0)),
                  pl.BlockSpec((E, 1), lambda i: (0, 0))],
        out_specs=[pl.BlockSpec((TOP_K, tm), lambda i: (0, i)),
                   pl.BlockSpec((TOP_K, tm), lambda i: (0, i)),
                   pl.BlockSpec((TOP_K, tm), lambda i: (0, i)),
                   pl.BlockSpec((E, 1), lambda i: (0, 0)),
                   pl.BlockSpec((tm, D), lambda i: (i, 0))],
        out_shape=[jax.ShapeDtypeStruct((TOP_K, T), I32),
                   jax.ShapeDtypeStruct((TOP_K, T), I32),
                   jax.ShapeDtypeStruct((TOP_K, T), F32),
                   jax.ShapeDtypeStruct((E, 1), F32),
                   jax.ShapeDtypeStruct((T, D), BF16)],
        scratch_shapes=[pltpu.VMEM((E, 1), F32)],
        compiler_params=_params(("arbitrary",)),
        name="router",
    )(x1, wh, wl, router_bias.astype(F32).reshape(E, 1))


def _slot_kernel(start_ref, idx_ref, rank_ref, slot_ref):
    idx = idx_ref[...]
    acc = rank_ref[...]
    for e in range(N_EXPERTS):
        acc = acc + jnp.where(idx == e, start_ref[e], 0)
    slot_ref[...] = acc


def _slots(pad_start, idx, rank, tm=2048):
    T = idx.shape[1]
    tm = _tile(T, tm)
    return pl.pallas_call(
        _slot_kernel,
        grid=(T // tm,),
        in_specs=[pl.BlockSpec(memory_space=pltpu.SMEM),
                  pl.BlockSpec((TOP_K, tm), lambda i: (0, i)),
                  pl.BlockSpec((TOP_K, tm), lambda i: (0, i))],
        out_specs=pl.BlockSpec((TOP_K, tm), lambda i: (0, i)),
        out_shape=jax.ShapeDtypeStruct((TOP_K, T), I32),
        compiler_params=_params(("arbitrary",)),
        name="slot_index",
    )(pad_start, idx, rank)


def _zero_fill_plan(npad, start, te, fn):
    pos = start
    bit = te // 2
    while bit >= 1:
        take = (npad & bit) != 0

        @pl.when(take)
        def _(pos=pos, bit=bit):
            fn(pos, bit)

        pos = pos + jnp.where(take, bit, 0)
        bit //= 2


def _dispatch_kernel(fill_start_ref, fill_len_ref, slot_ref, x_ref, xs_ref, zero_ref, sem, zsem, *, te, rows, pitch):
    i = pl.program_id(0)
    tm = x_ref.shape[0] // pitch

    def start_token(t, carry):
        src = x_ref.at[pl.ds(pl.multiple_of(t * pitch, SUBLANES), rows)]
        for k in range(TOP_K):
            dst = xs_ref.at[pl.ds(pl.multiple_of(slot_ref[t * TOP_K + k] * pitch, SUBLANES), rows)]
            pltpu.make_async_copy(src, dst, sem).start()
        return carry

    lax.fori_loop(0, tm, start_token, 0)

    @pl.when(i == 0)
    def _():
        zero_ref[...] = jnp.zeros_like(zero_ref)

        def zero_copy(pos, n):
            return pltpu.make_async_copy(zero_ref.at[pl.ds(0, n * pitch)],
                                         xs_ref.at[pl.ds(pl.multiple_of(pos * pitch, SUBLANES), n * pitch)], zsem)

        def start_fill(e, carry):
            _zero_fill_plan(fill_len_ref[e], fill_start_ref[e], te, lambda pos, n: zero_copy(pos, n).start())
            return carry

        def wait_fill(e, carry):
            _zero_fill_plan(fill_len_ref[e], fill_start_ref[e], te, lambda pos, n: zero_copy(pos, n).wait())
            return carry

        lax.fori_loop(0, N_EXPERTS, start_fill, 0)
        lax.fori_loop(0, N_EXPERTS, wait_fill, 0)

    def wait_token(t, carry):
        for k in range(TOP_K):
            pltpu.make_async_copy(x_ref.at[pl.ds(0, rows)], xs_ref.at[pl.ds(0, rows)], sem).wait()
        return carry

    lax.fori_loop(0, tm, wait_token, 0)


def _dispatch(x_slab, slot_flat, fill_start, fill_len, n_slots, te, rows, pitch, tm=256):
    T = x_slab.shape[0] // pitch
    tm = _tile(T, tm)
    return pl.pallas_call(
        functools.partial(_dispatch_kernel, te=te, rows=rows, pitch=pitch),
        grid_spec=pltpu.PrefetchScalarGridSpec(
            num_scalar_prefetch=2,
            grid=(T // tm,),
            in_specs=[pl.BlockSpec((tm * TOP_K,), lambda i, fs, fl: (i,), memory_space=pltpu.SMEM),
                      pl.BlockSpec((tm * pitch, LANES), lambda i, fs, fl: (i, 0))],
            out_specs=pl.BlockSpec(memory_space=pl.ANY),
            scratch_shapes=[pltpu.VMEM((te // 2 * pitch, LANES), I32),
                            pltpu.SemaphoreType.DMA(()),
                            pltpu.SemaphoreType.DMA(())]),
        out_shape=jax.ShapeDtypeStruct((n_slots * pitch, LANES), I32),
        compiler_params=_params(("arbitrary",)),
        name="moe_dispatch",
    )(fill_start, fill_len, slot_flat, x_slab)


def _ffn_partial(xb, wg, wu, wd):
    hid = _silu(jnp.dot(xb, wg, preferred_element_type=F32)) * jnp.dot(xb, wu, preferred_element_type=F32)
    return jnp.dot(hid.astype(BF16), wd, preferred_element_type=F32)


def _expert_kernel(blk_e_ref, nvalid_ref, x_ref, wg_ref, wu_ref, wd_ref, o_ref, xb_ref, acc_ref, *, rows, pitch):
    b = pl.program_id(0)
    j = pl.program_id(1)
    te, d_model = xb_ref.shape
    half = d_model // 2

    @pl.when(b < nvalid_ref[0])
    def _():
        @pl.when(j == 0)
        def _():
            for r in range(rows):
                lo, hi = _unpack_pair(x_ref[pl.ds(r, te, stride=pitch), :])
                xb_ref[:, r * LANES:(r + 1) * LANES] = lo.astype(BF16)
                xb_ref[:, half + r * LANES:half + (r + 1) * LANES] = hi.astype(BF16)
            acc_ref[...] = jnp.zeros_like(acc_ref)

        acc_ref[...] += _ffn_partial(xb_ref[...], wg_ref[0], wu_ref[0], wd_ref[0])

        @pl.when(j == pl.num_programs(1) - 1)
        def _():
            _store_slabs(o_ref, lambda c: acc_ref[:, c * LANES:(c + 1) * LANES], te, rows, pitch)


def _routed_ffn(xs, blk_e, nvalid, wg, wu, wd, te, rows, pitch, tj=256):
    P = xs.shape[0] // pitch
    D, DE = wg.shape[1], wg.shape[2]
    tj = _tile(DE, tj)
    nj = DE // tj
    nblk = P // te

    def bclamp(b, nv):
        return jnp.minimum(b, nv[0] - 1)

    def jclamp(b, j, nv):
        return jnp.where(b < nv[0], j, nj - 1)

    return pl.pallas_call(
        functools.partial(_expert_kernel, rows=rows, pitch=pitch),
        grid_spec=pltpu.PrefetchScalarGridSpec(
            num_scalar_prefetch=2,
            grid=(nblk, nj),
            in_specs=[pl.BlockSpec((te * pitch, LANES), lambda b, j, be, nv: (bclamp(b, nv), 0)),
                      pl.BlockSpec((1, D, tj), lambda b, j, be, nv: (be[bclamp(b, nv)], 0, jclamp(b, j, nv))),
                      pl.BlockSpec((1, D, tj), lambda b, j, be, nv: (be[bclamp(b, nv)], 0, jclamp(b, j, nv))),
                      pl.BlockSpec((1, tj, D), lambda b, j, be, nv: (be[bclamp(b, nv)], jclamp(b, j, nv), 0))],
            out_specs=pl.BlockSpec((te * pitch, LANES), lambda b, j, be, nv: (bclamp(b, nv), 0)),
            scratch_shapes=[pltpu.VMEM((te, D), BF16), pltpu.VMEM((te, D), F32)]),
        out_shape=jax.ShapeDtypeStruct((P * pitch, LANES), I32),
        compiler_params=_params(("arbitrary", "arbitrary")),
        name="routed_experts",
    )(blk_e, nvalid, xs, wg, wu, wd)


def _shared_kernel(x_ref, wg_ref, wu_ref, wd_ref, o_ref):
    @pl.when(pl.program_id(1) == 0)
    def _():
        o_ref[...] = jnp.zeros_like(o_ref)

    o_ref[...] += _ffn_partial(x_ref[...], wg_ref[...], wu_ref[...], wd_ref[...])


def _shared_ffn(xb, wg, wu, wd, tm=512, tj=256):
    T, D = xb.shape
    DS = wg.shape[1]
    tm, tj = _tile(T, tm), _tile(DS, tj)
    return pl.pallas_call(
        _shared_kernel,
        grid=(T // tm, DS // tj),
        in_specs=[pl.BlockSpec((tm, D), lambda i, j: (i, 0)),
                  pl.BlockSpec((D, tj), lambda i, j: (0, j)),
                  pl.BlockSpec((D, tj), lambda i, j: (0, j)),
                  pl.BlockSpec((tj, D), lambda i, j: (j, 0))],
        out_specs=pl.BlockSpec((tm, D), lambda i, j: (i, 0)),
        out_shape=jax.ShapeDtypeStruct((T, D), F32),
        compiler_params=_params(("arbitrary", "arbitrary")),
        name="shared_expert",
    )(xb, wg, wu, wd)


def _combine_kernel(slot_ref, slot_next_ref, gate_ref, x_ref, sh_ref, lw_ref, lb_ref, ys_ref, o1_ref, o2_ref,
                    buf_ref, z_ref, sem, *, rows, pitch, steps1):
    i = pl.program_id(0)
    nsteps = pl.num_programs(0)
    tm, d_model = x_ref.shape
    half = d_model // 2
    cur = i % 2

    def start_all(srefs, buf):
        def body(t, carry):
            for kk in range(TOP_K):
                src = ys_ref.at[pl.ds(pl.multiple_of(srefs[t * TOP_K + kk] * pitch, SUBLANES), rows)]
                dst = buf_ref.at[buf, kk, pl.ds(pl.multiple_of(t * pitch, SUBLANES), rows)]
                pltpu.make_async_copy(src, dst, sem.at[buf]).start()
            return carry
        lax.fori_loop(0, tm, body, 0)

    @pl.when(i == 0)
    def _():
        start_all(slot_ref, 0)

    @pl.when(i + 1 < nsteps)
    def _():
        start_all(slot_next_ref, 1 - cur)

    def wait_body(t, carry):
        for kk in range(TOP_K):
            pltpu.make_async_copy(ys_ref.at[pl.ds(0, rows)], buf_ref.at[cur, 0, pl.ds(0, rows)], sem.at[cur]).wait()
        return carry

    lax.fori_loop(0, tm, wait_body, 0)

    gates = gate_ref[...]
    gk = [gates[:, kk:kk + 1] for kk in range(TOP_K)]
    for r in range(rows):
        lo_acc = hi_acc = None
        for kk in range(TOP_K):
            lo, hi = _unpack_pair(buf_ref[cur, kk, pl.ds(r, tm, stride=pitch), :])
            lo_acc = gk[kk] * lo if lo_acc is None else lo_acc + gk[kk] * lo
            hi_acc = gk[kk] * hi if hi_acc is None else hi_acc + gk[kk] * hi
        for c0, acc in ((r * LANES, lo_acc), (half + r * LANES, hi_acc)):
            cols = slice(c0, c0 + LANES)
            z_ref[:, cols] = ALPHA * x_ref[:, cols] + (acc + sh_ref[:, cols])
    z = z_ref[...]
    mu = jnp.mean(z, axis=1, keepdims=True)
    cen = z - mu
    var = jnp.mean(cen * cen, axis=1, keepdims=True)
    y = cen * lax.rsqrt(var + LN_EPS) * lw_ref[...] + lb_ref[...]

    @pl.when(i < steps1)
    def _():
        o1_ref[...] = y

    @pl.when(i >= steps1)
    def _():
        o2_ref[...] = y


def _combine(ys, slot_flat, gate_tk, x1, shared, ln_w, ln_b, n1, rows, pitch, tm=128):
    T, D = x1.shape
    tm = _tile(math.gcd(n1, T - n1), tm)
    nsteps = T // tm
    steps1 = n1 // tm
    return pl.pallas_call(
        functools.partial(_combine_kernel, rows=rows, pitch=pitch, steps1=steps1),
        grid=(nsteps,),
        in_specs=[pl.BlockSpec((tm * TOP_K,), lambda i: (i,), memory_space=pltpu.SMEM),
                  pl.BlockSpec((tm * TOP_K,), lambda i: (jnp.minimum(i + 1, nsteps - 1),), memory_space=pltpu.SMEM),
                  pl.BlockSpec((tm, TOP_K), lambda i: (i, 0)),
                  pl.BlockSpec((tm, D), lambda i: (i, 0)),
                  pl.BlockSpec((tm, D), lambda i: (i, 0)),
                  pl.BlockSpec((1, D), lambda i: (0, 0)),
                  pl.BlockSpec((1, D), lambda i: (0, 0)),
                  pl.BlockSpec(memory_space=pl.ANY)],
        out_specs=[pl.BlockSpec((tm, D), lambda i: (jnp.minimum(i, steps1 - 1), 0)),
                   pl.BlockSpec((tm, D), lambda i: (jnp.maximum(i - steps1, 0), 0))],
        out_shape=[jax.ShapeDtypeStruct((n1, D), F32), jax.ShapeDtypeStruct((T - n1, D), F32)],
        scratch_shapes=[pltpu.VMEM((2, TOP_K, tm * pitch, LANES), I32), pltpu.VMEM((tm, D), F32),
                        pltpu.SemaphoreType.DMA((2,))],
        compiler_params=_params(("arbitrary",)),
        name="moe_combine_layernorm",
    )(slot_flat, slot_flat, gate_tk, x1, shared, ln_w.astype(F32).reshape(1, -1), ln_b.astype(F32).reshape(1, -1),
      ys)


def _seq_tables(groups, blk):
    first, last, pos = [], [], []
    for nseq, slen in groups:
        nb = slen // blk
        for _ in range(nseq):
            for b in range(nb):
                first.append(int(b == 0))
                last.append(int(b == nb - 1))
                pos.append(b)
    return (jnp.asarray(np.array(first, np.int32)), jnp.asarray(np.array(last, np.int32)),
            jnp.asarray(np.array(pos, np.int32)))


def _expert_block_rows(d_model):
    return 512 if d_model >= 4096 else 128


def kernel(x_prompt, x_sample, rel_table, w_in, attn_sink, ret_decay, ret_gn_w, w_att_out, w_ret_out, w_o, ln1_w,
           ln1_b, w_router, router_bias, w_exp_gate, w_exp_up, w_exp_down, w_sh_gate, w_sh_up, w_sh_down, ln2_w,
           ln2_b):
    assert DEPTH == 1
    B1, S1, D = x_prompt.shape
    B2, S2, _ = x_sample.shape
    assert S1 % BLOCK == 0 and S2 % BLOCK == 0 and BLOCK == RET_CHUNK
    x = jnp.concatenate([x_prompt.reshape(B1 * S1, D), x_sample.reshape(B2 * S2, D)], axis=0)
    T = x.shape[0]
    n1 = B1 * S1
    groups = ((B1, S1), (B2, S2))
    first, last, _ = _seq_tables(groups, BLOCK)
    ret_chunks = math.gcd(4, math.gcd(S1 // RET_CHUNK, S2 // RET_CHUNK))
    rfirst, rlast, rpos = _seq_tables(groups, ret_chunks * RET_CHUNK)

    att_q, att_kv = ATT_HEADS * HEAD_DIM, ATT_KV_HEADS * HEAD_DIM
    ret_qk, ret_v = RET_HEADS * RET_DK, RET_HEADS * RET_DV
    cols = np.concatenate([[0], np.cumsum([att_q, att_kv, att_kv, ret_qk, ret_qk, ret_v, ret_v, D, D])])
    c_aq, c_ak, c_av, c_rq, c_rk, c_rv, c_rg, c_ga, c_gr = (int(c) for c in cols[:-1])
    assert int(cols[-1]) == w_in.shape[2]
    rows = _slab_rows(D)
    pitch = _slab_pitch(rows)

    proj = _matmul(x.astype(BF16), w_in[0].astype(BF16), F32)
    att = _attention(proj, first, last, attn_sink[0], rel_table, c_aq, c_ak, c_av)
    ret = _retention(proj, rfirst, rlast, rpos, ret_chunks, ret_decay[0], ret_gn_w[0], max(S1, S2),
                     c_rq, c_rk, c_rv, c_rg)
    merged = _merge(att, ret, proj, w_att_out[0].astype(BF16), w_ret_out[0].astype(BF16), c_ga, c_gr)
    x1, x1_slab = _wo_layernorm(merged, w_o[0].astype(BF16), x, ln1_w[0], ln1_b[0])

    idx, rank, gate, counts, x1b = _router(x1, w_router[0], router_bias[0])
    te = _expert_block_rows(D)
    cnt = counts[:, 0].astype(I32)
    padded = (cnt + te - 1) // te * te
    pad_end = jnp.cumsum(padded)
    pad_start = pad_end - padded
    nblk = (T * TOP_K) // te + N_EXPERTS
    blk_e = jnp.minimum(jnp.searchsorted(pad_end, jnp.arange(nblk, dtype=I32) * te, side='right'),
                        N_EXPERTS - 1).astype(I32)
    nvalid = (pad_end[-1:] // te).astype(I32)
    slot = _slots(pad_start.astype(I32), idx, rank)
    slot_flat = slot.T.reshape(-1)

    xs = _dispatch(x1_slab, slot_flat, (pad_start + cnt).astype(I32), (padded - cnt).astype(I32), nblk * te, te,
                   rows, pitch)
    ys = _routed_ffn(xs, blk_e, nvalid, w_exp_gate[0].astype(BF16), w_exp_up[0].astype(BF16),
                     w_exp_down[0].astype(BF16), te, rows, pitch)
    shared = _shared_ffn(x1b, w_sh_gate[0].astype(BF16), w_sh_up[0].astype(BF16), w_sh_down[0].astype(BF16))
    y1, y2 = _combine(ys, slot_flat, gate.T, x1, shared, ln2_w[0], ln2_b[0], n1, rows, pitch)
    return (y1.reshape(B1, S1, D), y2.reshape(B2, S2, D))
```

```python
import functools
import math

import numpy as np
import jax
import jax.numpy as jnp
from jax import lax
from jax.experimental import pallas as pl
from jax.experimental.pallas import tpu as pltpu

F32 = jnp.float32
BF16 = jnp.bfloat16
I32 = jnp.int32

HEAD_DIM = 128
ATT_HEADS = 16
ATT_KV_HEADS = 4
WINDOW = 128
BLOCK = 128
REL_BUCKETS = 32
REL_MAX_DIST = 128
RET_HEADS = 8
RET_DK = 128
RET_DV = 256
RET_CHUNK = 128
ROPE_BASE = 10000.0
N_EXPERTS = 128
TOP_K = 8
N_GROUPS = 8
TOPK_GROUPS = 4
ROUTED_SCALE = 2.5
LN_EPS = 1e-5
GN_EPS = 1e-5
DEPTH = 1
ALPHA = (2.0 * DEPTH) ** 0.25

V7X_VMEM_BUDGET_BYTES = 56 * 1024 * 1024
LANES = 128
SUBLANES = 8

NT_DIMS = (((1,), (1,)), ((), ()))
TN_DIMS = (((0,), (0,)), ((), ()))


def _tile(dim, pref):
    t = min(dim, pref)
    assert dim % t == 0, (dim, pref)
    return t


def _params(sem, vmem=V7X_VMEM_BUDGET_BYTES):
    return pltpu.CompilerParams(dimension_semantics=sem, vmem_limit_bytes=vmem)


def _silu(v):
    return v * jax.nn.sigmoid(v)


def _slab_rows(d_model):
    rows = d_model // (2 * LANES)
    assert rows % SUBLANES == 0
    return rows


def _slab_pitch(rows):
    return rows if (rows // SUBLANES) % 2 == 1 else rows + SUBLANES


def _pack_pair(lo, hi):
    lo_bits = lax.bitcast_convert_type(lo.astype(BF16).astype(F32), I32)
    hi_bits = lax.bitcast_convert_type(hi.astype(BF16).astype(F32), I32)
    return hi_bits | lax.shift_right_logical(lo_bits, 16)


def _unpack_pair(w):
    lo = lax.bitcast_convert_type(lax.shift_left(w, 16), F32)
    hi = lax.bitcast_convert_type(w & jnp.int32(-65536), F32)
    return lo, hi


def _store_slabs(slab_ref, col, n_tok, rows, pitch):
    for r in range(rows):
        slab_ref[pl.ds(r, n_tok, stride=pitch), :] = _pack_pair(col(r), col(rows + r))
    for r in range(rows, pitch):
        slab_ref[pl.ds(r, n_tok, stride=pitch), :] = jnp.zeros((n_tok, LANES), I32)


def _matmul_kernel(x_ref, w_ref, o_ref):
    o_ref[...] = jnp.dot(x_ref[...], w_ref[...], preferred_element_type=F32).astype(o_ref.dtype)


def _col_tiles(w, tn):
    *lead, K, N = w.shape
    tn = _tile(N, tn)
    wt = w.astype(BF16).reshape(*lead, K, N // tn, tn)
    return jnp.moveaxis(wt, -2, -3)


def _matmul(x, w_tiles, out_dtype, tm=1024):
    M, K = x.shape
    nt, _, tn = w_tiles.shape
    tm = _tile(M, tm)
    return pl.pallas_call(
        _matmul_kernel,
        grid=(M // tm, nt),
        in_specs=[pl.BlockSpec((tm, K), lambda i, j: (i, 0)),
                  pl.BlockSpec((None, K, tn), lambda i, j: (j, 0, 0))],
        out_specs=pl.BlockSpec((tm, tn), lambda i, j: (i, j)),
        out_shape=jax.ShapeDtypeStruct((M, nt * tn), out_dtype),
        compiler_params=_params(("arbitrary", "arbitrary")),
        name="proj_matmul",
    )(x, w_tiles)


def _attn_kernel(first_ref, last_ref, sink_ref, q_ref, kp_ref, kc_ref, kn_ref, vp_ref, vc_ref, vn_ref,
                 bias_ref, o_ref, *, kv_heads, group, scale):
    n = pl.program_id(0)
    neg_inf = jnp.float32(-jnp.inf)
    pen_p = jnp.where(first_ref[n] == 1, neg_inf, jnp.float32(0.0))
    pen_n = jnp.where(last_ref[n] == 1, neg_inf, jnp.float32(0.0))
    lane = lax.broadcasted_iota(I32, (1, 3 * BLOCK), 1)
    pen = jnp.where(lane < BLOCK, pen_p, jnp.where(lane >= 2 * BLOCK, pen_n, jnp.float32(0.0)))
    for h in range(kv_heads):
        cols = slice(h * HEAD_DIM, (h + 1) * HEAD_DIM)
        kcat = jnp.concatenate([kp_ref[:, cols], kc_ref[:, cols], kn_ref[:, cols]], axis=0).astype(BF16)
        vcat = jnp.concatenate([vp_ref[:, cols], vc_ref[:, cols], vn_ref[:, cols]], axis=0).astype(BF16)
        heads = [h * group + g for g in range(group)]
        qcols = [slice(hd * HEAD_DIM, (hd + 1) * HEAD_DIM) for hd in heads]
        sinks = [sink_ref[hd] for hd in heads]
        s = [lax.dot_general(q_ref[:, c].astype(BF16), kcat, NT_DIMS, preferred_element_type=F32) * scale
             + bias_ref[hd] + pen for hd, c in zip(heads, qcols)]
        m = [jnp.maximum(jnp.max(si, axis=1, keepdims=True), sk) for si, sk in zip(s, sinks)]
        e = [jnp.exp(si - mi) for si, mi in zip(s, m)]
        denom = [jnp.sum(ei, axis=1, keepdims=True) + jnp.exp(sk - mi) for ei, sk, mi in zip(e, sinks, m)]
        p = [(ei * (1.0 / di)).astype(BF16) for ei, di in zip(e, denom)]
        for c, pi in zip(qcols, p):
            o_ref[:, c] = jnp.dot(pi, vcat, preferred_element_type=F32).astype(o_ref.dtype)


def _t5_bucket(rel):
    nb = REL_BUCKETS // 2
    max_exact = nb // 2
    ret = (rel > 0).astype(I32) * nb
    n = jnp.abs(rel)
    nf = jnp.maximum(n, max_exact).astype(F32)
    large = max_exact + (jnp.log(nf / max_exact) / math.log(REL_MAX_DIST / max_exact) * (nb - max_exact)).astype(I32)
    large = jnp.minimum(large, nb - 1)
    return ret + jnp.where(n < max_exact, n, large)


def _attention(proj, first, last, sink, rel_table, col_q, col_k, col_v):
    T = proj.shape[0]
    nb = T // BLOCK
    group = ATT_HEADS // ATT_KV_HEADS
    qw, kw = ATT_HEADS * HEAD_DIM, ATT_KV_HEADS * HEAD_DIM
    assert col_q % qw == 0 and col_k % kw == 0 and col_v % kw == 0
    qo, ko, vo = col_q // qw, col_k // kw, col_v // kw
    rel = jnp.arange(3 * BLOCK)[None, :] - BLOCK - jnp.arange(BLOCK)[:, None]
    bias = rel_table.astype(F32)[_t5_bucket(rel)]
    bias = jnp.where((jnp.abs(rel) <= WINDOW)[:, :, None], bias, -jnp.inf)
    bias = jnp.transpose(bias, (2, 0, 1))

    def kv_spec(off, shift):
        def imap(n, first, last):
            return (jnp.clip(n + shift, 0, nb - 1), off)
        return pl.BlockSpec((BLOCK, kw), imap)

    grid_spec = pltpu.PrefetchScalarGridSpec(
        num_scalar_prefetch=2,
        grid=(nb,),
        in_specs=[pl.BlockSpec(memory_space=pltpu.SMEM),
                  pl.BlockSpec((BLOCK, qw), lambda n, first, last: (n, qo)),
                  kv_spec(ko, -1), kv_spec(ko, 0), kv_spec(ko, 1),
                  kv_spec(vo, -1), kv_spec(vo, 0), kv_spec(vo, 1),
                  pl.BlockSpec((ATT_HEADS, BLOCK, 3 * BLOCK), lambda n, first, last: (0, 0, 0))],
        out_specs=pl.BlockSpec((BLOCK, qw), lambda n, first, last: (n, 0)),
    )
    return pl.pallas_call(
        functools.partial(_attn_kernel, kv_heads=ATT_KV_HEADS, group=group, scale=HEAD_DIM ** -0.5),
        grid_spec=grid_spec,
        out_shape=jax.ShapeDtypeStruct((T, qw), BF16),
        compiler_params=_params(("arbitrary",)),
        name="window_attention",
    )(first, last, sink.astype(F32), proj, proj, proj, proj, proj, proj, proj, bias)


def _rope(x, cos2, sin2):
    return x * cos2 + pltpu.roll(x, RET_DK // 2, axis=1) * sin2


def _ret_bwd_kernel(last_ref, posblk_ref, cdec_ref, q_ref, k_ref, v_ref, cos_ref, sin_ref, qb_ref, kb_ref, o_ref,
                    state_ref, *, chunks):
    h = pl.program_id(0)
    n = pl.num_programs(1) - 1 - pl.program_id(1)
    C = RET_CHUNK

    @pl.when(last_ref[n] == 1)
    def _():
        state_ref[...] = jnp.zeros_like(state_ref)

    state = state_ref[...]
    cdec = cdec_ref[1, h]
    for c in range(chunks - 1, -1, -1):
        sl = slice(c * C, (c + 1) * C)
        cos2, sin2 = cos_ref[sl, :], sin_ref[sl, :]
        q = _rope(q_ref[sl, :], cos2, sin2)
        k = _rope(k_ref[sl, :], cos2, sin2) * (RET_DK ** -0.5)
        vb = v_ref[sl, :].astype(BF16)
        o_ref[sl, :] = jnp.dot((q * qb_ref[0]).astype(BF16), state.astype(BF16), preferred_element_type=F32)
        state = state * cdec + lax.dot_general((k * kb_ref[0]).astype(BF16), vb, TN_DIMS,
                                               preferred_element_type=F32)
    state_ref[...] = state


def _ret_fwd_kernel(first_ref, posblk_ref, cdec_ref, q_ref, k_ref, v_ref, g_ref, cos_ref, sin_ref, dmat_ref, qf_ref,
                    kf_ref, ob_ref, gnw_ref, o_ref, state_ref, *, chunks):
    h = pl.program_id(0)
    n = pl.program_id(1)
    C = RET_CHUNK

    @pl.when(first_ref[n] == 1)
    def _():
        state_ref[...] = jnp.zeros_like(state_ref)

    state = state_ref[...]
    cdec = cdec_ref[0, h]
    for c in range(chunks):
        sl = slice(c * C, (c + 1) * C)
        cos2, sin2 = cos_ref[sl, :], sin_ref[sl, :]
        q = _rope(q_ref[sl, :], cos2, sin2)
        k = _rope(k_ref[sl, :], cos2, sin2) * (RET_DK ** -0.5)
        vb = v_ref[sl, :].astype(BF16)
        intra = lax.dot_general(q.astype(BF16), k.astype(BF16), NT_DIMS, preferred_element_type=F32) * dmat_ref[0]
        out = (jnp.dot(intra.astype(BF16), vb, preferred_element_type=F32)
               + jnp.dot((q * qf_ref[0]).astype(BF16), state.astype(BF16), preferred_element_type=F32)
               + ob_ref[sl, :])
        state = state * cdec + lax.dot_general((k * kf_ref[0]).astype(BF16), vb, TN_DIMS,
                                               preferred_element_type=F32)
        mu = jnp.mean(out, axis=1, keepdims=True)
        cen = out - mu
        var = jnp.mean(cen * cen, axis=1, keepdims=True)
        y = cen * lax.rsqrt(var + GN_EPS) * gnw_ref[...]
        o_ref[sl, :] = (_silu(g_ref[sl, :]) * y).astype(o_ref.dtype)
    state_ref[...] = state


def _retention(proj, first, last, posblk, chunks, ret_decay, ret_gn_w, s_max, col_q, col_k, col_v, col_g):
    T = proj.shape[0]
    C = RET_CHUNK
    R = chunks * C
    nc = T // R
    H = RET_HEADS
    qo, ko, vo, go = col_q // RET_DK, col_k // RET_DK, col_v // RET_DV, col_g // RET_DV
    assert col_q % RET_DK == 0 and col_k % RET_DK == 0 and col_v % RET_DV == 0 and col_g % RET_DV == 0

    half = RET_DK // 2
    inv = ROPE_BASE ** (-jnp.arange(half, dtype=F32) * 2.0 / RET_DK)
    ang = jnp.arange(s_max, dtype=F32)[:, None] * inv[None, :]
    cos2 = jnp.concatenate([jnp.cos(ang), jnp.cos(ang)], axis=1)
    sin2 = jnp.concatenate([-jnp.sin(ang), jnp.sin(ang)], axis=1)

    lg = -jnp.exp(ret_decay.astype(F32))
    pos = jnp.arange(C, dtype=F32)
    diff = pos[:, None] - pos[None, :]
    dec_f = jnp.where((diff >= 0)[None], jnp.exp(jnp.maximum(diff, 0.0)[None] * lg[0][:, None, None]), 0.0)
    dec_b = jnp.where((diff < 0)[None], jnp.exp(jnp.maximum(-diff, 0.0)[None] * lg[1][:, None, None]), 0.0)
    dmat = dec_f + dec_b

    def rows(tab):
        return jnp.broadcast_to(tab[:, :, None], (H, C, RET_DK))

    qf = rows(jnp.exp((pos[None, :] + 1.0) * lg[0][:, None]))
    kf = rows(jnp.exp((C - 1.0 - pos)[None, :] * lg[0][:, None]))
    qb = rows(jnp.exp((C - pos)[None, :] * lg[1][:, None]))
    kb = rows(jnp.exp(pos[None, :] * lg[1][:, None]))
    cdec = jnp.exp(C * lg)

    smem = pl.BlockSpec(memory_space=pltpu.SMEM)
    rev = lambda c: nc - 1 - c
    bwd_specs = [
        smem,
        pl.BlockSpec((R, RET_DK), lambda h, c, last, posb: (rev(c), qo + h)),
        pl.BlockSpec((R, RET_DK), lambda h, c, last, posb: (rev(c), ko + h)),
        pl.BlockSpec((R, RET_DV), lambda h, c, last, posb: (rev(c), vo + h)),
        pl.BlockSpec((R, RET_DK), lambda h, c, last, posb: (posb[rev(c)], 0)),
        pl.BlockSpec((R, RET_DK), lambda h, c, last, posb: (posb[rev(c)], 0)),
        pl.BlockSpec((1, C, RET_DK), lambda h, c, last, posb: (h, 0, 0)),
        pl.BlockSpec((1, C, RET_DK), lambda h, c, last, posb: (h, 0, 0)),
    ]
    out_b = pl.pallas_call(
        functools.partial(_ret_bwd_kernel, chunks=chunks),
        grid_spec=pltpu.PrefetchScalarGridSpec(
            num_scalar_prefetch=2, grid=(H, nc), in_specs=bwd_specs,
            out_specs=pl.BlockSpec((R, RET_DV), lambda h, c, last, posb: (rev(c), h)),
            scratch_shapes=[pltpu.VMEM((RET_DK, RET_DV), F32)]),
        out_shape=jax.ShapeDtypeStruct((T, H * RET_DV), F32),
        compiler_params=_params(("arbitrary", "arbitrary")),
        name="retention_backward",
    )(last, posblk, cdec, proj, proj, proj, cos2, sin2, qb, kb)

    fwd_specs = [
        smem,
        pl.BlockSpec((R, RET_DK), lambda h, n, first, posb: (n, qo + h)),
        pl.BlockSpec((R, RET_DK), lambda h, n, first, posb: (n, ko + h)),
        pl.BlockSpec((R, RET_DV), lambda h, n, first, posb: (n, vo + h)),
        pl.BlockSpec((R, RET_DV), lambda h, n, first, posb: (n, go + h)),
        pl.BlockSpec((R, RET_DK), lambda h, n, first, posb: (posb[n], 0)),
        pl.BlockSpec((R, RET_DK), lambda h, n, first, posb: (posb[n], 0)),
        pl.BlockSpec((1, C, C), lambda h, n, first, posb: (h, 0, 0)),
        pl.BlockSpec((1, C, RET_DK), lambda h, n, first, posb: (h, 0, 0)),
        pl.BlockSpec((1, C, RET_DK), lambda h, n, first, posb: (h, 0, 0)),
        pl.BlockSpec((R, RET_DV), lambda h, n, first, posb: (n, h)),
        pl.BlockSpec((1, RET_DV), lambda h, n, first, posb: (0, h)),
    ]
    return pl.pallas_call(
        functools.partial(_ret_fwd_kernel, chunks=chunks),
        grid_spec=pltpu.PrefetchScalarGridSpec(
            num_scalar_prefetch=2, grid=(H, nc), in_specs=fwd_specs,
            out_specs=pl.BlockSpec((R, RET_DV), lambda h, n, first, posb: (n, h)),
            scratch_shapes=[pltpu.VMEM((RET_DK, RET_DV), F32)]),
        out_shape=jax.ShapeDtypeStruct((T, H * RET_DV), BF16),
        compiler_params=_params(("arbitrary", "arbitrary")),
        name="retention_forward",
    )(first, posblk, cdec, proj, proj, proj, proj, cos2, sin2, dmat, qf, kf, out_b,
      ret_gn_w.astype(F32).reshape(1, -1))


def _merge_kernel(att_ref, wa_ref, ret_ref, wr_ref, ga_ref, gr_ref, o_ref):
    a = jnp.dot(att_ref[...], wa_ref[...], preferred_element_type=F32)
    r = jnp.dot(ret_ref[...], wr_ref[...], preferred_element_type=F32)
    o_ref[...] = (jax.nn.sigmoid(ga_ref[...]) * a + jax.nn.sigmoid(gr_ref[...]) * r).astype(o_ref.dtype)


def _merge(att, ret, proj, wa_tiles, wr_tiles, col_ga, col_gr, tm=1024):
    T = att.shape[0]
    nt, _, tn = wa_tiles.shape
    tm = _tile(T, tm)
    assert col_ga % tn == 0 and col_gr % tn == 0
    ao, ro = col_ga // tn, col_gr // tn
    return pl.pallas_call(
        _merge_kernel,
        grid=(T // tm, nt),
        in_specs=[pl.BlockSpec((tm, att.shape[1]), lambda i, j: (i, 0)),
                  pl.BlockSpec((None, wa_tiles.shape[1], tn), lambda i, j: (j, 0, 0)),
                  pl.BlockSpec((tm, ret.shape[1]), lambda i, j: (i, 0)),
                  pl.BlockSpec((None, wr_tiles.shape[1], tn), lambda i, j: (j, 0, 0)),
                  pl.BlockSpec((tm, tn), lambda i, j: (i, ao + j)),
                  pl.BlockSpec((tm, tn), lambda i, j: (i, ro + j))],
        out_specs=pl.BlockSpec((tm, tn), lambda i, j: (i, j)),
        out_shape=jax.ShapeDtypeStruct((T, nt * tn), BF16),
        compiler_params=_params(("arbitrary", "arbitrary")),
        name="branch_merge",
    )(att, wa_tiles, ret, wr_tiles, proj, proj)


def _layernorm_chunks(zs, d_model, w_of, b_of):
    total = zs[0].sum(axis=1, keepdims=True)
    for z in zs[1:]:
        total = total + z.sum(axis=1, keepdims=True)
    mu = total / d_model
    cens = [z - mu for z in zs]
    sq = (cens[0] * cens[0]).sum(axis=1, keepdims=True)
    for cen in cens[1:]:
        sq = sq + (cen * cen).sum(axis=1, keepdims=True)
    rstd = lax.rsqrt(sq / d_model + LN_EPS)
    return [cen * rstd * w_of(c) + b_of(c) for c, cen in enumerate(cens)]


def _wo_ln_kernel(m_ref, w_ref, xa_ref, xb_ref, lw_ref, lb_ref, o_ref, slab_ref, z_ref, *, rows, pitch, steps1):
    i = pl.program_id(0)
    n = pl.program_id(1)
    nn, tm, tn = z_ref.shape
    mm = jnp.dot(m_ref[...], w_ref[...], preferred_element_type=F32)

    @pl.when(i < steps1)
    def _():
        z_ref[n] = ALPHA * xa_ref[...] + mm

    @pl.when(i >= steps1)
    def _():
        z_ref[n] = ALPHA * xb_ref[...] + mm

    @pl.when(n == nn - 1)
    def _():
        ys = _layernorm_chunks([z_ref[c] for c in range(nn)], nn * tn,
                               lambda c: lw_ref[:, c * tn:(c + 1) * tn], lambda c: lb_ref[:, c * tn:(c + 1) * tn])
        for c in range(nn):
            o_ref[:, c * tn:(c + 1) * tn] = ys[c]
        per = tn // LANES

        def col(j):
            return ys[j // per][:, (j % per) * LANES:(j % per + 1) * LANES]

        _store_slabs(slab_ref, col, tm, rows, pitch)


def _wo_layernorm(merged, wo_tiles, xa, xb, ln_w, ln_b, tm=256):
    T, K = merged.shape
    nt, _, tn = wo_tiles.shape
    D = nt * tn
    n1 = xa.shape[0]
    tm = _tile(math.gcd(n1, T - n1), tm)
    steps1 = n1 // tm
    rows = _slab_rows(D)
    pitch = _slab_pitch(rows)
    return pl.pallas_call(
        functools.partial(_wo_ln_kernel, rows=rows, pitch=pitch, steps1=steps1),
        grid=(T // tm, nt),
        in_specs=[pl.BlockSpec((tm, K), lambda i, n: (i, 0)),
                  pl.BlockSpec((None, K, tn), lambda i, n: (n, 0, 0)),
                  pl.BlockSpec((tm, tn), lambda i, n: (jnp.minimum(i, steps1 - 1), jnp.where(i < steps1, n, nt - 1))),
                  pl.BlockSpec((tm, tn), lambda i, n: (jnp.maximum(i - steps1, 0), jnp.where(i < steps1, 0, n))),
                  pl.BlockSpec((1, D), lambda i, n: (0, 0)),
                  pl.BlockSpec((1, D), lambda i, n: (0, 0))],
        out_specs=[pl.BlockSpec((tm, D), lambda i, n: (i, 0)),
                   pl.BlockSpec((tm * pitch, LANES), lambda i, n: (i, 0))],
        out_shape=[jax.ShapeDtypeStruct((T, D), F32),
                   jax.ShapeDtypeStruct((T * pitch, LANES), I32)],
        scratch_shapes=[pltpu.VMEM((nt, tm, tn), F32)],
        compiler_params=_params(("arbitrary", "arbitrary")),
        name="out_proj_layernorm",
    )(merged, wo_tiles, xa, xb, ln_w.astype(F32).reshape(1, -1), ln_b.astype(F32).reshape(1, -1))


def _router_kernel(x_ref, wh_ref, wl_ref, bias_ref, idx_ref, rank_ref, gate_ref, cnt_ref, xb_ref, carry_ref):
    i = pl.program_id(0)
    tm = x_ref.shape[0]
    E = N_EXPERTS
    gsz = E // N_GROUPS
    neg_inf = jnp.float32(-jnp.inf)

    @pl.when(i == 0)
    def _():
        carry_ref[...] = jnp.zeros_like(carry_ref)

    x = x_ref[...]
    xh = x.astype(BF16)
    xb_ref[...] = xh
    xl = (x - xh.astype(F32)).astype(BF16)
    logits = (lax.dot_general(wh_ref[...], xh, NT_DIMS, preferred_element_type=F32)
              + lax.dot_general(wh_ref[...], xl, NT_DIMS, preferred_element_type=F32)
              + lax.dot_general(wl_ref[...], xh, NT_DIMS, preferred_element_type=F32))
    scores = jax.nn.sigmoid(logits)
    biased = scores + bias_ref[...]

    row = lax.broadcasted_iota(I32, (gsz, tm), 0)
    gscore = []
    for g in range(N_GROUPS):
        blk = biased[g * gsz:(g + 1) * gsz, :]
        m1 = jnp.max(blk, axis=0, keepdims=True)
        first = jnp.min(jnp.where(blk == m1, row, gsz), axis=0, keepdims=True)
        m2 = jnp.max(jnp.where(row == first, neg_inf, blk), axis=0, keepdims=True)
        gscore.append(m1 + m2)
    gid = lax.broadcasted_iota(I32, (E, tm), 0) // gsz
    keep = jnp.zeros((E, tm), I32)
    for g in range(N_GROUPS):
        beaten = jnp.zeros((1, tm), I32)
        for o in range(N_GROUPS):
            if o == g:
                continue
            wins = (gscore[o] >= gscore[g]) if o < g else (gscore[o] > gscore[g])
            beaten = beaten + wins.astype(I32)
        keep = jnp.where(gid == g, (beaten < TOPK_GROUPS).astype(I32), keep)
    masked = jnp.where(keep > 0, biased, neg_inf)

    eidx = lax.broadcasted_iota(I32, (E, tm), 0)
    sel = jnp.zeros((E, tm), jnp.bool_)
    picks, weights = [], []
    for _ in range(TOP_K):
        m = jnp.max(masked, axis=0, keepdims=True)
        pick = jnp.min(jnp.where(masked == m, eidx, E), axis=0, keepdims=True)
        onehot = eidx == pick
        weights.append(jnp.sum(jnp.where(onehot, scores, 0.0), axis=0, keepdims=True))
        picks.append(pick)
        masked = jnp.where(onehot, neg_inf, masked)
        sel = sel | onehot
    wsum = weights[0]
    for w in weights[1:]:
        wsum = wsum + w

    self32 = sel.astype(F32)
    tri = (lax.broadcasted_iota(I32, (tm, tm), 0) < lax.broadcasted_iota(I32, (tm, tm), 1)).astype(F32).astype(BF16)
    prefix = jnp.dot(self32.astype(BF16), tri, preferred_element_type=F32) + carry_ref[...]
    for kk in range(TOP_K):
        onehot = eidx == picks[kk]
        idx_ref[kk:kk + 1, :] = picks[kk]
        rank_ref[kk:kk + 1, :] = jnp.sum(jnp.where(onehot, prefix, 0.0), axis=0, keepdims=True).astype(I32)
        gate_ref[kk:kk + 1, :] = weights[kk] / wsum * ROUTED_SCALE
    carry_ref[...] = carry_ref[...] + jnp.sum(self32, axis=1, keepdims=True)
    cnt_ref[...] = carry_ref[...]


def _router(x1, w_router, router_bias, tm=256):
    T, D = x1.shape
    E = N_EXPERTS
    tm = _tile(T, tm)
    wt = w_router.astype(F32).T
    wh = wt.astype(BF16)
    wl = (wt - wh.astype(F32)).astype(BF16)
    return pl.pallas_call(
        _router_kernel,
        grid=(T // tm,),
        in_specs=[pl.BlockSpec((tm, D), lambda i: (i, 0)),
                  pl.BlockSpec((E, D), lambda i: (0, 0)),
                  pl.BlockSpec((E, D), lambda i: (0, 0)),
                  pl.BlockSpec((E, 1), lambda i: (0, 0))],
        out_specs=[pl.BlockSpec((TOP_K, tm), lambda i: (0, i)),
                   pl.BlockSpec((TOP_K, tm), lambda i: (0, i)),
                   pl.BlockSpec((TOP_K, tm), lambda i: (0, i)),
                   pl.BlockSpec((E, 1), lambda i: (0, 0)),
                   pl.BlockSpec((tm, D), lambda i: (i, 0))],
        out_shape=[jax.ShapeDtypeStruct((TOP_K, T), I32),
                   jax.ShapeDtypeStruct((TOP_K, T), I32),
                   jax.ShapeDtypeStruct((TOP_K, T), F32),
                   jax.ShapeDtypeStruct((E, 1), F32),
                   jax.ShapeDtypeStruct((T, D), BF16)],
        scratch_shapes=[pltpu.VMEM((E, 1), F32)],
        compiler_params=_params(("arbitrary",)),
        name="router",
    )(x1, wh, wl, router_bias.astype(F32).reshape(E, 1))


def _slot_kernel(start_ref, idx_ref, rank_ref, slot_ref):
    idx = idx_ref[...]
    acc = rank_ref[...]
    for e in range(N_EXPERTS):
        acc = acc + jnp.where(idx == e, start_ref[e], 0)
    slot_ref[...] = acc


def _slots(pad_start, idx, rank, tm=2048):
    T = idx.shape[1]
    tm = _tile(T, tm)
    return pl.pallas_call(
        _slot_kernel,
        grid=(T // tm,),
        in_specs=[pl.BlockSpec(memory_space=pltpu.SMEM),
                  pl.BlockSpec((TOP_K, tm), lambda i: (0, i)),
                  pl.BlockSpec((TOP_K, tm), lambda i: (0, i))],
        out_specs=pl.BlockSpec((TOP_K, tm), lambda i: (0, i)),
        out_shape=jax.ShapeDtypeStruct((TOP_K, T), I32),
        compiler_params=_params(("arbitrary",)),
        name="slot_index",
    )(pad_start, idx, rank)


def _zero_fill_plan(npad, start, te, fn):
    pos = start
    bit = te // 2
    while bit >= 1:
        take = (npad & bit) != 0

        @pl.when(take)
        def _(pos=pos, bit=bit):
            fn(pos, bit)

        pos = pos + jnp.where(take, bit, 0)
        bit //= 2


def _dispatch_kernel(fill_start_ref, fill_len_ref, slot_ref, x_ref, xs_ref, zero_ref, sem, zsem, *, te, rows, pitch):
    i = pl.program_id(0)
    tm = x_ref.shape[0] // pitch

    def start_token(t, carry):
        src = x_ref.at[pl.ds(pl.multiple_of(t * pitch, SUBLANES), rows)]
        for k in range(TOP_K):
            dst = xs_ref.at[pl.ds(pl.multiple_of(slot_ref[t * TOP_K + k] * pitch, SUBLANES), rows)]
            pltpu.make_async_copy(src, dst, sem).start(priority=k % 2)
        return carry

    lax.fori_loop(0, tm, start_token, 0)

    @pl.when(i == 0)
    def _():
        zero_ref[...] = jnp.zeros_like(zero_ref)

        def zero_copy(pos, n):
            return pltpu.make_async_copy(zero_ref.at[pl.ds(0, n * pitch)],
                                         xs_ref.at[pl.ds(pl.multiple_of(pos * pitch, SUBLANES), n * pitch)], zsem)

        def start_fill(e, carry):
            _zero_fill_plan(fill_len_ref[e], fill_start_ref[e], te, lambda pos, n: zero_copy(pos, n).start())
            return carry

        def wait_fill(e, carry):
            _zero_fill_plan(fill_len_ref[e], fill_start_ref[e], te, lambda pos, n: zero_copy(pos, n).wait())
            return carry

        lax.fori_loop(0, N_EXPERTS, start_fill, 0)
        lax.fori_loop(0, N_EXPERTS, wait_fill, 0)

    def wait_token(t, carry):
        for k in range(TOP_K):
            pltpu.make_async_copy(x_ref.at[pl.ds(0, rows)], xs_ref.at[pl.ds(0, rows)], sem).wait()
        return carry

    lax.fori_loop(0, tm, wait_token, 0)


def _dispatch(x_slab, slot_flat, fill_start, fill_len, n_slots, te, rows, pitch, tm=256):
    T = x_slab.shape[0] // pitch
    tm = _tile(T, tm)
    return pl.pallas_call(
        functools.partial(_dispatch_kernel, te=te, rows=rows, pitch=pitch),
        grid_spec=pltpu.PrefetchScalarGridSpec(
            num_scalar_prefetch=2,
            grid=(T // tm,),
            in_specs=[pl.BlockSpec((tm * TOP_K,), lambda i, fs, fl: (i,), memory_space=pltpu.SMEM),
                      pl.BlockSpec((tm * pitch, LANES), lambda i, fs, fl: (i, 0))],
            out_specs=pl.BlockSpec(memory_space=pl.ANY),
            scratch_shapes=[pltpu.VMEM((te // 2 * pitch, LANES), I32),
                            pltpu.SemaphoreType.DMA(()),
                            pltpu.SemaphoreType.DMA(())]),
        out_shape=jax.ShapeDtypeStruct((n_slots * pitch, LANES), I32),
        compiler_params=_params(("arbitrary",)),
        name="moe_dispatch",
    )(fill_start, fill_len, slot_flat, x_slab)


def _ffn_partial(xb, wg, wu, wd):
    hid = _silu(jnp.dot(xb, wg, preferred_element_type=F32)) * jnp.dot(xb, wu, preferred_element_type=F32)
    return jnp.dot(hid.astype(BF16), wd, preferred_element_type=F32)


def _expert_kernel(blk_e_ref, nvalid_ref, x_ref, wg_ref, wu_ref, wd_ref, o_ref, xb_ref, acc_ref, *, rows, pitch):
    b = pl.program_id(0)
    j = pl.program_id(1)
    te, d_model = xb_ref.shape
    half = d_model // 2

    @pl.when(b < nvalid_ref[0])
    def _():
        @pl.when(j == 0)
        def _():
            for r in range(rows):
                lo, hi = _unpack_pair(x_ref[pl.ds(r, te, stride=pitch), :])
                xb_ref[:, r * LANES:(r + 1) * LANES] = lo.astype(BF16)
                xb_ref[:, half + r * LANES:half + (r + 1) * LANES] = hi.astype(BF16)
            acc_ref[...] = jnp.zeros_like(acc_ref)

        acc_ref[...] += _ffn_partial(xb_ref[...], wg_ref[...], wu_ref[...], wd_ref[...])

        @pl.when(j == pl.num_programs(1) - 1)
        def _():
            _store_slabs(o_ref, lambda c: acc_ref[:, c * LANES:(c + 1) * LANES], te, rows, pitch)


def _routed_ffn(xs, blk_e, nvalid, wg, wu, wd, te, rows, pitch):
    P = xs.shape[0] // pitch
    _, nj, D, tj = wg.shape
    nblk = P // te

    def bclamp(b, nv):
        return jnp.minimum(b, nv[0] - 1)

    def jclamp(b, j, nv):
        return jnp.where(b < nv[0], j, nj - 1)

    return pl.pallas_call(
        functools.partial(_expert_kernel, rows=rows, pitch=pitch),
        grid_spec=pltpu.PrefetchScalarGridSpec(
            num_scalar_prefetch=2,
            grid=(nblk, nj),
            in_specs=[pl.BlockSpec((te * pitch, LANES), lambda b, j, be, nv: (bclamp(b, nv), 0)),
                      pl.BlockSpec((None, None, D, tj),
                                   lambda b, j, be, nv: (be[bclamp(b, nv)], jclamp(b, j, nv), 0, 0)),
                      pl.BlockSpec((None, None, D, tj),
                                   lambda b, j, be, nv: (be[bclamp(b, nv)], jclamp(b, j, nv), 0, 0)),
                      pl.BlockSpec((None, tj, D), lambda b, j, be, nv: (be[bclamp(b, nv)], jclamp(b, j, nv), 0))],
            out_specs=pl.BlockSpec((te * pitch, LANES), lambda b, j, be, nv: (bclamp(b, nv), 0)),
            scratch_shapes=[pltpu.VMEM((te, D), BF16), pltpu.VMEM((te, D), F32)]),
        out_shape=jax.ShapeDtypeStruct((P * pitch, LANES), I32),
        compiler_params=_params(("arbitrary", "arbitrary")),
        name="routed_experts",
    )(blk_e, nvalid, xs, wg, wu, wd)


def _shared_kernel(x_ref, wg_ref, wu_ref, wd_ref, o_ref):
    @pl.when(pl.program_id(1) == 0)
    def _():
        o_ref[...] = jnp.zeros_like(o_ref)

    o_ref[...] += _ffn_partial(x_ref[...], wg_ref[...], wu_ref[...], wd_ref[...])


def _shared_ffn(xb, wg, wu, wd, tm=512):
    T, D = xb.shape
    nj, _, tj = wg.shape
    tm = _tile(T, tm)
    return pl.pallas_call(
        _shared_kernel,
        grid=(T // tm, nj),
        in_specs=[pl.BlockSpec((tm, D), lambda i, j: (i, 0)),
                  pl.BlockSpec((None, D, tj), lambda i, j: (j, 0, 0)),
                  pl.BlockSpec((None, D, tj), lambda i, j: (j, 0, 0)),
                  pl.BlockSpec((tj, D), lambda i, j: (j, 0))],
        out_specs=pl.BlockSpec((tm, D), lambda i, j: (i, 0)),
        out_shape=jax.ShapeDtypeStruct((T, D), F32),
        compiler_params=_params(("arbitrary", "arbitrary")),
        name="shared_expert",
    )(xb, wg, wu, wd)


def _combine_kernel(slot_ref, slot_next_ref, gate_ref, x_ref, sh_ref, lw_ref, lb_ref, ys_ref, o1_ref, o2_ref,
                    buf_ref, z_ref, sem, *, rows, pitch, steps1):
    i = pl.program_id(0)
    nsteps = pl.num_programs(0)
    tm, d_model = x_ref.shape
    half = d_model // 2
    cur = i % 2

    def start_all(srefs, buf):
        def body(t, carry):
            for kk in range(TOP_K):
                src = ys_ref.at[pl.ds(pl.multiple_of(srefs[t * TOP_K + kk] * pitch, SUBLANES), rows)]
                dst = buf_ref.at[buf, kk, pl.ds(pl.multiple_of(t * pitch, SUBLANES), rows)]
                pltpu.make_async_copy(src, dst, sem.at[buf]).start(priority=kk % 2)
            return carry
        lax.fori_loop(0, tm, body, 0)

    @pl.when(i == 0)
    def _():
        start_all(slot_ref, 0)

    @pl.when(i + 1 < nsteps)
    def _():
        start_all(slot_next_ref, 1 - cur)

    def wait_body(t, carry):
        for kk in range(TOP_K):
            pltpu.make_async_copy(ys_ref.at[pl.ds(0, rows)], buf_ref.at[cur, 0, pl.ds(0, rows)], sem.at[cur]).wait()
        return carry

    lax.fori_loop(0, tm, wait_body, 0)

    gates = gate_ref[...]
    gk = [gates[:, kk:kk + 1] for kk in range(TOP_K)]
    for r in range(rows):
        lo_acc = hi_acc = None
        for kk in range(TOP_K):
            lo, hi = _unpack_pair(buf_ref[cur, kk, pl.ds(r, tm, stride=pitch), :])
            lo_acc = gk[kk] * lo if lo_acc is None else lo_acc + gk[kk] * lo
            hi_acc = gk[kk] * hi if hi_acc is None else hi_acc + gk[kk] * hi
        for c0, acc in ((r * LANES, lo_acc), (half + r * LANES, hi_acc)):
            cols = slice(c0, c0 + LANES)
            z_ref[:, cols] = ALPHA * x_ref[:, cols] + (acc + sh_ref[:, cols])
    z = z_ref[...]
    mu = jnp.mean(z, axis=1, keepdims=True)
    cen = z - mu
    var = jnp.mean(cen * cen, axis=1, keepdims=True)
    y = cen * lax.rsqrt(var + LN_EPS) * lw_ref[...] + lb_ref[...]

    @pl.when(i < steps1)
    def _():
        o1_ref[...] = y

    @pl.when(i >= steps1)
    def _():
        o2_ref[...] = y


def _combine(ys, slot_flat, gate_tk, x1, shared, ln_w, ln_b, n1, rows, pitch, tm=128):
    T, D = x1.shape
    tm = _tile(math.gcd(n1, T - n1), tm)
    nsteps = T // tm
    steps1 = n1 // tm
    return pl.pallas_call(
        functools.partial(_combine_kernel, rows=rows, pitch=pitch, steps1=steps1),
        grid=(nsteps,),
        in_specs=[pl.BlockSpec((tm * TOP_K,), lambda i: (i,), memory_space=pltpu.SMEM),
                  pl.BlockSpec((tm * TOP_K,), lambda i: (jnp.minimum(i + 1, nsteps - 1),), memory_space=pltpu.SMEM),
                  pl.BlockSpec((tm, TOP_K), lambda i: (i, 0)),
                  pl.BlockSpec((tm, D), lambda i: (i, 0)),
                  pl.BlockSpec((tm, D), lambda i: (i, 0)),
                  pl.BlockSpec((1, D), lambda i: (0, 0)),
                  pl.BlockSpec((1, D), lambda i: (0, 0)),
                  pl.BlockSpec(memory_space=pl.ANY)],
        out_specs=[pl.BlockSpec((tm, D), lambda i: (jnp.minimum(i, steps1 - 1), 0)),
                   pl.BlockSpec((tm, D), lambda i: (jnp.maximum(i - steps1, 0), 0))],
        out_shape=[jax.ShapeDtypeStruct((n1, D), F32), jax.ShapeDtypeStruct((T - n1, D), F32)],
        scratch_shapes=[pltpu.VMEM((2, TOP_K, tm * pitch, LANES), I32), pltpu.VMEM((tm, D), F32),
                        pltpu.SemaphoreType.DMA((2,))],
        compiler_params=_params(("arbitrary",)),
        name="moe_combine_layernorm",
    )(slot_flat, slot_flat, gate_tk, x1, shared, ln_w.astype(F32).reshape(1, -1), ln_b.astype(F32).reshape(1, -1),
      ys)


def _seq_tables(groups, blk):
    first, last, pos = [], [], []
    for nseq, slen in groups:
        nb = slen // blk
        for _ in range(nseq):
            for b in range(nb):
                first.append(int(b == 0))
                last.append(int(b == nb - 1))
                pos.append(b)
    return (jnp.asarray(np.array(first, np.int32)), jnp.asarray(np.array(last, np.int32)),
            jnp.asarray(np.array(pos, np.int32)))


def _expert_block_rows(d_model):
    return 512 if d_model >= 4096 else 128


def kernel(x_prompt, x_sample, rel_table, w_in, attn_sink, ret_decay, ret_gn_w, w_att_out, w_ret_out, w_o, ln1_w,
           ln1_b, w_router, router_bias, w_exp_gate, w_exp_up, w_exp_down, w_sh_gate, w_sh_up, w_sh_down, ln2_w,
           ln2_b):
    assert DEPTH == 1
    B1, S1, D = x_prompt.shape
    B2, S2, _ = x_sample.shape
    assert S1 % BLOCK == 0 and S2 % BLOCK == 0 and BLOCK == RET_CHUNK
    xa, xb = x_prompt.reshape(B1 * S1, D), x_sample.reshape(B2 * S2, D)
    n1 = B1 * S1
    T = n1 + B2 * S2
    groups = ((B1, S1), (B2, S2))
    first, last, _ = _seq_tables(groups, BLOCK)
    ret_chunks = math.gcd(4, math.gcd(S1 // RET_CHUNK, S2 // RET_CHUNK))
    rfirst, rlast, rpos = _seq_tables(groups, ret_chunks * RET_CHUNK)

    att_q, att_kv = ATT_HEADS * HEAD_DIM, ATT_KV_HEADS * HEAD_DIM
    ret_qk, ret_v = RET_HEADS * RET_DK, RET_HEADS * RET_DV
    cols = np.concatenate([[0], np.cumsum([att_q, att_kv, att_kv, ret_qk, ret_qk, ret_v, ret_v, D, D])])
    c_aq, c_ak, c_av, c_rq, c_rk, c_rv, c_rg, c_ga, c_gr = (int(c) for c in cols[:-1])
    assert int(cols[-1]) == w_in.shape[2]
    rows = _slab_rows(D)
    pitch = _slab_pitch(rows)

    x_bf = jnp.concatenate([xa.astype(BF16), xb.astype(BF16)], axis=0)
    proj = _matmul(x_bf, _col_tiles(w_in[0], 1024), F32)
    att = _attention(proj, first, last, attn_sink[0], rel_table, c_aq, c_ak, c_av)
    ret = _retention(proj, rfirst, rlast, rpos, ret_chunks, ret_decay[0], ret_gn_w[0], max(S1, S2),
                     c_rq, c_rk, c_rv, c_rg)
    merged = _merge(att, ret, proj, _col_tiles(w_att_out[0], 512), _col_tiles(w_ret_out[0], 512), c_ga, c_gr)
    x1, x1_slab = _wo_layernorm(merged, _col_tiles(w_o[0], 512), xa, xb, ln1_w[0], ln1_b[0])

    idx, rank, gate, counts, x1b = _router(x1, w_router[0], router_bias[0])
    te = _expert_block_rows(D)
    cnt = counts[:, 0].astype(I32)
    padded = (cnt + te - 1) // te * te
    pad_end = jnp.cumsum(padded)
    pad_start = pad_end - padded
    nblk = (T * TOP_K) // te + N_EXPERTS
    blk_e = jnp.minimum(jnp.searchsorted(pad_end, jnp.arange(nblk, dtype=I32) * te, side='right'),
                        N_EXPERTS - 1).astype(I32)
    nvalid = (pad_end[-1:] // te).astype(I32)
    slot = _slots(pad_start.astype(I32), idx, rank)
    slot_flat = slot.T.reshape(-1)

    xs = _dispatch(x1_slab, slot_flat, (pad_start + cnt).astype(I32), (padded - cnt).astype(I32), nblk * te, te,
                   rows, pitch)
    ys = _routed_ffn(xs, blk_e, nvalid, _col_tiles(w_exp_gate[0], 256), _col_tiles(w_exp_up[0], 256),
                     w_exp_down[0].astype(BF16), te, rows, pitch)
    shared = _shared_ffn(x1b, _col_tiles(w_sh_gate[0], 256), _col_tiles(w_sh_up[0], 256), w_sh_down[0].astype(BF16))
    y1, y2 = _combine(ys, slot_flat, gate.T, x1, shared, ln2_w[0], ln2_b[0], n1, rows, pitch)
    return (y1.reshape(B1, S1, D), y2.reshape(B2, S2, D))
```

```python
import functools
import math

import numpy as np
import jax
import jax.numpy as jnp
from jax import lax
from jax.experimental import pallas as pl
from jax.experimental.pallas import tpu as pltpu

F32 = jnp.float32
BF16 = jnp.bfloat16
I32 = jnp.int32

HEAD_DIM = 128
ATT_HEADS = 16
ATT_KV_HEADS = 4
WINDOW = 128
BLOCK = 128
REL_BUCKETS = 32
REL_MAX_DIST = 128
RET_HEADS = 8
RET_DK = 128
RET_DV = 256
RET_CHUNK = 128
ROPE_BASE = 10000.0
N_EXPERTS = 128
TOP_K = 8
N_GROUPS = 8
TOPK_GROUPS = 4
ROUTED_SCALE = 2.5
LN_EPS = 1e-5
GN_EPS = 1e-5
DEPTH = 1
ALPHA = (2.0 * DEPTH) ** 0.25

V7X_VMEM_BUDGET_BYTES = 56 * 1024 * 1024
LANES = 128
SUBLANES = 8

NT_DIMS = (((1,), (1,)), ((), ()))
TN_DIMS = (((0,), (0,)), ((), ()))


def _tile(dim, pref):
    t = min(dim, pref)
    assert dim % t == 0, (dim, pref)
    return t


def _params(sem, vmem=V7X_VMEM_BUDGET_BYTES):
    return pltpu.CompilerParams(dimension_semantics=sem, vmem_limit_bytes=vmem)


def _silu(v):
    return v * jax.nn.sigmoid(v)


def _slab_rows(d_model):
    rows = d_model // (2 * LANES)
    assert rows % SUBLANES == 0
    return rows


def _slab_pitch(rows):
    return rows if (rows // SUBLANES) % 2 == 1 else rows + SUBLANES


def _pack_pair(lo, hi):
    lo_bits = lax.bitcast_convert_type(lo.astype(BF16).astype(F32), I32)
    hi_bits = lax.bitcast_convert_type(hi.astype(BF16).astype(F32), I32)
    return hi_bits | lax.shift_right_logical(lo_bits, 16)


def _unpack_pair(w):
    lo = lax.bitcast_convert_type(lax.shift_left(w, 16), F32)
    hi = lax.bitcast_convert_type(w & jnp.int32(-65536), F32)
    return lo, hi


def _store_slabs(slab_ref, col, n_tok, rows, pitch):
    for r in range(rows):
        slab_ref[pl.ds(r, n_tok, stride=pitch), :] = _pack_pair(col(r), col(rows + r))
    for r in range(rows, pitch):
        slab_ref[pl.ds(r, n_tok, stride=pitch), :] = jnp.zeros((n_tok, LANES), I32)


def _matmul_kernel(x_ref, w_ref, o_ref):
    o_ref[...] = jnp.dot(x_ref[...], w_ref[...], preferred_element_type=F32).astype(o_ref.dtype)


def _matmul(x, w, out_dtype, tm=1024, tn=1024):
    M, K = x.shape
    N = w.shape[1]
    tm, tn = _tile(M, tm), _tile(N, tn)
    return pl.pallas_call(
        _matmul_kernel,
        grid=(M // tm, N // tn),
        in_specs=[pl.BlockSpec((tm, K), lambda i, j: (i, 0)),
                  pl.BlockSpec((K, tn), lambda i, j: (0, j))],
        out_specs=pl.BlockSpec((tm, tn), lambda i, j: (i, j)),
        out_shape=jax.ShapeDtypeStruct((M, N), out_dtype),
        compiler_params=_params(("arbitrary", "arbitrary")),
        name="proj_matmul",
    )(x, w)


def _residual_matmul_kernel(m_ref, w_ref, xa_ref, xb_ref, o_ref, *, steps1):
    mm = jnp.dot(m_ref[...], w_ref[...], preferred_element_type=F32)

    @pl.when(pl.program_id(0) < steps1)
    def _():
        o_ref[...] = ALPHA * xa_ref[...] + mm

    @pl.when(pl.program_id(0) >= steps1)
    def _():
        o_ref[...] = ALPHA * xb_ref[...] + mm


def _residual_matmul(merged, w_o, xa, xb, tm=512, tn=1024):
    T, K = merged.shape
    D = w_o.shape[1]
    n1 = xa.shape[0]
    tm, tn = _tile(math.gcd(n1, T - n1), tm), _tile(D, tn)
    steps1, nt = n1 // tm, D // tn
    return pl.pallas_call(
        functools.partial(_residual_matmul_kernel, steps1=steps1),
        grid=(T // tm, nt),
        in_specs=[pl.BlockSpec((tm, K), lambda i, n: (i, 0)),
                  pl.BlockSpec((K, tn), lambda i, n: (0, n)),
                  pl.BlockSpec((tm, tn), lambda i, n: (jnp.minimum(i, steps1 - 1), jnp.where(i < steps1, n, nt - 1))),
                  pl.BlockSpec((tm, tn), lambda i, n: (jnp.maximum(i - steps1, 0), jnp.where(i < steps1, 0, n)))],
        out_specs=pl.BlockSpec((tm, tn), lambda i, n: (i, n)),
        out_shape=jax.ShapeDtypeStruct((T, D), F32),
        compiler_params=_params(("arbitrary", "arbitrary")),
        name="out_proj_residual",
    )(merged, w_o, xa, xb)


def _attn_kernel(first_ref, last_ref, sink_ref, q_ref, kp_ref, kc_ref, kn_ref, vp_ref, vc_ref, vn_ref,
                 bias_ref, o_ref, *, kv_heads, group, scale):
    n = pl.program_id(0)
    neg_inf = jnp.float32(-jnp.inf)
    pen_p = jnp.where(first_ref[n] == 1, neg_inf, jnp.float32(0.0))
    pen_n = jnp.where(last_ref[n] == 1, neg_inf, jnp.float32(0.0))
    lane = lax.broadcasted_iota(I32, (1, 3 * BLOCK), 1)
    pen = jnp.where(lane < BLOCK, pen_p, jnp.where(lane >= 2 * BLOCK, pen_n, jnp.float32(0.0)))
    for h in range(kv_heads):
        cols = slice(h * HEAD_DIM, (h + 1) * HEAD_DIM)
        kcat = jnp.concatenate([kp_ref[:, cols], kc_ref[:, cols], kn_ref[:, cols]], axis=0).astype(BF16)
        vcat = jnp.concatenate([vp_ref[:, cols], vc_ref[:, cols], vn_ref[:, cols]], axis=0).astype(BF16)
        heads = [h * group + g for g in range(group)]
        qcols = [slice(hd * HEAD_DIM, (hd + 1) * HEAD_DIM) for hd in heads]
        sinks = [sink_ref[hd] for hd in heads]
        s = [lax.dot_general(q_ref[:, c].astype(BF16), kcat, NT_DIMS, preferred_element_type=F32) * scale
             + bias_ref[hd] + pen for hd, c in zip(heads, qcols)]
        m = [jnp.maximum(jnp.max(si, axis=1, keepdims=True), sk) for si, sk in zip(s, sinks)]
        e = [jnp.exp(si - mi) for si, mi in zip(s, m)]
        denom = [jnp.sum(ei, axis=1, keepdims=True) + jnp.exp(sk - mi) for ei, sk, mi in zip(e, sinks, m)]
        p = [(ei * (1.0 / di)).astype(BF16) for ei, di in zip(e, denom)]
        for c, pi in zip(qcols, p):
            o_ref[:, c] = jnp.dot(pi, vcat, preferred_element_type=F32).astype(o_ref.dtype)


def _t5_bucket(rel):
    nb = REL_BUCKETS // 2
    max_exact = nb // 2
    ret = (rel > 0).astype(I32) * nb
    n = jnp.abs(rel)
    nf = jnp.maximum(n, max_exact).astype(F32)
    large = max_exact + (jnp.log(nf / max_exact) / math.log(REL_MAX_DIST / max_exact) * (nb - max_exact)).astype(I32)
    large = jnp.minimum(large, nb - 1)
    return ret + jnp.where(n < max_exact, n, large)


def _attention(proj, first, last, sink, rel_table, col_q, col_k, col_v):
    T = proj.shape[0]
    nb = T // BLOCK
    group = ATT_HEADS // ATT_KV_HEADS
    qw, kw = ATT_HEADS * HEAD_DIM, ATT_KV_HEADS * HEAD_DIM
    assert col_q % qw == 0 and col_k % kw == 0 and col_v % kw == 0
    qo, ko, vo = col_q // qw, col_k // kw, col_v // kw
    rel = jnp.arange(3 * BLOCK)[None, :] - BLOCK - jnp.arange(BLOCK)[:, None]
    bias = rel_table.astype(F32)[_t5_bucket(rel)]
    bias = jnp.where((jnp.abs(rel) <= WINDOW)[:, :, None], bias, -jnp.inf)
    bias = jnp.transpose(bias, (2, 0, 1))

    def kv_spec(off, shift):
        def imap(n, first, last):
            return (jnp.clip(n + shift, 0, nb - 1), off)
        return pl.BlockSpec((BLOCK, kw), imap)

    grid_spec = pltpu.PrefetchScalarGridSpec(
        num_scalar_prefetch=2,
        grid=(nb,),
        in_specs=[pl.BlockSpec(memory_space=pltpu.SMEM),
                  pl.BlockSpec((BLOCK, qw), lambda n, first, last: (n, qo)),
                  kv_spec(ko, -1), kv_spec(ko, 0), kv_spec(ko, 1),
                  kv_spec(vo, -1), kv_spec(vo, 0), kv_spec(vo, 1),
                  pl.BlockSpec((ATT_HEADS, BLOCK, 3 * BLOCK), lambda n, first, last: (0, 0, 0))],
        out_specs=pl.BlockSpec((BLOCK, qw), lambda n, first, last: (n, 0)),
    )
    return pl.pallas_call(
        functools.partial(_attn_kernel, kv_heads=ATT_KV_HEADS, group=group, scale=HEAD_DIM ** -0.5),
        grid_spec=grid_spec,
        out_shape=jax.ShapeDtypeStruct((T, qw), BF16),
        compiler_params=_params(("arbitrary",)),
        name="window_attention",
    )(first, last, sink.astype(F32), proj, proj, proj, proj, proj, proj, proj, bias)


def _rope(x, cos2, sin2):
    return x * cos2 + pltpu.roll(x, RET_DK // 2, axis=1) * sin2


def _ret_bwd_kernel(last_ref, posblk_ref, cdec_ref, q_ref, k_ref, v_ref, cos_ref, sin_ref, qb_ref, kb_ref, o_ref,
                    state_ref, *, chunks):
    h = pl.program_id(0)
    n = pl.num_programs(1) - 1 - pl.program_id(1)
    C = RET_CHUNK

    @pl.when(last_ref[n] == 1)
    def _():
        state_ref[...] = jnp.zeros_like(state_ref)

    state = state_ref[...]
    cdec = cdec_ref[1, h]
    for c in range(chunks - 1, -1, -1):
        sl = slice(c * C, (c + 1) * C)
        cos2, sin2 = cos_ref[sl, :], sin_ref[sl, :]
        q = _rope(q_ref[sl, :], cos2, sin2)
        k = _rope(k_ref[sl, :], cos2, sin2) * (RET_DK ** -0.5)
        vb = v_ref[sl, :].astype(BF16)
        o_ref[sl, :] = jnp.dot((q * qb_ref[0]).astype(BF16), state.astype(BF16), preferred_element_type=F32)
        state = state * cdec + lax.dot_general((k * kb_ref[0]).astype(BF16), vb, TN_DIMS,
                                               preferred_element_type=F32)
    state_ref[...] = state


def _ret_fwd_kernel(first_ref, posblk_ref, cdec_ref, q_ref, k_ref, v_ref, g_ref, cos_ref, sin_ref, dmat_ref, qf_ref,
                    kf_ref, ob_ref, gnw_ref, o_ref, state_ref, *, chunks):
    h = pl.program_id(0)
    n = pl.program_id(1)
    C = RET_CHUNK

    @pl.when(first_ref[n] == 1)
    def _():
        state_ref[...] = jnp.zeros_like(state_ref)

    state = state_ref[...]
    cdec = cdec_ref[0, h]
    for c in range(chunks):
        sl = slice(c * C, (c + 1) * C)
        cos2, sin2 = cos_ref[sl, :], sin_ref[sl, :]
        q = _rope(q_ref[sl, :], cos2, sin2)
        k = _rope(k_ref[sl, :], cos2, sin2) * (RET_DK ** -0.5)
        vb = v_ref[sl, :].astype(BF16)
        intra = lax.dot_general(q.astype(BF16), k.astype(BF16), NT_DIMS, preferred_element_type=F32) * dmat_ref[0]
        out = (jnp.dot(intra.astype(BF16), vb, preferred_element_type=F32)
               + jnp.dot((q * qf_ref[0]).astype(BF16), state.astype(BF16), preferred_element_type=F32)
               + ob_ref[sl, :])
        state = state * cdec + lax.dot_general((k * kf_ref[0]).astype(BF16), vb, TN_DIMS,
                                               preferred_element_type=F32)
        mu = jnp.mean(out, axis=1, keepdims=True)
        cen = out - mu
        var = jnp.mean(cen * cen, axis=1, keepdims=True)
        y = cen * lax.rsqrt(var + GN_EPS) * gnw_ref[...]
        o_ref[sl, :] = (_silu(g_ref[sl, :]) * y).astype(o_ref.dtype)
    state_ref[...] = state


def _retention(proj, first, last, posblk, chunks, ret_decay, ret_gn_w, s_max, col_q, col_k, col_v, col_g):
    T = proj.shape[0]
    C = RET_CHUNK
    R = chunks * C
    nc = T // R
    H = RET_HEADS
    qo, ko, vo, go = col_q // RET_DK, col_k // RET_DK, col_v // RET_DV, col_g // RET_DV
    assert col_q % RET_DK == 0 and col_k % RET_DK == 0 and col_v % RET_DV == 0 and col_g % RET_DV == 0

    half = RET_DK // 2
    inv = ROPE_BASE ** (-jnp.arange(half, dtype=F32) * 2.0 / RET_DK)
    ang = jnp.arange(s_max, dtype=F32)[:, None] * inv[None, :]
    cos2 = jnp.concatenate([jnp.cos(ang), jnp.cos(ang)], axis=1)
    sin2 = jnp.concatenate([-jnp.sin(ang), jnp.sin(ang)], axis=1)

    lg = -jnp.exp(ret_decay.astype(F32))
    pos = jnp.arange(C, dtype=F32)
    diff = pos[:, None] - pos[None, :]
    dec_f = jnp.where((diff >= 0)[None], jnp.exp(jnp.maximum(diff, 0.0)[None] * lg[0][:, None, None]), 0.0)
    dec_b = jnp.where((diff < 0)[None], jnp.exp(jnp.maximum(-diff, 0.0)[None] * lg[1][:, None, None]), 0.0)
    dmat = dec_f + dec_b

    def rows(tab):
        return jnp.broadcast_to(tab[:, :, None], (H, C, RET_DK))

    qf = rows(jnp.exp((pos[None, :] + 1.0) * lg[0][:, None]))
    kf = rows(jnp.exp((C - 1.0 - pos)[None, :] * lg[0][:, None]))
    qb = rows(jnp.exp((C - pos)[None, :] * lg[1][:, None]))
    kb = rows(jnp.exp(pos[None, :] * lg[1][:, None]))
    cdec = jnp.exp(C * lg)

    smem = pl.BlockSpec(memory_space=pltpu.SMEM)
    rev = lambda c: nc - 1 - c
    bwd_specs = [
        smem,
        pl.BlockSpec((R, RET_DK), lambda h, c, last, posb: (rev(c), qo + h)),
        pl.BlockSpec((R, RET_DK), lambda h, c, last, posb: (rev(c), ko + h)),
        pl.BlockSpec((R, RET_DV), lambda h, c, last, posb: (rev(c), vo + h)),
        pl.BlockSpec((R, RET_DK), lambda h, c, last, posb: (posb[rev(c)], 0)),
        pl.BlockSpec((R, RET_DK), lambda h, c, last, posb: (posb[rev(c)], 0)),
        pl.BlockSpec((1, C, RET_DK), lambda h, c, last, posb: (h, 0, 0)),
        pl.BlockSpec((1, C, RET_DK), lambda h, c, last, posb: (h, 0, 0)),
    ]
    out_b = pl.pallas_call(
        functools.partial(_ret_bwd_kernel, chunks=chunks),
        grid_spec=pltpu.PrefetchScalarGridSpec(
            num_scalar_prefetch=2, grid=(H, nc), in_specs=bwd_specs,
            out_specs=pl.BlockSpec((R, RET_DV), lambda h, c, last, posb: (rev(c), h)),
            scratch_shapes=[pltpu.VMEM((RET_DK, RET_DV), F32)]),
        out_shape=jax.ShapeDtypeStruct((T, H * RET_DV), F32),
        compiler_params=_params(("arbitrary", "arbitrary")),
        name="retention_backward",
    )(last, posblk, cdec, proj, proj, proj, cos2, sin2, qb, kb)

    fwd_specs = [
        smem,
        pl.BlockSpec((R, RET_DK), lambda h, n, first, posb: (n, qo + h)),
        pl.BlockSpec((R, RET_DK), lambda h, n, first, posb: (n, ko + h)),
        pl.BlockSpec((R, RET_DV), lambda h, n, first, posb: (n, vo + h)),
        pl.BlockSpec((R, RET_DV), lambda h, n, first, posb: (n, go + h)),
        pl.BlockSpec((R, RET_DK), lambda h, n, first, posb: (posb[n], 0)),
        pl.BlockSpec((R, RET_DK), lambda h, n, first, posb: (posb[n], 0)),
        pl.BlockSpec((1, C, C), lambda h, n, first, posb: (h, 0, 0)),
        pl.BlockSpec((1, C, RET_DK), lambda h, n, first, posb: (h, 0, 0)),
        pl.BlockSpec((1, C, RET_DK), lambda h, n, first, posb: (h, 0, 0)),
        pl.BlockSpec((R, RET_DV), lambda h, n, first, posb: (n, h)),
        pl.BlockSpec((1, RET_DV), lambda h, n, first, posb: (0, h)),
    ]
    return pl.pallas_call(
        functools.partial(_ret_fwd_kernel, chunks=chunks),
        grid_spec=pltpu.PrefetchScalarGridSpec(
            num_scalar_prefetch=2, grid=(H, nc), in_specs=fwd_specs,
            out_specs=pl.BlockSpec((R, RET_DV), lambda h, n, first, posb: (n, h)),
            scratch_shapes=[pltpu.VMEM((RET_DK, RET_DV), F32)]),
        out_shape=jax.ShapeDtypeStruct((T, H * RET_DV), BF16),
        compiler_params=_params(("arbitrary", "arbitrary")),
        name="retention_forward",
    )(first, posblk, cdec, proj, proj, proj, proj, cos2, sin2, dmat, qf, kf, out_b,
      ret_gn_w.astype(F32).reshape(1, -1))


def _merge_kernel(att_ref, wa_ref, ret_ref, wr_ref, ga_ref, gr_ref, o_ref):
    a = jnp.dot(att_ref[...], wa_ref[...], preferred_element_type=F32)
    r = jnp.dot(ret_ref[...], wr_ref[...], preferred_element_type=F32)
    o_ref[...] = (jax.nn.sigmoid(ga_ref[...]) * a + jax.nn.sigmoid(gr_ref[...]) * r).astype(o_ref.dtype)


def _merge(att, ret, proj, w_att_out, w_ret_out, col_ga, col_gr, tm=1024, tn=512):
    T = att.shape[0]
    D = w_att_out.shape[1]
    tm, tn = _tile(T, tm), _tile(D, tn)
    assert col_ga % tn == 0 and col_gr % tn == 0
    ao, ro = col_ga // tn, col_gr // tn
    return pl.pallas_call(
        _merge_kernel,
        grid=(T // tm, D // tn),
        in_specs=[pl.BlockSpec((tm, att.shape[1]), lambda i, j: (i, 0)),
                  pl.BlockSpec((w_att_out.shape[0], tn), lambda i, j: (0, j)),
                  pl.BlockSpec((tm, ret.shape[1]), lambda i, j: (i, 0)),
                  pl.BlockSpec((w_ret_out.shape[0], tn), lambda i, j: (0, j)),
                  pl.BlockSpec((tm, tn), lambda i, j: (i, ao + j)),
                  pl.BlockSpec((tm, tn), lambda i, j: (i, ro + j))],
        out_specs=pl.BlockSpec((tm, tn), lambda i, j: (i, j)),
        out_shape=jax.ShapeDtypeStruct((T, D), BF16),
        compiler_params=_params(("arbitrary", "arbitrary")),
        name="branch_merge",
    )(att, w_att_out, ret, w_ret_out, proj, proj)


def _layernorm_rows(z, w, b):
    mu = jnp.mean(z, axis=1, keepdims=True)
    cen = z - mu
    var = jnp.mean(cen * cen, axis=1, keepdims=True)
    return cen * lax.rsqrt(var + LN_EPS) * w + b


def _router_kernel(z_ref, lw_ref, lb_ref, wh_ref, wl_ref, bias_ref, idx_ref, rank_ref, gate_ref, cnt_ref, x1_ref,
                   slab_ref, xb_ref, carry_ref, *, rows, pitch):
    i = pl.program_id(0)
    tm = z_ref.shape[0]
    E = N_EXPERTS
    gsz = E // N_GROUPS
    neg_inf = jnp.float32(-jnp.inf)

    @pl.when(i == 0)
    def _():
        carry_ref[...] = jnp.zeros_like(carry_ref)

    x1_ref[...] = _layernorm_rows(z_ref[...], lw_ref[...], lb_ref[...])
    _store_slabs(slab_ref, lambda c: x1_ref[:, c * LANES:(c + 1) * LANES], tm, rows, pitch)
    x = x1_ref[...]
    xh = x.astype(BF16)
    xb_ref[...] = xh
    xl = (x - xh.astype(F32)).astype(BF16)
    logits = (lax.dot_general(wh_ref[...], xh, NT_DIMS, preferred_element_type=F32)
              + lax.dot_general(wh_ref[...], xl, NT_DIMS, preferred_element_type=F32)
              + lax.dot_general(wl_ref[...], xh, NT_DIMS, preferred_element_type=F32))
    scores = jax.nn.sigmoid(logits)
    biased = scores + bias_ref[...]

    row = lax.broadcasted_iota(I32, (gsz, tm), 0)
    gscore = []
    for g in range(N_GROUPS):
        blk = biased[g * gsz:(g + 1) * gsz, :]
        m1 = jnp.max(blk, axis=0, keepdims=True)
        first = jnp.min(jnp.where(blk == m1, row, gsz), axis=0, keepdims=True)
        m2 = jnp.max(jnp.where(row == first, neg_inf, blk), axis=0, keepdims=True)
        gscore.append(m1 + m2)
    gid = lax.broadcasted_iota(I32, (E, tm), 0) // gsz
    keep = jnp.zeros((E, tm), I32)
    for g in range(N_GROUPS):
        beaten = jnp.zeros((1, tm), I32)
        for o in range(N_GROUPS):
            if o == g:
                continue
            wins = (gscore[o] >= gscore[g]) if o < g else (gscore[o] > gscore[g])
            beaten = beaten + wins.astype(I32)
        keep = jnp.where(gid == g, (beaten < TOPK_GROUPS).astype(I32), keep)
    masked = jnp.where(keep > 0, biased, neg_inf)

    eidx = lax.broadcasted_iota(I32, (E, tm), 0)
    sel = jnp.zeros((E, tm), jnp.bool_)
    picks, weights = [], []
    for _ in range(TOP_K):
        m = jnp.max(masked, axis=0, keepdims=True)
        pick = jnp.min(jnp.where(masked == m, eidx, E), axis=0, keepdims=True)
        onehot = eidx == pick
        weights.append(jnp.sum(jnp.where(onehot, scores, 0.0), axis=0, keepdims=True))
        picks.append(pick)
        masked = jnp.where(onehot, neg_inf, masked)
        sel = sel | onehot
    wsum = weights[0]
    for w in weights[1:]:
        wsum = wsum + w

    self32 = sel.astype(F32)
    tri = (lax.broadcasted_iota(I32, (tm, tm), 0) < lax.broadcasted_iota(I32, (tm, tm), 1)).astype(F32).astype(BF16)
    prefix = jnp.dot(self32.astype(BF16), tri, preferred_element_type=F32) + carry_ref[...]
    for kk in range(TOP_K):
        onehot = eidx == picks[kk]
        idx_ref[kk:kk + 1, :] = picks[kk]
        rank_ref[kk:kk + 1, :] = jnp.sum(jnp.where(onehot, prefix, 0.0), axis=0, keepdims=True).astype(I32)
        gate_ref[kk:kk + 1, :] = weights[kk] / wsum * ROUTED_SCALE
    carry_ref[...] = carry_ref[...] + jnp.sum(self32, axis=1, keepdims=True)
    cnt_ref[...] = carry_ref[...]


def _ln_router(z, ln_w, ln_b, w_router, router_bias, rows, pitch, tm=256):
    T, D = z.shape
    E = N_EXPERTS
    tm = _tile(T, tm)
    wt = w_router.astype(F32).T
    wh = wt.astype(BF16)
    wl = (wt - wh.astype(F32)).astype(BF16)
    return pl.pallas_call(
        functools.partial(_router_kernel, rows=rows, pitch=pitch),
        grid=(T // tm,),
        in_specs=[pl.BlockSpec((tm, D), lambda i: (i, 0)),
                  pl.BlockSpec((1, D), lambda i: (0, 0)),
                  pl.BlockSpec((1, D), lambda i: (0, 0)),
                  pl.BlockSpec((E, D), lambda i: (0, 0)),
                  pl.BlockSpec((E, D), lambda i: (0, 0)),
                  pl.BlockSpec((E, 1), lambda i: (0, 0))],
        out_specs=[pl.BlockSpec((TOP_K, tm), lambda i: (0, i)),
                   pl.BlockSpec((TOP_K, tm), lambda i: (0, i)),
                   pl.BlockSpec((TOP_K, tm), lambda i: (0, i)),
                   pl.BlockSpec((E, 1), lambda i: (0, 0)),
                   pl.BlockSpec((tm, D), lambda i: (i, 0)),
                   pl.BlockSpec((tm * pitch, LANES), lambda i: (i, 0)),
                   pl.BlockSpec((tm, D), lambda i: (i, 0))],
        out_shape=[jax.ShapeDtypeStruct((TOP_K, T), I32),
                   jax.ShapeDtypeStruct((TOP_K, T), I32),
                   jax.ShapeDtypeStruct((TOP_K, T), F32),
                   jax.ShapeDtypeStruct((E, 1), F32),
                   jax.ShapeDtypeStruct((T, D), F32),
                   jax.ShapeDtypeStruct((T * pitch, LANES), I32),
                   jax.ShapeDtypeStruct((T, D), BF16)],
        scratch_shapes=[pltpu.VMEM((E, 1), F32)],
        compiler_params=_params(("arbitrary",)),
        name="layernorm_router",
    )(z, ln_w.astype(F32).reshape(1, -1), ln_b.astype(F32).reshape(1, -1), wh, wl,
      router_bias.astype(F32).reshape(E, 1))


def _slot_kernel(start_ref, idx_ref, rank_ref, slot_ref):
    idx = idx_ref[...]
    acc = rank_ref[...]
    for e in range(N_EXPERTS):
        acc = acc + jnp.where(idx == e, start_ref[e], 0)
    slot_ref[...] = acc


def _slots(pad_start, idx, rank, tm=2048):
    T = idx.shape[1]
    tm = _tile(T, tm)
    return pl.pallas_call(
        _slot_kernel,
        grid=(T // tm,),
        in_specs=[pl.BlockSpec(memory_space=pltpu.SMEM),
                  pl.BlockSpec((TOP_K, tm), lambda i: (0, i)),
                  pl.BlockSpec((TOP_K, tm), lambda i: (0, i))],
        out_specs=pl.BlockSpec((TOP_K, tm), lambda i: (0, i)),
        out_shape=jax.ShapeDtypeStruct((TOP_K, T), I32),
        compiler_params=_params(("arbitrary",)),
        name="slot_index",
    )(pad_start, idx, rank)


def _zero_fill_plan(npad, start, te, fn):
    pos = start
    bit = te // 2
    while bit >= 1:
        take = (npad & bit) != 0

        @pl.when(take)
        def _(pos=pos, bit=bit):
            fn(pos, bit)

        pos = pos + jnp.where(take, bit, 0)
        bit //= 2


def _dispatch_kernel(fill_start_ref, fill_len_ref, slot_ref, x_ref, xs_ref, zero_ref, sem, zsem, *, te, rows, pitch):
    i = pl.program_id(0)
    tm = x_ref.shape[0] // pitch

    def start_token(t, carry):
        src = x_ref.at[pl.ds(pl.multiple_of(t * pitch, SUBLANES), rows)]
        for k in range(TOP_K):
            dst = xs_ref.at[pl.ds(pl.multiple_of(slot_ref[t * TOP_K + k] * pitch, SUBLANES), rows)]
            pltpu.make_async_copy(src, dst, sem).start(priority=k % 2)
        return carry

    lax.fori_loop(0, tm, start_token, 0)

    @pl.when(i == 0)
    def _():
        zero_ref[...] = jnp.zeros_like(zero_ref)

        def zero_copy(pos, n):
            return pltpu.make_async_copy(zero_ref.at[pl.ds(0, n * pitch)],
                                         xs_ref.at[pl.ds(pl.multiple_of(pos * pitch, SUBLANES), n * pitch)], zsem)

        def start_fill(e, carry):
            _zero_fill_plan(fill_len_ref[e], fill_start_ref[e], te, lambda pos, n: zero_copy(pos, n).start())
            return carry

        def wait_fill(e, carry):
            _zero_fill_plan(fill_len_ref[e], fill_start_ref[e], te, lambda pos, n: zero_copy(pos, n).wait())
            return carry

        lax.fori_loop(0, N_EXPERTS, start_fill, 0)
        lax.fori_loop(0, N_EXPERTS, wait_fill, 0)

    def wait_token(t, carry):
        for k in range(TOP_K):
            pltpu.make_async_copy(x_ref.at[pl.ds(0, rows)], xs_ref.at[pl.ds(0, rows)], sem).wait()
        return carry

    lax.fori_loop(0, tm, wait_token, 0)


def _dispatch(x_slab, slot_flat, fill_start, fill_len, n_slots, te, rows, pitch, tm=256):
    T = x_slab.shape[0] // pitch
    tm = _tile(T, tm)
    return pl.pallas_call(
        functools.partial(_dispatch_kernel, te=te, rows=rows, pitch=pitch),
        grid_spec=pltpu.PrefetchScalarGridSpec(
            num_scalar_prefetch=2,
            grid=(T // tm,),
            in_specs=[pl.BlockSpec((tm * TOP_K,), lambda i, fs, fl: (i,), memory_space=pltpu.SMEM),
                      pl.BlockSpec((tm * pitch, LANES), lambda i, fs, fl: (i, 0))],
            out_specs=pl.BlockSpec(memory_space=pl.ANY),
            scratch_shapes=[pltpu.VMEM((te // 2 * pitch, LANES), I32),
                            pltpu.SemaphoreType.DMA(()),
                            pltpu.SemaphoreType.DMA(())]),
        out_shape=jax.ShapeDtypeStruct((n_slots * pitch, LANES), I32),
        compiler_params=_params(("arbitrary",)),
        name="moe_dispatch",
    )(fill_start, fill_len, slot_flat, x_slab)


def _ffn_partial(xb, wg, wu, wd):
    hid = _silu(jnp.dot(xb, wg, preferred_element_type=F32)) * jnp.dot(xb, wu, preferred_element_type=F32)
    return jnp.dot(hid.astype(BF16), wd, preferred_element_type=F32)


def _expert_kernel(blk_e_ref, nvalid_ref, x_ref, wg_ref, wu_ref, wd_ref, o_ref, acc_ref, *, rows, pitch):
    b = pl.program_id(0)
    j = pl.program_id(1)
    te = acc_ref.shape[0]

    @pl.when(b < nvalid_ref[0])
    def _():
        @pl.when(j == 0)
        def _():
            acc_ref[...] = jnp.zeros_like(acc_ref)

        pairs = [_unpack_pair(x_ref[pl.ds(r, te, stride=pitch), :]) for r in range(rows)]
        xb = jnp.concatenate([lo.astype(BF16) for lo, _ in pairs] + [hi.astype(BF16) for _, hi in pairs], axis=1)
        acc_ref[...] += _ffn_partial(xb, wg_ref[...], wu_ref[...], wd_ref[...])

        @pl.when(j == pl.num_programs(1) - 1)
        def _():
            _store_slabs(o_ref, lambda c: acc_ref[:, c * LANES:(c + 1) * LANES], te, rows, pitch)


def _routed_ffn(xs, blk_e, nvalid, wg, wu, wd, te, rows, pitch, tj=256):
    P = xs.shape[0] // pitch
    _, D, DE = wg.shape
    tj = _tile(DE, tj)
    nj = DE // tj
    nblk = P // te

    def bclamp(b, nv):
        return jnp.minimum(b, nv[0] - 1)

    def jclamp(b, j, nv):
        return jnp.where(b < nv[0], j, nj - 1)

    return pl.pallas_call(
        functools.partial(_expert_kernel, rows=rows, pitch=pitch),
        grid_spec=pltpu.PrefetchScalarGridSpec(
            num_scalar_prefetch=2,
            grid=(nblk, nj),
            in_specs=[pl.BlockSpec((te * pitch, LANES), lambda b, j, be, nv: (bclamp(b, nv), 0)),
                      pl.BlockSpec((None, D, tj), lambda b, j, be, nv: (be[bclamp(b, nv)], 0, jclamp(b, j, nv))),
                      pl.BlockSpec((None, D, tj), lambda b, j, be, nv: (be[bclamp(b, nv)], 0, jclamp(b, j, nv))),
                      pl.BlockSpec((None, tj, D), lambda b, j, be, nv: (be[bclamp(b, nv)], jclamp(b, j, nv), 0))],
            out_specs=pl.BlockSpec((te * pitch, LANES), lambda b, j, be, nv: (bclamp(b, nv), 0)),
            scratch_shapes=[pltpu.VMEM((te, D), F32)]),
        out_shape=jax.ShapeDtypeStruct((P * pitch, LANES), I32),
        compiler_params=_params(("arbitrary", "arbitrary")),
        name="routed_experts",
    )(blk_e, nvalid, xs, wg, wu, wd)


def _shared_kernel(x_ref, wg_ref, wu_ref, wd_ref, o_ref):
    @pl.when(pl.program_id(1) == 0)
    def _():
        o_ref[...] = jnp.zeros_like(o_ref)

    o_ref[...] += _ffn_partial(x_ref[...], wg_ref[...], wu_ref[...], wd_ref[...])


def _shared_ffn(xb, wg, wu, wd, tm=512, tj=256):
    T, D = xb.shape
    DS = wg.shape[1]
    tm, tj = _tile(T, tm), _tile(DS, tj)
    return pl.pallas_call(
        _shared_kernel,
        grid=(T // tm, DS // tj),
        in_specs=[pl.BlockSpec((tm, D), lambda i, j: (i, 0)),
                  pl.BlockSpec((D, tj), lambda i, j: (0, j)),
                  pl.BlockSpec((D, tj), lambda i, j: (0, j)),
                  pl.BlockSpec((tj, D), lambda i, j: (j, 0))],
        out_specs=pl.BlockSpec((tm, D), lambda i, j: (i, 0)),
        out_shape=jax.ShapeDtypeStruct((T, D), F32),
        compiler_params=_params(("arbitrary", "arbitrary")),
        name="shared_expert",
    )(xb, wg, wu, wd)


def _combine_kernel(slot_ref, slot_next_ref, gate_ref, x_ref, sh_ref, lw_ref, lb_ref, ys_ref, o1_ref, o2_ref,
                    buf_ref, z_ref, sem, *, rows, pitch, steps1):
    i = pl.program_id(0)
    nsteps = pl.num_programs(0)
    tm, d_model = x_ref.shape
    half = d_model // 2
    cur = i % 2

    def start_all(srefs, buf):
        def body(t, carry):
            for kk in range(TOP_K):
                src = ys_ref.at[pl.ds(pl.multiple_of(srefs[t * TOP_K + kk] * pitch, SUBLANES), rows)]
                dst = buf_ref.at[buf, kk, pl.ds(pl.multiple_of(t * pitch, SUBLANES), rows)]
                pltpu.make_async_copy(src, dst, sem.at[buf]).start(priority=kk % 2)
            return carry
        lax.fori_loop(0, tm, body, 0)

    @pl.when(i == 0)
    def _():
        start_all(slot_ref, 0)

    @pl.when(i + 1 < nsteps)
    def _():
        start_all(slot_next_ref, 1 - cur)

    def wait_body(t, carry):
        for kk in range(TOP_K):
            pltpu.make_async_copy(ys_ref.at[pl.ds(0, rows)], buf_ref.at[cur, 0, pl.ds(0, rows)], sem.at[cur]).wait()
        return carry

    lax.fori_loop(0, tm, wait_body, 0)

    gates = gate_ref[...]
    gk = [gates[:, kk:kk + 1] for kk in range(TOP_K)]
    for r in range(rows):
        lo_acc = hi_acc = None
        for kk in range(TOP_K):
            lo, hi = _unpack_pair(buf_ref[cur, kk, pl.ds(r, tm, stride=pitch), :])
            lo_acc = gk[kk] * lo if lo_acc is None else lo_acc + gk[kk] * lo
            hi_acc = gk[kk] * hi if hi_acc is None else hi_acc + gk[kk] * hi
        for c0, acc in ((r * LANES, lo_acc), (half + r * LANES, hi_acc)):
            cols = slice(c0, c0 + LANES)
            z_ref[:, cols] = ALPHA * x_ref[:, cols] + (acc + sh_ref[:, cols])
    y = _layernorm_rows(z_ref[...], lw_ref[...], lb_ref[...])

    @pl.when(i < steps1)
    def _():
        o1_ref[...] = y

    @pl.when(i >= steps1)
    def _():
        o2_ref[...] = y


def _combine(ys, slot_flat, gate_tk, x1, shared, ln_w, ln_b, n1, rows, pitch, tm=128):
    T, D = x1.shape
    tm = _tile(math.gcd(n1, T - n1), tm)
    nsteps = T // tm
    steps1 = n1 // tm
    return pl.pallas_call(
        functools.partial(_combine_kernel, rows=rows, pitch=pitch, steps1=steps1),
        grid=(nsteps,),
        in_specs=[pl.BlockSpec((tm * TOP_K,), lambda i: (i,), memory_space=pltpu.SMEM),
                  pl.BlockSpec((tm * TOP_K,), lambda i: (jnp.minimum(i + 1, nsteps - 1),), memory_space=pltpu.SMEM),
                  pl.BlockSpec((tm, TOP_K), lambda i: (i, 0)),
                  pl.BlockSpec((tm, D), lambda i: (i, 0)),
                  pl.BlockSpec((tm, D), lambda i: (i, 0)),
                  pl.BlockSpec((1, D), lambda i: (0, 0)),
                  pl.BlockSpec((1, D), lambda i: (0, 0)),
                  pl.BlockSpec(memory_space=pl.ANY)],
        out_specs=[pl.BlockSpec((tm, D), lambda i: (jnp.minimum(i, steps1 - 1), 0)),
                   pl.BlockSpec((tm, D), lambda i: (jnp.maximum(i - steps1, 0), 0))],
        out_shape=[jax.ShapeDtypeStruct((n1, D), F32), jax.ShapeDtypeStruct((T - n1, D), F32)],
        scratch_shapes=[pltpu.VMEM((2, TOP_K, tm * pitch, LANES), I32), pltpu.VMEM((tm, D), F32),
                        pltpu.SemaphoreType.DMA((2,))],
        compiler_params=_params(("arbitrary",)),
        name="moe_combine_layernorm",
    )(slot_flat, slot_flat, gate_tk, x1, shared, ln_w.astype(F32).reshape(1, -1), ln_b.astype(F32).reshape(1, -1),
      ys)


def _seq_tables(groups, blk):
    first, last, pos = [], [], []
    for nseq, slen in groups:
        nb = slen // blk
        for _ in range(nseq):
            for b in range(nb):
                first.append(int(b == 0))
                last.append(int(b == nb - 1))
                pos.append(b)
    return (jnp.asarray(np.array(first, np.int32)), jnp.asarray(np.array(last, np.int32)),
            jnp.asarray(np.array(pos, np.int32)))


def _expert_block_rows(d_model):
    return 512 if d_model >= 4096 else 128


def kernel(x_prompt, x_sample, rel_table, w_in, attn_sink, ret_decay, ret_gn_w, w_att_out, w_ret_out, w_o, ln1_w,
           ln1_b, w_router, router_bias, w_exp_gate, w_exp_up, w_exp_down, w_sh_gate, w_sh_up, w_sh_down, ln2_w,
           ln2_b):
    assert DEPTH == 1
    B1, S1, D = x_prompt.shape
    B2, S2, _ = x_sample.shape
    assert S1 % BLOCK == 0 and S2 % BLOCK == 0 and BLOCK == RET_CHUNK
    xa, xb = x_prompt.reshape(B1 * S1, D), x_sample.reshape(B2 * S2, D)
    n1 = B1 * S1
    T = n1 + B2 * S2
    groups = ((B1, S1), (B2, S2))
    first, last, _ = _seq_tables(groups, BLOCK)
    ret_chunks = math.gcd(4, math.gcd(S1 // RET_CHUNK, S2 // RET_CHUNK))
    rfirst, rlast, rpos = _seq_tables(groups, ret_chunks * RET_CHUNK)

    att_q, att_kv = ATT_HEADS * HEAD_DIM, ATT_KV_HEADS * HEAD_DIM
    ret_qk, ret_v = RET_HEADS * RET_DK, RET_HEADS * RET_DV
    cols = np.concatenate([[0], np.cumsum([att_q, att_kv, att_kv, ret_qk, ret_qk, ret_v, ret_v, D, D])])
    c_aq, c_ak, c_av, c_rq, c_rk, c_rv, c_rg, c_ga, c_gr = (int(c) for c in cols[:-1])
    assert int(cols[-1]) == w_in.shape[2]
    rows = _slab_rows(D)
    pitch = _slab_pitch(rows)

    x_bf = jnp.concatenate([xa, xb], axis=0).astype(BF16)
    proj = _matmul(x_bf, w_in[0].astype(BF16), F32)
    att = _attention(proj, first, last, attn_sink[0], rel_table, c_aq, c_ak, c_av)
    ret = _retention(proj, rfirst, rlast, rpos, ret_chunks, ret_decay[0], ret_gn_w[0], max(S1, S2),
                     c_rq, c_rk, c_rv, c_rg)
    merged = _merge(att, ret, proj, w_att_out[0].astype(BF16), w_ret_out[0].astype(BF16), c_ga, c_gr)
    z1 = _residual_matmul(merged, w_o[0].astype(BF16), xa, xb)

    idx, rank, gate, counts, x1, x1_slab, x1b = _ln_router(z1, ln1_w[0], ln1_b[0], w_router[0], router_bias[0],
                                                           rows, pitch)
    te = _expert_block_rows(D)
    cnt = counts[:, 0].astype(I32)
    padded = (cnt + te - 1) // te * te
    pad_end = jnp.cumsum(padded)
    pad_start = pad_end - padded
    nblk = (T * TOP_K) // te + N_EXPERTS
    blk_e = jnp.minimum(jnp.searchsorted(pad_end, jnp.arange(nblk, dtype=I32) * te, side='right'),
                        N_EXPERTS - 1).astype(I32)
    nvalid = (pad_end[-1:] // te).astype(I32)
    slot = _slots(pad_start.astype(I32), idx, rank)
    slot_flat = slot.T.reshape(-1)

    xs = _dispatch(x1_slab, slot_flat, (pad_start + cnt).astype(I32), (padded - cnt).astype(I32), nblk * te, te,
                   rows, pitch)
    ys = _routed_ffn(xs, blk_e, nvalid, w_exp_gate[0].astype(BF16), w_exp_up[0].astype(BF16),
                     w_exp_down[0].astype(BF16), te, rows, pitch)
    shared = _shared_ffn(x1b, w_sh_gate[0].astype(BF16), w_sh_up[0].astype(BF16), w_sh_down[0].astype(BF16))
    y1, y2 = _combine(ys, slot_flat, gate.T, x1, shared, ln2_w[0], ln2_b[0], n1, rows, pitch)
    return (y1.reshape(B1, S1, D), y2.reshape(B2, S2, D))
```

```python
import functools
import math

import numpy as np
import jax
import jax.numpy as jnp
from jax import lax
from jax.experimental import pallas as pl
from jax.experimental.pallas import tpu as pltpu

F32 = jnp.float32
BF16 = jnp.bfloat16
I32 = jnp.int32

HEAD_DIM = 128
ATT_HEADS = 16
ATT_KV_HEADS = 4
WINDOW = 128
BLOCK = 128
REL_BUCKETS = 32
REL_MAX_DIST = 128
RET_HEADS = 8
RET_DK = 128
RET_DV = 256
RET_CHUNK = 128
ROPE_BASE = 10000.0
N_EXPERTS = 128
TOP_K = 8
N_GROUPS = 8
TOPK_GROUPS = 4
ROUTED_SCALE = 2.5
LN_EPS = 1e-5
GN_EPS = 1e-5
DEPTH = 1
ALPHA = (2.0 * DEPTH) ** 0.25

V7X_VMEM_BUDGET_BYTES = 56 * 1024 * 1024
LANES = 128
SUBLANES = 8

NT_DIMS = (((1,), (1,)), ((), ()))
TN_DIMS = (((0,), (0,)), ((), ()))


def _tile(dim, pref):
    t = min(dim, pref)
    assert dim % t == 0, (dim, pref)
    return t


def _params(sem, vmem=V7X_VMEM_BUDGET_BYTES):
    return pltpu.CompilerParams(dimension_semantics=sem, vmem_limit_bytes=vmem)


def _silu(v):
    return v * jax.nn.sigmoid(v)


def _slab_rows(d_model):
    rows = d_model // (2 * LANES)
    assert rows % SUBLANES == 0
    return rows


def _slab_pitch(rows):
    return rows if (rows // SUBLANES) % 2 == 1 else rows + SUBLANES


def _pack_pair(lo, hi):
    lo_bits = lax.bitcast_convert_type(lo.astype(BF16).astype(F32), I32)
    hi_bits = lax.bitcast_convert_type(hi.astype(BF16).astype(F32), I32)
    return hi_bits | lax.shift_right_logical(lo_bits, 16)


def _unpack_pair(w):
    lo = lax.bitcast_convert_type(lax.shift_left(w, 16), F32)
    hi = lax.bitcast_convert_type(w & jnp.int32(-65536), F32)
    return lo, hi


def _store_slabs(slab_ref, col, n_tok, rows, pitch):
    for r in range(rows):
        slab_ref[pl.ds(r, n_tok, stride=pitch), :] = _pack_pair(col(r), col(rows + r))
    for r in range(rows, pitch):
        slab_ref[pl.ds(r, n_tok, stride=pitch), :] = jnp.zeros((n_tok, LANES), I32)


def _matmul_kernel(x_ref, w_ref, o_ref):
    o_ref[...] = jnp.dot(x_ref[...], w_ref[...], preferred_element_type=F32).astype(o_ref.dtype)


def _matmul(x, w, out_dtype, tm=1024, tn=1024):
    M, K = x.shape
    N = w.shape[1]
    tm, tn = _tile(M, tm), _tile(N, tn)
    return pl.pallas_call(
        _matmul_kernel,
        grid=(M // tm, N // tn),
        in_specs=[pl.BlockSpec((tm, K), lambda i, j: (i, 0)),
                  pl.BlockSpec((K, tn), lambda i, j: (0, j))],
        out_specs=pl.BlockSpec((tm, tn), lambda i, j: (i, j)),
        out_shape=jax.ShapeDtypeStruct((M, N), out_dtype),
        compiler_params=_params(("arbitrary", "arbitrary")),
        name="proj_matmul",
    )(x, w)


def _residual_matmul_kernel(m_ref, w_ref, xa_ref, xb_ref, o_ref, *, steps1):
    mm = jnp.dot(m_ref[...], w_ref[...], preferred_element_type=F32)

    @pl.when(pl.program_id(0) < steps1)
    def _():
        o_ref[...] = ALPHA * xa_ref[...] + mm

    @pl.when(pl.program_id(0) >= steps1)
    def _():
        o_ref[...] = ALPHA * xb_ref[...] + mm


def _residual_matmul(merged, w_o, xa, xb, tm=512, tn=1024):
    T, K = merged.shape
    D = w_o.shape[1]
    n1 = xa.shape[0]
    tm, tn = _tile(math.gcd(n1, T - n1), tm), _tile(D, tn)
    steps1, nt = n1 // tm, D // tn
    return pl.pallas_call(
        functools.partial(_residual_matmul_kernel, steps1=steps1),
        grid=(T // tm, nt),
        in_specs=[pl.BlockSpec((tm, K), lambda i, n: (i, 0)),
                  pl.BlockSpec((K, tn), lambda i, n: (0, n)),
                  pl.BlockSpec((tm, tn), lambda i, n: (jnp.minimum(i, steps1 - 1), jnp.where(i < steps1, n, nt - 1))),
                  pl.BlockSpec((tm, tn), lambda i, n: (jnp.maximum(i - steps1, 0), jnp.where(i < steps1, 0, n)))],
        out_specs=pl.BlockSpec((tm, tn), lambda i, n: (i, n)),
        out_shape=jax.ShapeDtypeStruct((T, D), F32),
        compiler_params=_params(("arbitrary", "arbitrary")),
        name="out_proj_residual",
    )(merged, w_o, xa, xb)


def _attn_kernel(first_ref, last_ref, sink_ref, q_ref, kp_ref, kc_ref, kn_ref, vp_ref, vc_ref, vn_ref,
                 bias_ref, o_ref, *, kv_heads, group, scale):
    n = pl.program_id(0)
    neg_inf = jnp.float32(-jnp.inf)
    pen_p = jnp.where(first_ref[n] == 1, neg_inf, jnp.float32(0.0))
    pen_n = jnp.where(last_ref[n] == 1, neg_inf, jnp.float32(0.0))
    lane = lax.broadcasted_iota(I32, (1, 3 * BLOCK), 1)
    pen = jnp.where(lane < BLOCK, pen_p, jnp.where(lane >= 2 * BLOCK, pen_n, jnp.float32(0.0)))
    for h in range(kv_heads):
        cols = slice(h * HEAD_DIM, (h + 1) * HEAD_DIM)
        kcat = jnp.concatenate([kp_ref[:, cols], kc_ref[:, cols], kn_ref[:, cols]], axis=0).astype(BF16)
        vcat = jnp.concatenate([vp_ref[:, cols], vc_ref[:, cols], vn_ref[:, cols]], axis=0).astype(BF16)
        heads = [h * group + g for g in range(group)]
        qcols = [slice(hd * HEAD_DIM, (hd + 1) * HEAD_DIM) for hd in heads]
        sinks = [sink_ref[hd] for hd in heads]
        s = [lax.dot_general(q_ref[:, c].astype(BF16), kcat, NT_DIMS, preferred_element_type=F32) * scale
             + bias_ref[hd] + pen for hd, c in zip(heads, qcols)]
        m = [jnp.maximum(jnp.max(si, axis=1, keepdims=True), sk) for si, sk in zip(s, sinks)]
        e = [jnp.exp(si - mi) for si, mi in zip(s, m)]
        denom = [jnp.sum(ei, axis=1, keepdims=True) + jnp.exp(sk - mi) for ei, sk, mi in zip(e, sinks, m)]
        p = [(ei * (1.0 / di)).astype(BF16) for ei, di in zip(e, denom)]
        for c, pi in zip(qcols, p):
            o_ref[:, c] = jnp.dot(pi, vcat, preferred_element_type=F32).astype(o_ref.dtype)


def _t5_bucket(rel):
    nb = REL_BUCKETS // 2
    max_exact = nb // 2
    ret = (rel > 0).astype(I32) * nb
    n = jnp.abs(rel)
    nf = jnp.maximum(n, max_exact).astype(F32)
    large = max_exact + (jnp.log(nf / max_exact) / math.log(REL_MAX_DIST / max_exact) * (nb - max_exact)).astype(I32)
    large = jnp.minimum(large, nb - 1)
    return ret + jnp.where(n < max_exact, n, large)


def _attention(proj, first, last, sink, rel_table, col_q, col_k, col_v):
    T = proj.shape[0]
    nb = T // BLOCK
    group = ATT_HEADS // ATT_KV_HEADS
    qw, kw = ATT_HEADS * HEAD_DIM, ATT_KV_HEADS * HEAD_DIM
    assert col_q % qw == 0 and col_k % kw == 0 and col_v % kw == 0
    qo, ko, vo = col_q // qw, col_k // kw, col_v // kw
    rel = jnp.arange(3 * BLOCK)[None, :] - BLOCK - jnp.arange(BLOCK)[:, None]
    bias = rel_table.astype(F32)[_t5_bucket(rel)]
    bias = jnp.where((jnp.abs(rel) <= WINDOW)[:, :, None], bias, -jnp.inf)
    bias = jnp.transpose(bias, (2, 0, 1))

    def kv_spec(off, shift):
        def imap(n, first, last):
            return (jnp.clip(n + shift, 0, nb - 1), off)
        return pl.BlockSpec((BLOCK, kw), imap)

    grid_spec = pltpu.PrefetchScalarGridSpec(
        num_scalar_prefetch=2,
        grid=(nb,),
        in_specs=[pl.BlockSpec(memory_space=pltpu.SMEM),
                  pl.BlockSpec((BLOCK, qw), lambda n, first, last: (n, qo)),
                  kv_spec(ko, -1), kv_spec(ko, 0), kv_spec(ko, 1),
                  kv_spec(vo, -1), kv_spec(vo, 0), kv_spec(vo, 1),
                  pl.BlockSpec((ATT_HEADS, BLOCK, 3 * BLOCK), lambda n, first, last: (0, 0, 0))],
        out_specs=pl.BlockSpec((BLOCK, qw), lambda n, first, last: (n, 0)),
    )
    return pl.pallas_call(
        functools.partial(_attn_kernel, kv_heads=ATT_KV_HEADS, group=group, scale=HEAD_DIM ** -0.5),
        grid_spec=grid_spec,
        out_shape=jax.ShapeDtypeStruct((T, qw), BF16),
        compiler_params=_params(("arbitrary",)),
        name="window_attention",
    )(first, last, sink.astype(F32), proj, proj, proj, proj, proj, proj, proj, bias)


def _rope(x, cos2, sin2):
    return x * cos2 + pltpu.roll(x, RET_DK // 2, axis=1) * sin2


def _ret_bwd_kernel(last_ref, posblk_ref, cdec_ref, q_ref, k_ref, v_ref, cos_ref, sin_ref, qb_ref, kb_ref, o_ref,
                    state_ref, *, chunks):
    h = pl.program_id(0)
    n = pl.num_programs(1) - 1 - pl.program_id(1)
    C = RET_CHUNK

    @pl.when(last_ref[n] == 1)
    def _():
        state_ref[...] = jnp.zeros_like(state_ref)

    state = state_ref[...]
    cdec = cdec_ref[1, h]
    for c in range(chunks - 1, -1, -1):
        sl = slice(c * C, (c + 1) * C)
        cos2, sin2 = cos_ref[sl, :], sin_ref[sl, :]
        q = _rope(q_ref[sl, :], cos2, sin2)
        k = _rope(k_ref[sl, :], cos2, sin2) * (RET_DK ** -0.5)
        vb = v_ref[sl, :].astype(BF16)
        o_ref[sl, :] = jnp.dot((q * qb_ref[0]).astype(BF16), state.astype(BF16), preferred_element_type=F32)
        state = state * cdec + lax.dot_general((k * kb_ref[0]).astype(BF16), vb, TN_DIMS,
                                               preferred_element_type=F32)
    state_ref[...] = state


def _ret_fwd_kernel(first_ref, posblk_ref, cdec_ref, q_ref, k_ref, v_ref, g_ref, cos_ref, sin_ref, dmat_ref, qf_ref,
                    kf_ref, ob_ref, gnw_ref, o_ref, state_ref, *, chunks):
    h = pl.program_id(0)
    n = pl.program_id(1)
    C = RET_CHUNK

    @pl.when(first_ref[n] == 1)
    def _():
        state_ref[...] = jnp.zeros_like(state_ref)

    state = state_ref[...]
    cdec = cdec_ref[0, h]
    for c in range(chunks):
        sl = slice(c * C, (c + 1) * C)
        cos2, sin2 = cos_ref[sl, :], sin_ref[sl, :]
        q = _rope(q_ref[sl, :], cos2, sin2)
        k = _rope(k_ref[sl, :], cos2, sin2) * (RET_DK ** -0.5)
        vb = v_ref[sl, :].astype(BF16)
        intra = lax.dot_general(q.astype(BF16), k.astype(BF16), NT_DIMS, preferred_element_type=F32) * dmat_ref[0]
        out = (jnp.dot(intra.astype(BF16), vb, preferred_element_type=F32)
               + jnp.dot((q * qf_ref[0]).astype(BF16), state.astype(BF16), preferred_element_type=F32)
               + ob_ref[sl, :])
        state = state * cdec + lax.dot_general((k * kf_ref[0]).astype(BF16), vb, TN_DIMS,
                                               preferred_element_type=F32)
        mu = jnp.mean(out, axis=1, keepdims=True)
        cen = out - mu
        var = jnp.mean(cen * cen, axis=1, keepdims=True)
        y = cen * lax.rsqrt(var + GN_EPS) * gnw_ref[...]
        o_ref[sl, :] = (_silu(g_ref[sl, :]) * y).astype(o_ref.dtype)
    state_ref[...] = state


def _retention(proj, first, last, posblk, chunks, ret_decay, ret_gn_w, s_max, col_q, col_k, col_v, col_g):
    T = proj.shape[0]
    C = RET_CHUNK
    R = chunks * C
    nc = T // R
    H = RET_HEADS
    qo, ko, vo, go = col_q // RET_DK, col_k // RET_DK, col_v // RET_DV, col_g // RET_DV
    assert col_q % RET_DK == 0 and col_k % RET_DK == 0 and col_v % RET_DV == 0 and col_g % RET_DV == 0

    half = RET_DK // 2
    inv = ROPE_BASE ** (-jnp.arange(half, dtype=F32) * 2.0 / RET_DK)
    ang = jnp.arange(s_max, dtype=F32)[:, None] * inv[None, :]
    cos2 = jnp.concatenate([jnp.cos(ang), jnp.cos(ang)], axis=1)
    sin2 = jnp.concatenate([-jnp.sin(ang), jnp.sin(ang)], axis=1)

    lg = -jnp.exp(ret_decay.astype(F32))
    pos = jnp.arange(C, dtype=F32)
    diff = pos[:, None] - pos[None, :]
    dec_f = jnp.where((diff >= 0)[None], jnp.exp(jnp.maximum(diff, 0.0)[None] * lg[0][:, None, None]), 0.0)
    dec_b = jnp.where((diff < 0)[None], jnp.exp(jnp.maximum(-diff, 0.0)[None] * lg[1][:, None, None]), 0.0)
    dmat = dec_f + dec_b

    def rows(tab):
        return jnp.broadcast_to(tab[:, :, None], (H, C, RET_DK))

    qf = rows(jnp.exp((pos[None, :] + 1.0) * lg[0][:, None]))
    kf = rows(jnp.exp((C - 1.0 - pos)[None, :] * lg[0][:, None]))
    qb = rows(jnp.exp((C - pos)[None, :] * lg[1][:, None]))
    kb = rows(jnp.exp(pos[None, :] * lg[1][:, None]))
    cdec = jnp.exp(C * lg)

    smem = pl.BlockSpec(memory_space=pltpu.SMEM)
    rev = lambda c: nc - 1 - c
    bwd_specs = [
        smem,
        pl.BlockSpec((R, RET_DK), lambda h, c, last, posb: (rev(c), qo + h)),
        pl.BlockSpec((R, RET_DK), lambda h, c, last, posb: (rev(c), ko + h)),
        pl.BlockSpec((R, RET_DV), lambda h, c, last, posb: (rev(c), vo + h)),
        pl.BlockSpec((R, RET_DK), lambda h, c, last, posb: (posb[rev(c)], 0)),
        pl.BlockSpec((R, RET_DK), lambda h, c, last, posb: (posb[rev(c)], 0)),
        pl.BlockSpec((1, C, RET_DK), lambda h, c, last, posb: (h, 0, 0)),
        pl.BlockSpec((1, C, RET_DK), lambda h, c, last, posb: (h, 0, 0)),
    ]
    out_b = pl.pallas_call(
        functools.partial(_ret_bwd_kernel, chunks=chunks),
        grid_spec=pltpu.PrefetchScalarGridSpec(
            num_scalar_prefetch=2, grid=(H, nc), in_specs=bwd_specs,
            out_specs=pl.BlockSpec((R, RET_DV), lambda h, c, last, posb: (rev(c), h)),
            scratch_shapes=[pltpu.VMEM((RET_DK, RET_DV), F32)]),
        out_shape=jax.ShapeDtypeStruct((T, H * RET_DV), F32),
        compiler_params=_params(("arbitrary", "arbitrary")),
        name="retention_backward",
    )(last, posblk, cdec, proj, proj, proj, cos2, sin2, qb, kb)

    fwd_specs = [
        smem,
        pl.BlockSpec((R, RET_DK), lambda h, n, first, posb: (n, qo + h)),
        pl.BlockSpec((R, RET_DK), lambda h, n, first, posb: (n, ko + h)),
        pl.BlockSpec((R, RET_DV), lambda h, n, first, posb: (n, vo + h)),
        pl.BlockSpec((R, RET_DV), lambda h, n, first, posb: (n, go + h)),
        pl.BlockSpec((R, RET_DK), lambda h, n, first, posb: (posb[n], 0)),
        pl.BlockSpec((R, RET_DK), lambda h, n, first, posb: (posb[n], 0)),
        pl.BlockSpec((1, C, C), lambda h, n, first, posb: (h, 0, 0)),
        pl.BlockSpec((1, C, RET_DK), lambda h, n, first, posb: (h, 0, 0)),
        pl.BlockSpec((1, C, RET_DK), lambda h, n, first, posb: (h, 0, 0)),
        pl.BlockSpec((R, RET_DV), lambda h, n, first, posb: (n, h)),
        pl.BlockSpec((1, RET_DV), lambda h, n, first, posb: (0, h)),
    ]
    return pl.pallas_call(
        functools.partial(_ret_fwd_kernel, chunks=chunks),
        grid_spec=pltpu.PrefetchScalarGridSpec(
            num_scalar_prefetch=2, grid=(H, nc), in_specs=fwd_specs,
            out_specs=pl.BlockSpec((R, RET_DV), lambda h, n, first, posb: (n, h)),
            scratch_shapes=[pltpu.VMEM((RET_DK, RET_DV), F32)]),
        out_shape=jax.ShapeDtypeStruct((T, H * RET_DV), BF16),
        compiler_params=_params(("arbitrary", "arbitrary")),
        name="retention_forward",
    )(first, posblk, cdec, proj, proj, proj, proj, cos2, sin2, dmat, qf, kf, out_b,
      ret_gn_w.astype(F32).reshape(1, -1))


def _merge_kernel(att_ref, wa_ref, ret_ref, wr_ref, ga_ref, gr_ref, o_ref):
    a = jnp.dot(att_ref[...], wa_ref[...], preferred_element_type=F32)
    r = jnp.dot(ret_ref[...], wr_ref[...], preferred_element_type=F32)
    o_ref[...] = (jax.nn.sigmoid(ga_ref[...]) * a + jax.nn.sigmoid(gr_ref[...]) * r).astype(o_ref.dtype)


def _merge(att, ret, proj, w_att_out, w_ret_out, col_ga, col_gr, tm=1024, tn=512):
    T = att.shape[0]
    D = w_att_out.shape[1]
    tm, tn = _tile(T, tm), _tile(D, tn)
    assert col_ga % tn == 0 and col_gr % tn == 0
    ao, ro = col_ga // tn, col_gr // tn
    return pl.pallas_call(
        _merge_kernel,
        grid=(T // tm, D // tn),
        in_specs=[pl.BlockSpec((tm, att.shape[1]), lambda i, j: (i, 0)),
                  pl.BlockSpec((w_att_out.shape[0], tn), lambda i, j: (0, j)),
                  pl.BlockSpec((tm, ret.shape[1]), lambda i, j: (i, 0)),
                  pl.BlockSpec((w_ret_out.shape[0], tn), lambda i, j: (0, j)),
                  pl.BlockSpec((tm, tn), lambda i, j: (i, ao + j)),
                  pl.BlockSpec((tm, tn), lambda i, j: (i, ro + j))],
        out_specs=pl.BlockSpec((tm, tn), lambda i, j: (i, j)),
        out_shape=jax.ShapeDtypeStruct((T, D), BF16),
        compiler_params=_params(("arbitrary", "arbitrary")),
        name="branch_merge",
    )(att, w_att_out, ret, w_ret_out, proj, proj)


def _layernorm_rows(z, w, b):
    mu = jnp.mean(z, axis=1, keepdims=True)
    cen = z - mu
    var = jnp.mean(cen * cen, axis=1, keepdims=True)
    return cen * lax.rsqrt(var + LN_EPS) * w + b


def _router_kernel(z_ref, lw_ref, lb_ref, wh_ref, wl_ref, bias_ref, idx_ref, rank_ref, gate_ref, cnt_ref, x1_ref,
                   slab_ref, xb_ref, carry_ref, *, rows, pitch):
    i = pl.program_id(0)
    tm = z_ref.shape[0]
    E = N_EXPERTS
    gsz = E // N_GROUPS
    neg_inf = jnp.float32(-jnp.inf)

    @pl.when(i == 0)
    def _():
        carry_ref[...] = jnp.zeros_like(carry_ref)

    x1_ref[...] = _layernorm_rows(z_ref[...], lw_ref[...], lb_ref[...])
    _store_slabs(slab_ref, lambda c: x1_ref[:, c * LANES:(c + 1) * LANES], tm, rows, pitch)
    x = x1_ref[...]
    xh = x.astype(BF16)
    xb_ref[...] = xh
    xl = (x - xh.astype(F32)).astype(BF16)
    logits = (lax.dot_general(wh_ref[...], xh, NT_DIMS, preferred_element_type=F32)
              + lax.dot_general(wh_ref[...], xl, NT_DIMS, preferred_element_type=F32)
              + lax.dot_general(wl_ref[...], xh, NT_DIMS, preferred_element_type=F32))
    scores = jax.nn.sigmoid(logits)
    biased = scores + bias_ref[...]

    row = lax.broadcasted_iota(I32, (gsz, tm), 0)
    gscore = []
    for g in range(N_GROUPS):
        blk = biased[g * gsz:(g + 1) * gsz, :]
        m1 = jnp.max(blk, axis=0, keepdims=True)
        first = jnp.min(jnp.where(blk == m1, row, gsz), axis=0, keepdims=True)
        m2 = jnp.max(jnp.where(row == first, neg_inf, blk), axis=0, keepdims=True)
        gscore.append(m1 + m2)
    gid = lax.broadcasted_iota(I32, (E, tm), 0) // gsz
    keep = jnp.zeros((E, tm), I32)
    for g in range(N_GROUPS):
        beaten = jnp.zeros((1, tm), I32)
        for o in range(N_GROUPS):
            if o == g:
                continue
            wins = (gscore[o] >= gscore[g]) if o < g else (gscore[o] > gscore[g])
            beaten = beaten + wins.astype(I32)
        keep = jnp.where(gid == g, (beaten < TOPK_GROUPS).astype(I32), keep)
    masked = jnp.where(keep > 0, biased, neg_inf)

    eidx = lax.broadcasted_iota(I32, (E, tm), 0)
    sel = jnp.zeros((E, tm), jnp.bool_)
    picks, weights = [], []
    for _ in range(TOP_K):
        m = jnp.max(masked, axis=0, keepdims=True)
        pick = jnp.min(jnp.where(masked == m, eidx, E), axis=0, keepdims=True)
        onehot = eidx == pick
        weights.append(jnp.sum(jnp.where(onehot, scores, 0.0), axis=0, keepdims=True))
        picks.append(pick)
        masked = jnp.where(onehot, neg_inf, masked)
        sel = sel | onehot
    wsum = weights[0]
    for w in weights[1:]:
        wsum = wsum + w

    self32 = sel.astype(F32)
    tri = (lax.broadcasted_iota(I32, (tm, tm), 0) < lax.broadcasted_iota(I32, (tm, tm), 1)).astype(F32).astype(BF16)
    prefix = jnp.dot(self32.astype(BF16), tri, preferred_element_type=F32) + carry_ref[...]
    for kk in range(TOP_K):
        onehot = eidx == picks[kk]
        idx_ref[kk:kk + 1, :] = picks[kk]
        rank_ref[kk:kk + 1, :] = jnp.sum(jnp.where(onehot, prefix, 0.0), axis=0, keepdims=True).astype(I32)
        gate_ref[kk:kk + 1, :] = weights[kk] / wsum * ROUTED_SCALE
    carry_ref[...] = carry_ref[...] + jnp.sum(self32, axis=1, keepdims=True)
    cnt_ref[...] = carry_ref[...]


def _ln_router(z, ln_w, ln_b, w_router, router_bias, rows, pitch, tm=256):
    T, D = z.shape
    E = N_EXPERTS
    tm = _tile(T, tm)
    wt = w_router.astype(F32).T
    wh = wt.astype(BF16)
    wl = (wt - wh.astype(F32)).astype(BF16)
    return pl.pallas_call(
        functools.partial(_router_kernel, rows=rows, pitch=pitch),
        grid=(T // tm,),
        in_specs=[pl.BlockSpec((tm, D), lambda i: (i, 0)),
                  pl.BlockSpec((1, D), lambda i: (0, 0)),
                  pl.BlockSpec((1, D), lambda i: (0, 0)),
                  pl.BlockSpec((E, D), lambda i: (0, 0)),
                  pl.BlockSpec((E, D), lambda i: (0, 0)),
                  pl.BlockSpec((E, 1), lambda i: (0, 0))],
        out_specs=[pl.BlockSpec((TOP_K, tm), lambda i: (0, i)),
                   pl.BlockSpec((TOP_K, tm), lambda i: (0, i)),
                   pl.BlockSpec((TOP_K, tm), lambda i: (0, i)),
                   pl.BlockSpec((E, 1), lambda i: (0, 0)),
                   pl.BlockSpec((tm, D), lambda i: (i, 0)),
                   pl.BlockSpec((tm * pitch, LANES), lambda i: (i, 0)),
                   pl.BlockSpec((tm, D), lambda i: (i, 0))],
        out_shape=[jax.ShapeDtypeStruct((TOP_K, T), I32),
                   jax.ShapeDtypeStruct((TOP_K, T), I32),
                   jax.ShapeDtypeStruct((TOP_K, T), F32),
                   jax.ShapeDtypeStruct((E, 1), F32),
                   jax.ShapeDtypeStruct((T, D), F32),
                   jax.ShapeDtypeStruct((T * pitch, LANES), I32),
                   jax.ShapeDtypeStruct((T, D), BF16)],
        scratch_shapes=[pltpu.VMEM((E, 1), F32)],
        compiler_params=_params(("arbitrary",)),
        name="layernorm_router",
    )(z, ln_w.astype(F32).reshape(1, -1), ln_b.astype(F32).reshape(1, -1), wh, wl,
      router_bias.astype(F32).reshape(E, 1))


def _slot_kernel(start_ref, idx_ref, rank_ref, slot_ref):
    idx = idx_ref[...]
    acc = rank_ref[...]
    for e in range(N_EXPERTS):
        acc = acc + jnp.where(idx == e, start_ref[e], 0)
    slot_ref[...] = acc


def _slots(pad_start, idx, rank, tm=2048):
    T = idx.shape[1]
    tm = _tile(T, tm)
    return pl.pallas_call(
        _slot_kernel,
        grid=(T // tm,),
        in_specs=[pl.BlockSpec(memory_space=pltpu.SMEM),
                  pl.BlockSpec((TOP_K, tm), lambda i: (0, i)),
                  pl.BlockSpec((TOP_K, tm), lambda i: (0, i))],
        out_specs=pl.BlockSpec((TOP_K, tm), lambda i: (0, i)),
        out_shape=jax.ShapeDtypeStruct((TOP_K, T), I32),
        compiler_params=_params(("arbitrary",)),
        name="slot_index",
    )(pad_start, idx, rank)


def _zero_fill_plan(npad, start, te, fn):
    pos = start
    bit = te // 2
    while bit >= 1:
        take = (npad & bit) != 0

        @pl.when(take)
        def _(pos=pos, bit=bit):
            fn(pos, bit)

        pos = pos + jnp.where(take, bit, 0)
        bit //= 2


def _dispatch_kernel(fill_start_ref, fill_len_ref, slot_ref, x_ref, xs_ref, zero_ref, sem, zsem, *, te, rows, pitch):
    i = pl.program_id(0)
    tm = x_ref.shape[0] // pitch

    def start_token(t, carry):
        src = x_ref.at[pl.ds(pl.multiple_of(t * pitch, SUBLANES), rows)]
        for k in range(TOP_K):
            dst = xs_ref.at[pl.ds(pl.multiple_of(slot_ref[t * TOP_K + k] * pitch, SUBLANES), rows)]
            pltpu.make_async_copy(src, dst, sem).start(priority=k % 2)
        return carry

    lax.fori_loop(0, tm, start_token, 0)

    @pl.when(i == 0)
    def _():
        zero_ref[...] = jnp.zeros_like(zero_ref)

        def zero_copy(pos, n):
            return pltpu.make_async_copy(zero_ref.at[pl.ds(0, n * pitch)],
                                         xs_ref.at[pl.ds(pl.multiple_of(pos * pitch, SUBLANES), n * pitch)], zsem)

        def start_fill(e, carry):
            _zero_fill_plan(fill_len_ref[e], fill_start_ref[e], te, lambda pos, n: zero_copy(pos, n).start())
            return carry

        def wait_fill(e, carry):
            _zero_fill_plan(fill_len_ref[e], fill_start_ref[e], te, lambda pos, n: zero_copy(pos, n).wait())
            return carry

        lax.fori_loop(0, N_EXPERTS, start_fill, 0)
        lax.fori_loop(0, N_EXPERTS, wait_fill, 0)

    def wait_token(t, carry):
        for k in range(TOP_K):
            pltpu.make_async_copy(x_ref.at[pl.ds(0, rows)], xs_ref.at[pl.ds(0, rows)], sem).wait()
        return carry

    lax.fori_loop(0, tm, wait_token, 0)


def _dispatch(x_slab, slot_flat, fill_start, fill_len, n_slots, te, rows, pitch, tm=256):
    T = x_slab.shape[0] // pitch
    tm = _tile(T, tm)
    return pl.pallas_call(
        functools.partial(_dispatch_kernel, te=te, rows=rows, pitch=pitch),
        grid_spec=pltpu.PrefetchScalarGridSpec(
            num_scalar_prefetch=2,
            grid=(T // tm,),
            in_specs=[pl.BlockSpec((tm * TOP_K,), lambda i, fs, fl: (i,), memory_space=pltpu.SMEM),
                      pl.BlockSpec((tm * pitch, LANES), lambda i, fs, fl: (i, 0))],
            out_specs=pl.BlockSpec(memory_space=pl.ANY),
            scratch_shapes=[pltpu.VMEM((te // 2 * pitch, LANES), I32),
                            pltpu.SemaphoreType.DMA(()),
                            pltpu.SemaphoreType.DMA(())]),
        out_shape=jax.ShapeDtypeStruct((n_slots * pitch, LANES), I32),
        compiler_params=_params(("arbitrary",)),
        name="moe_dispatch",
    )(fill_start, fill_len, slot_flat, x_slab)


def _ffn_partial(xb, wg, wu, wd):
    hid = _silu(jnp.dot(xb, wg, preferred_element_type=F32)) * jnp.dot(xb, wu, preferred_element_type=F32)
    return jnp.dot(hid.astype(BF16), wd, preferred_element_type=F32)


def _expert_kernel(blk_e_ref, nvalid_ref, x_ref, wg_ref, wu_ref, wd_ref, o_ref, acc_ref, *, rows, pitch):
    b = pl.program_id(0)
    j = pl.program_id(1)
    te = acc_ref.shape[0]

    @pl.when(b < nvalid_ref[0])
    def _():
        @pl.when((b == 0) & (j == 0))
        def _():
            acc_ref[...] = jnp.zeros_like(acc_ref)

        def partial_sum():
            pairs = [_unpack_pair(x_ref[pl.ds(r, te, stride=pitch), :]) for r in range(rows)]
            xb = jnp.concatenate([lo.astype(BF16) for lo, _ in pairs] + [hi.astype(BF16) for _, hi in pairs], axis=1)
            return acc_ref[...] + _ffn_partial(xb, wg_ref[...], wu_ref[...], wd_ref[...])

        last = pl.num_programs(1) - 1

        @pl.when(j < last)
        def _():
            acc_ref[...] = partial_sum()

        @pl.when(j == last)
        def _():
            total = partial_sum()
            _store_slabs(o_ref, lambda c: total[:, c * LANES:(c + 1) * LANES], te, rows, pitch)
            acc_ref[...] = jnp.zeros_like(acc_ref)


def _routed_ffn(xs, blk_e, nvalid, wg, wu, wd, te, rows, pitch, tj=256):
    P = xs.shape[0] // pitch
    _, D, DE = wg.shape
    tj = _tile(DE, tj)
    nj = DE // tj
    nblk = P // te

    def bclamp(b, nv):
        return jnp.minimum(b, nv[0] - 1)

    def jclamp(b, j, nv):
        return jnp.where(b < nv[0], j, nj - 1)

    return pl.pallas_call(
        functools.partial(_expert_kernel, rows=rows, pitch=pitch),
        grid_spec=pltpu.PrefetchScalarGridSpec(
            num_scalar_prefetch=2,
            grid=(nblk, nj),
            in_specs=[pl.BlockSpec((te * pitch, LANES), lambda b, j, be, nv: (bclamp(b, nv), 0)),
                      pl.BlockSpec((None, D, tj), lambda b, j, be, nv: (be[bclamp(b, nv)], 0, jclamp(b, j, nv))),
                      pl.BlockSpec((None, D, tj), lambda b, j, be, nv: (be[bclamp(b, nv)], 0, jclamp(b, j, nv))),
                      pl.BlockSpec((None, tj, D), lambda b, j, be, nv: (be[bclamp(b, nv)], jclamp(b, j, nv), 0))],
            out_specs=pl.BlockSpec((te * pitch, LANES), lambda b, j, be, nv: (bclamp(b, nv), 0)),
            scratch_shapes=[pltpu.VMEM((te, D), F32)]),
        out_shape=jax.ShapeDtypeStruct((P * pitch, LANES), I32),
        compiler_params=_params(("arbitrary", "arbitrary")),
        name="routed_experts",
    )(blk_e, nvalid, xs, wg, wu, wd)


def _shared_kernel(x_ref, wg_ref, wu_ref, wd_ref, o_ref):
    @pl.when(pl.program_id(1) == 0)
    def _():
        o_ref[...] = jnp.zeros_like(o_ref)

    o_ref[...] += _ffn_partial(x_ref[...], wg_ref[...], wu_ref[...], wd_ref[...])


def _shared_ffn(xb, wg, wu, wd, tm=512, tj=256):
    T, D = xb.shape
    DS = wg.shape[1]
    tm, tj = _tile(T, tm), _tile(DS, tj)
    return pl.pallas_call(
        _shared_kernel,
        grid=(T // tm, DS // tj),
        in_specs=[pl.BlockSpec((tm, D), lambda i, j: (i, 0)),
                  pl.BlockSpec((D, tj), lambda i, j: (0, j)),
                  pl.BlockSpec((D, tj), lambda i, j: (0, j)),
                  pl.BlockSpec((tj, D), lambda i, j: (j, 0))],
        out_specs=pl.BlockSpec((tm, D), lambda i, j: (i, 0)),
        out_shape=jax.ShapeDtypeStruct((T, D), F32),
        compiler_params=_params(("arbitrary", "arbitrary")),
        name="shared_expert",
    )(xb, wg, wu, wd)


def _combine_kernel(slot_ref, slot_next_ref, gate_ref, x_ref, sh_ref, lw_ref, lb_ref, ys_ref, o1_ref, o2_ref,
                    buf_ref, z_ref, sem, *, rows, pitch, steps1):
    i = pl.program_id(0)
    nsteps = pl.num_programs(0)
    tm, d_model = x_ref.shape
    half = d_model // 2
    cur = i % 2

    def start_all(srefs, buf):
        def body(t, carry):
            for kk in range(TOP_K):
                src = ys_ref.at[pl.ds(pl.multiple_of(srefs[t * TOP_K + kk] * pitch, SUBLANES), rows)]
                dst = buf_ref.at[buf, kk, pl.ds(pl.multiple_of(t * pitch, SUBLANES), rows)]
                pltpu.make_async_copy(src, dst, sem.at[buf]).start(priority=kk % 2)
            return carry
        lax.fori_loop(0, tm, body, 0)

    @pl.when(i == 0)
    def _():
        start_all(slot_ref, 0)

    @pl.when(i + 1 < nsteps)
    def _():
        start_all(slot_next_ref, 1 - cur)

    def wait_body(t, carry):
        for kk in range(TOP_K):
            pltpu.make_async_copy(ys_ref.at[pl.ds(0, rows)], buf_ref.at[cur, 0, pl.ds(0, rows)], sem.at[cur]).wait()
        return carry

    lax.fori_loop(0, tm, wait_body, 0)

    gates = gate_ref[...]
    gk = [gates[:, kk:kk + 1] for kk in range(TOP_K)]
    for r in range(rows):
        lo_acc = hi_acc = None
        for kk in range(TOP_K):
            lo, hi = _unpack_pair(buf_ref[cur, kk, pl.ds(r, tm, stride=pitch), :])
            lo_acc = gk[kk] * lo if lo_acc is None else lo_acc + gk[kk] * lo
            hi_acc = gk[kk] * hi if hi_acc is None else hi_acc + gk[kk] * hi
        for c0, acc in ((r * LANES, lo_acc), (half + r * LANES, hi_acc)):
            cols = slice(c0, c0 + LANES)
            z_ref[:, cols] = ALPHA * x_ref[:, cols] + (acc + sh_ref[:, cols])
    y = _layernorm_rows(z_ref[...], lw_ref[...], lb_ref[...])

    @pl.when(i < steps1)
    def _():
        o1_ref[...] = y

    @pl.when(i >= steps1)
    def _():
        o2_ref[...] = y


def _combine(ys, slot_flat, gate_tk, x1, shared, ln_w, ln_b, n1, rows, pitch, tm=128):
    T, D = x1.shape
    tm = _tile(math.gcd(n1, T - n1), tm)
    nsteps = T // tm
    steps1 = n1 // tm
    return pl.pallas_call(
        functools.partial(_combine_kernel, rows=rows, pitch=pitch, steps1=steps1),
        grid=(nsteps,),
        in_specs=[pl.BlockSpec((tm * TOP_K,), lambda i: (i,), memory_space=pltpu.SMEM),
                  pl.BlockSpec((tm * TOP_K,), lambda i: (jnp.minimum(i + 1, nsteps - 1),), memory_space=pltpu.SMEM),
                  pl.BlockSpec((tm, TOP_K), lambda i: (i, 0)),
                  pl.BlockSpec((tm, D), lambda i: (i, 0)),
                  pl.BlockSpec((tm, D), lambda i: (i, 0)),
                  pl.BlockSpec((1, D), lambda i: (0, 0)),
                  pl.BlockSpec((1, D), lambda i: (0, 0)),
                  pl.BlockSpec(memory_space=pl.ANY)],
        out_specs=[pl.BlockSpec((tm, D), lambda i: (jnp.minimum(i, steps1 - 1), 0)),
                   pl.BlockSpec((tm, D), lambda i: (jnp.maximum(i - steps1, 0), 0))],
        out_shape=[jax.ShapeDtypeStruct((n1, D), F32), jax.ShapeDtypeStruct((T - n1, D), F32)],
        scratch_shapes=[pltpu.VMEM((2, TOP_K, tm * pitch, LANES), I32), pltpu.VMEM((tm, D), F32),
                        pltpu.SemaphoreType.DMA((2,))],
        compiler_params=_params(("arbitrary",)),
        name="moe_combine_layernorm",
    )(slot_flat, slot_flat, gate_tk, x1, shared, ln_w.astype(F32).reshape(1, -1), ln_b.astype(F32).reshape(1, -1),
      ys)


def _seq_tables(groups, blk):
    first, last, pos = [], [], []
    for nseq, slen in groups:
        nb = slen // blk
        for _ in range(nseq):
            for b in range(nb):
                first.append(int(b == 0))
                last.append(int(b == nb - 1))
                pos.append(b)
    return (jnp.asarray(np.array(first, np.int32)), jnp.asarray(np.array(last, np.int32)),
            jnp.asarray(np.array(pos, np.int32)))


def _expert_block_rows(d_model):
    return 512 if d_model >= 4096 else 128


def kernel(x_prompt, x_sample, rel_table, w_in, attn_sink, ret_decay, ret_gn_w, w_att_out, w_ret_out, w_o, ln1_w,
           ln1_b, w_router, router_bias, w_exp_gate, w_exp_up, w_exp_down, w_sh_gate, w_sh_up, w_sh_down, ln2_w,
           ln2_b):
    assert DEPTH == 1
    B1, S1, D = x_prompt.shape
    B2, S2, _ = x_sample.shape
    assert S1 % BLOCK == 0 and S2 % BLOCK == 0 and BLOCK == RET_CHUNK
    xa, xb = x_prompt.reshape(B1 * S1, D), x_sample.reshape(B2 * S2, D)
    n1 = B1 * S1
    T = n1 + B2 * S2
    groups = ((B1, S1), (B2, S2))
    first, last, _ = _seq_tables(groups, BLOCK)
    ret_chunks = math.gcd(8, math.gcd(S1 // RET_CHUNK, S2 // RET_CHUNK))
    rfirst, rlast, rpos = _seq_tables(groups, ret_chunks * RET_CHUNK)

    att_q, att_kv = ATT_HEADS * HEAD_DIM, ATT_KV_HEADS * HEAD_DIM
    ret_qk, ret_v = RET_HEADS * RET_DK, RET_HEADS * RET_DV
    cols = np.concatenate([[0], np.cumsum([att_q, att_kv, att_kv, ret_qk, ret_qk, ret_v, ret_v, D, D])])
    c_aq, c_ak, c_av, c_rq, c_rk, c_rv, c_rg, c_ga, c_gr = (int(c) for c in cols[:-1])
    assert int(cols[-1]) == w_in.shape[2]
    rows = _slab_rows(D)
    pitch = _slab_pitch(rows)

    x_bf = jnp.concatenate([xa, xb], axis=0).astype(BF16)
    proj = _matmul(x_bf, w_in[0].astype(BF16), F32)
    att = _attention(proj, first, last, attn_sink[0], rel_table, c_aq, c_ak, c_av)
    ret = _retention(proj, rfirst, rlast, rpos, ret_chunks, ret_decay[0], ret_gn_w[0], max(S1, S2),
                     c_rq, c_rk, c_rv, c_rg)
    merged = _merge(att, ret, proj, w_att_out[0].astype(BF16), w_ret_out[0].astype(BF16), c_ga, c_gr)
    z1 = _residual_matmul(merged, w_o[0].astype(BF16), xa, xb)

    idx, rank, gate, counts, x1, x1_slab, x1b = _ln_router(z1, ln1_w[0], ln1_b[0], w_router[0], router_bias[0],
                                                           rows, pitch)
    te = _expert_block_rows(D)
    cnt = counts[:, 0].astype(I32)
    padded = (cnt + te - 1) // te * te
    pad_end = jnp.cumsum(padded)
    pad_start = pad_end - padded
    nblk = (T * TOP_K) // te + N_EXPERTS
    blk_e = jnp.minimum(jnp.searchsorted(pad_end, jnp.arange(nblk, dtype=I32) * te, side='right'),
                        N_EXPERTS - 1).astype(I32)
    nvalid = (pad_end[-1:] // te).astype(I32)
    slot = _slots(pad_start.astype(I32), idx, rank)
    slot_flat = slot.T.reshape(-1)

    xs = _dispatch(x1_slab, slot_flat, (pad_start + cnt).astype(I32), (padded - cnt).astype(I32), nblk * te, te,
                   rows, pitch)
    ys = _routed_ffn(xs, blk_e, nvalid, w_exp_gate[0].astype(BF16), w_exp_up[0].astype(BF16),
                     w_exp_down[0].astype(BF16), te, rows, pitch)
    shared = _shared_ffn(x1b, w_sh_gate[0].astype(BF16), w_sh_up[0].astype(BF16), w_sh_down[0].astype(BF16))
    y1, y2 = _combine(ys, slot_flat, gate.T, x1, shared, ln2_w[0], ln2_b[0], n1, rows, pitch)
    return (y1.reshape(B1, S1, D), y2.reshape(B2, S2, D))
```

```python
import functools
import math

import numpy as np
import jax
import jax.numpy as jnp
from jax import lax
from jax.experimental import pallas as pl
from jax.experimental.pallas import tpu as pltpu

F32 = jnp.float32
BF16 = jnp.bfloat16
I32 = jnp.int32

HEAD_DIM = 128
ATT_HEADS = 16
ATT_KV_HEADS = 4
WINDOW = 128
BLOCK = 128
REL_BUCKETS = 32
REL_MAX_DIST = 128
RET_HEADS = 8
RET_DK = 128
RET_DV = 256
RET_CHUNK = 128
ROPE_BASE = 10000.0
N_EXPERTS = 128
TOP_K = 8
N_GROUPS = 8
TOPK_GROUPS = 4
ROUTED_SCALE = 2.5
LN_EPS = 1e-5
GN_EPS = 1e-5
DEPTH = 1
ALPHA = (2.0 * DEPTH) ** 0.25

V7X_VMEM_BUDGET_BYTES = 56 * 1024 * 1024
V7X_VMEM_EXPERT_BYTES = 63 * 1024 * 1024
LANES = 128
SUBLANES = 8

NT_DIMS = (((1,), (1,)), ((), ()))
TN_DIMS = (((0,), (0,)), ((), ()))


def _tile(dim, pref):
    t = min(dim, pref)
    assert dim % t == 0, (dim, pref)
    return t


def _params(sem, vmem=V7X_VMEM_BUDGET_BYTES):
    return pltpu.CompilerParams(dimension_semantics=sem, vmem_limit_bytes=vmem)


def _silu(v):
    return v * jax.nn.sigmoid(v)


def _slab_rows(d_model):
    rows = d_model // (2 * LANES)
    assert rows % SUBLANES == 0
    return rows


def _slab_pitch(rows):
    return rows if (rows // SUBLANES) % 2 == 1 else rows + SUBLANES


def _pack_pair(lo, hi):
    lo_bits = lax.bitcast_convert_type(lo.astype(BF16).astype(F32), I32)
    hi_bits = lax.bitcast_convert_type(hi.astype(BF16).astype(F32), I32)
    return hi_bits | lax.shift_right_logical(lo_bits, 16)


def _unpack_pair(w):
    lo = lax.bitcast_convert_type(lax.shift_left(w, 16), F32)
    hi = lax.bitcast_convert_type(w & jnp.int32(-65536), F32)
    return lo, hi


def _store_slabs(slab_ref, col, n_tok, rows, pitch):
    for r in range(rows):
        slab_ref[pl.ds(r, n_tok, stride=pitch), :] = _pack_pair(col(r), col(rows + r))
    for r in range(rows, pitch):
        slab_ref[pl.ds(r, n_tok, stride=pitch), :] = jnp.zeros((n_tok, LANES), I32)


def _matmul_kernel(x_ref, w_ref, o_ref):
    o_ref[...] = jnp.dot(x_ref[...], w_ref[...], preferred_element_type=F32).astype(o_ref.dtype)


def _concat_cast_kernel(xa_ref, xb_ref, o_ref, *, steps1):
    @pl.when(pl.program_id(0) < steps1)
    def _():
        o_ref[...] = xa_ref[...].astype(o_ref.dtype)

    @pl.when(pl.program_id(0) >= steps1)
    def _():
        o_ref[...] = xb_ref[...].astype(o_ref.dtype)


def _concat_cast(xa, xb, dtype, tm=256):
    n1, D = xa.shape
    T = n1 + xb.shape[0]
    tm = _tile(math.gcd(n1, T - n1), tm)
    steps1 = n1 // tm
    return pl.pallas_call(
        functools.partial(_concat_cast_kernel, steps1=steps1),
        grid=(T // tm,),
        in_specs=[pl.BlockSpec((tm, D), lambda i: (jnp.minimum(i, steps1 - 1), 0)),
                  pl.BlockSpec((tm, D), lambda i: (jnp.maximum(i - steps1, 0), 0))],
        out_specs=pl.BlockSpec((tm, D), lambda i: (i, 0)),
        out_shape=jax.ShapeDtypeStruct((T, D), dtype),
        compiler_params=_params(("arbitrary",)),
        name="concat_cast",
    )(xa, xb)


def _matmul(x, w, out_dtype, tm=1024, tn=1024):
    M, K = x.shape
    N = w.shape[1]
    tm, tn = _tile(M, tm), _tile(N, tn)
    return pl.pallas_call(
        _matmul_kernel,
        grid=(M // tm, N // tn),
        in_specs=[pl.BlockSpec((tm, K), lambda i, j: (i, 0)),
                  pl.BlockSpec((K, tn), lambda i, j: (0, j))],
        out_specs=pl.BlockSpec((tm, tn), lambda i, j: (i, j)),
        out_shape=jax.ShapeDtypeStruct((M, N), out_dtype),
        compiler_params=_params(("arbitrary", "arbitrary")),
        name="proj_matmul",
    )(x, w)


def _residual_matmul_kernel(m_ref, w_ref, xa_ref, xb_ref, o_ref, *, steps1):
    mm = jnp.dot(m_ref[...], w_ref[...], preferred_element_type=F32)

    @pl.when(pl.program_id(0) < steps1)
    def _():
        o_ref[...] = ALPHA * xa_ref[...] + mm

    @pl.when(pl.program_id(0) >= steps1)
    def _():
        o_ref[...] = ALPHA * xb_ref[...] + mm


def _residual_matmul(merged, w_o, xa, xb, tm=1024, tn=512):
    T, K = merged.shape
    D = w_o.shape[1]
    n1 = xa.shape[0]
    tm, tn = _tile(math.gcd(n1, T - n1), tm), _tile(D, tn)
    steps1, nt = n1 // tm, D // tn
    return pl.pallas_call(
        functools.partial(_residual_matmul_kernel, steps1=steps1),
        grid=(T // tm, nt),
        in_specs=[pl.BlockSpec((tm, K), lambda i, n: (i, 0)),
                  pl.BlockSpec((K, tn), lambda i, n: (0, n)),
                  pl.BlockSpec((tm, tn), lambda i, n: (jnp.minimum(i, steps1 - 1), jnp.where(i < steps1, n, nt - 1))),
                  pl.BlockSpec((tm, tn), lambda i, n: (jnp.maximum(i - steps1, 0), jnp.where(i < steps1, 0, n)))],
        out_specs=pl.BlockSpec((tm, tn), lambda i, n: (i, n)),
        out_shape=jax.ShapeDtypeStruct((T, D), F32),
        compiler_params=_params(("arbitrary", "arbitrary")),
        name="out_proj_residual",
    )(merged, w_o, xa, xb)


def _attn_kernel(first_ref, last_ref, sink_ref, q_ref, kp_ref, kc_ref, kn_ref, vp_ref, vc_ref, vn_ref,
                 bias_ref, o_ref, *, kv_heads, group, scale):
    n = pl.program_id(0)
    neg_inf = jnp.float32(-jnp.inf)
    pen_p = jnp.where(first_ref[n] == 1, neg_inf, jnp.float32(0.0))
    pen_n = jnp.where(last_ref[n] == 1, neg_inf, jnp.float32(0.0))
    lane = lax.broadcasted_iota(I32, (1, 3 * BLOCK), 1)
    pen = jnp.where(lane < BLOCK, pen_p, jnp.where(lane >= 2 * BLOCK, pen_n, jnp.float32(0.0)))
    for h in range(kv_heads):
        cols = slice(h * HEAD_DIM, (h + 1) * HEAD_DIM)
        kcat = jnp.concatenate([kp_ref[:, cols], kc_ref[:, cols], kn_ref[:, cols]], axis=0).astype(BF16)
        vcat = jnp.concatenate([vp_ref[:, cols], vc_ref[:, cols], vn_ref[:, cols]], axis=0).astype(BF16)
        heads = [h * group + g for g in range(group)]
        qcols = [slice(hd * HEAD_DIM, (hd + 1) * HEAD_DIM) for hd in heads]
        sinks = [sink_ref[hd] for hd in heads]
        s = [lax.dot_general(q_ref[:, c].astype(BF16), kcat, NT_DIMS, preferred_element_type=F32) * scale
             + bias_ref[hd] + pen for hd, c in zip(heads, qcols)]
        m = [jnp.maximum(jnp.max(si, axis=1, keepdims=True), sk) for si, sk in zip(s, sinks)]
        e = [jnp.exp(si - mi) for si, mi in zip(s, m)]
        denom = [jnp.sum(ei, axis=1, keepdims=True) + jnp.exp(sk - mi) for ei, sk, mi in zip(e, sinks, m)]
        p = [(ei * (1.0 / di)).astype(BF16) for ei, di in zip(e, denom)]
        for c, pi in zip(qcols, p):
            o_ref[:, c] = jnp.dot(pi, vcat, preferred_element_type=F32).astype(o_ref.dtype)


def _t5_bucket(rel):
    nb = REL_BUCKETS // 2
    max_exact = nb // 2
    ret = (rel > 0).astype(I32) * nb
    n = jnp.abs(rel)
    nf = jnp.maximum(n, max_exact).astype(F32)
    large = max_exact + (jnp.log(nf / max_exact) / math.log(REL_MAX_DIST / max_exact) * (nb - max_exact)).astype(I32)
    large = jnp.minimum(large, nb - 1)
    return ret + jnp.where(n < max_exact, n, large)


def _attention(proj, first, last, sink, rel_table, col_q, col_k, col_v):
    T = proj.shape[0]
    nb = T // BLOCK
    group = ATT_HEADS // ATT_KV_HEADS
    qw, kw = ATT_HEADS * HEAD_DIM, ATT_KV_HEADS * HEAD_DIM
    assert col_q % qw == 0 and col_k % kw == 0 and col_v % kw == 0
    qo, ko, vo = col_q // qw, col_k // kw, col_v // kw
    rel = jnp.arange(3 * BLOCK)[None, :] - BLOCK - jnp.arange(BLOCK)[:, None]
    onehot = (_t5_bucket(rel)[:, :, None] == jnp.arange(REL_BUCKETS)[None, None, :]).astype(F32)
    bias = jnp.einsum('qkb,bh->hqk', onehot, rel_table.astype(F32), precision=lax.Precision.HIGHEST)
    bias = jnp.where((jnp.abs(rel) <= WINDOW)[None, :, :], bias, -jnp.inf)

    def kv_spec(off, shift):
        def imap(n, first, last):
            return (jnp.clip(n + shift, 0, nb - 1), off)
        return pl.BlockSpec((BLOCK, kw), imap)

    grid_spec = pltpu.PrefetchScalarGridSpec(
        num_scalar_prefetch=2,
        grid=(nb,),
        in_specs=[pl.BlockSpec(memory_space=pltpu.SMEM),
                  pl.BlockSpec((BLOCK, qw), lambda n, first, last: (n, qo)),
                  kv_spec(ko, -1), kv_spec(ko, 0), kv_spec(ko, 1),
                  kv_spec(vo, -1), kv_spec(vo, 0), kv_spec(vo, 1),
                  pl.BlockSpec((ATT_HEADS, BLOCK, 3 * BLOCK), lambda n, first, last: (0, 0, 0))],
        out_specs=pl.BlockSpec((BLOCK, qw), lambda n, first, last: (n, 0)),
    )
    return pl.pallas_call(
        functools.partial(_attn_kernel, kv_heads=ATT_KV_HEADS, group=group, scale=HEAD_DIM ** -0.5),
        grid_spec=grid_spec,
        out_shape=jax.ShapeDtypeStruct((T, qw), BF16),
        compiler_params=_params(("arbitrary",)),
        name="window_attention",
    )(first, last, sink.astype(F32), proj, proj, proj, proj, proj, proj, proj, bias)


def _rope(x, cos2, sin2):
    return x * cos2 + pltpu.roll(x, RET_DK // 2, axis=1) * sin2


def _ret_bwd_kernel(last_ref, posblk_ref, cdec_ref, q_ref, k_ref, v_ref, cos_ref, sin_ref, qb_ref, kb_ref, o_ref,
                    state_ref, *, chunks):
    h = pl.program_id(0)
    n = pl.num_programs(1) - 1 - pl.program_id(1)
    C = RET_CHUNK

    @pl.when(last_ref[n] == 1)
    def _():
        state_ref[...] = jnp.zeros_like(state_ref)

    state = state_ref[...]
    cdec = cdec_ref[1, h]
    for c in range(chunks - 1, -1, -1):
        sl = slice(c * C, (c + 1) * C)
        cos2, sin2 = cos_ref[sl, :], sin_ref[sl, :]
        q = _rope(q_ref[sl, :], cos2, sin2)
        k = _rope(k_ref[sl, :], cos2, sin2) * (RET_DK ** -0.5)
        vb = v_ref[sl, :].astype(BF16)
        o_ref[sl, :] = jnp.dot((q * qb_ref[0]).astype(BF16), state.astype(BF16), preferred_element_type=F32)
        state = state * cdec + lax.dot_general((k * kb_ref[0]).astype(BF16), vb, TN_DIMS,
                                               preferred_element_type=F32)
    state_ref[...] = state


def _ret_fwd_kernel(first_ref, posblk_ref, cdec_ref, q_ref, k_ref, v_ref, g_ref, cos_ref, sin_ref, dmat_ref, qf_ref,
                    kf_ref, ob_ref, gnw_ref, o_ref, state_ref, *, chunks):
    h = pl.program_id(0)
    n = pl.program_id(1)
    C = RET_CHUNK

    @pl.when(first_ref[n] == 1)
    def _():
        state_ref[...] = jnp.zeros_like(state_ref)

    state = state_ref[...]
    cdec = cdec_ref[0, h]
    for c in range(chunks):
        sl = slice(c * C, (c + 1) * C)
        cos2, sin2 = cos_ref[sl, :], sin_ref[sl, :]
        q = _rope(q_ref[sl, :], cos2, sin2)
        k = _rope(k_ref[sl, :], cos2, sin2) * (RET_DK ** -0.5)
        vb = v_ref[sl, :].astype(BF16)
        intra = lax.dot_general(q.astype(BF16), k.astype(BF16), NT_DIMS, preferred_element_type=F32) * dmat_ref[0]
        out = (jnp.dot(intra.astype(BF16), vb, preferred_element_type=F32)
               + jnp.dot((q * qf_ref[0]).astype(BF16), state.astype(BF16), preferred_element_type=F32)
               + ob_ref[sl, :])
        state = state * cdec + lax.dot_general((k * kf_ref[0]).astype(BF16), vb, TN_DIMS,
                                               preferred_element_type=F32)
        mu = jnp.mean(out, axis=1, keepdims=True)
        cen = out - mu
        var = jnp.mean(cen * cen, axis=1, keepdims=True)
        y = cen * lax.rsqrt(var + GN_EPS) * gnw_ref[...]
        o_ref[sl, :] = (_silu(g_ref[sl, :]) * y).astype(o_ref.dtype)
    state_ref[...] = state


def _retention(proj, first, last, posblk, chunks, ret_decay, ret_gn_w, s_max, col_q, col_k, col_v, col_g):
    T = proj.shape[0]
    C = RET_CHUNK
    R = chunks * C
    nc = T // R
    H = RET_HEADS
    qo, ko, vo, go = col_q // RET_DK, col_k // RET_DK, col_v // RET_DV, col_g // RET_DV
    assert col_q % RET_DK == 0 and col_k % RET_DK == 0 and col_v % RET_DV == 0 and col_g % RET_DV == 0

    half = RET_DK // 2
    inv = ROPE_BASE ** (-jnp.arange(half, dtype=F32) * 2.0 / RET_DK)
    ang = jnp.arange(s_max, dtype=F32)[:, None] * inv[None, :]
    cos2 = jnp.concatenate([jnp.cos(ang), jnp.cos(ang)], axis=1)
    sin2 = jnp.concatenate([-jnp.sin(ang), jnp.sin(ang)], axis=1)

    lg = -jnp.exp(ret_decay.astype(F32))
    pos = jnp.arange(C, dtype=F32)
    diff = pos[:, None] - pos[None, :]
    dec_f = jnp.where((diff >= 0)[None], jnp.exp(jnp.maximum(diff, 0.0)[None] * lg[0][:, None, None]), 0.0)
    dec_b = jnp.where((diff < 0)[None], jnp.exp(jnp.maximum(-diff, 0.0)[None] * lg[1][:, None, None]), 0.0)
    dmat = dec_f + dec_b

    def rows(tab):
        return jnp.broadcast_to(tab[:, :, None], (H, C, RET_DK))

    qf = rows(jnp.exp((pos[None, :] + 1.0) * lg[0][:, None]))
    kf = rows(jnp.exp((C - 1.0 - pos)[None, :] * lg[0][:, None]))
    qb = rows(jnp.exp((C - pos)[None, :] * lg[1][:, None]))
    kb = rows(jnp.exp(pos[None, :] * lg[1][:, None]))
    cdec = jnp.exp(C * lg)

    smem = pl.BlockSpec(memory_space=pltpu.SMEM)
    rev = lambda c: nc - 1 - c
    bwd_specs = [
        smem,
        pl.BlockSpec((R, RET_DK), lambda h, c, last, posb: (rev(c), qo + h)),
        pl.BlockSpec((R, RET_DK), lambda h, c, last, posb: (rev(c), ko + h)),
        pl.BlockSpec((R, RET_DV), lambda h, c, last, posb: (rev(c), vo + h)),
        pl.BlockSpec((R, RET_DK), lambda h, c, last, posb: (posb[rev(c)], 0)),
        pl.BlockSpec((R, RET_DK), lambda h, c, last, posb: (posb[rev(c)], 0)),
        pl.BlockSpec((1, C, RET_DK), lambda h, c, last, posb: (h, 0, 0)),
        pl.BlockSpec((1, C, RET_DK), lambda h, c, last, posb: (h, 0, 0)),
    ]
    out_b = pl.pallas_call(
        functools.partial(_ret_bwd_kernel, chunks=chunks),
        grid_spec=pltpu.PrefetchScalarGridSpec(
            num_scalar_prefetch=2, grid=(H, nc), in_specs=bwd_specs,
            out_specs=pl.BlockSpec((R, RET_DV), lambda h, c, last, posb: (rev(c), h)),
            scratch_shapes=[pltpu.VMEM((RET_DK, RET_DV), F32)]),
        out_shape=jax.ShapeDtypeStruct((T, H * RET_DV), F32),
        compiler_params=_params(("arbitrary", "arbitrary")),
        name="retention_backward",
    )(last, posblk, cdec, proj, proj, proj, cos2, sin2, qb, kb)

    fwd_specs = [
        smem,
        pl.BlockSpec((R, RET_DK), lambda h, n, first, posb: (n, qo + h)),
        pl.BlockSpec((R, RET_DK), lambda h, n, first, posb: (n, ko + h)),
        pl.BlockSpec((R, RET_DV), lambda h, n, first, posb: (n, vo + h)),
        pl.BlockSpec((R, RET_DV), lambda h, n, first, posb: (n, go + h)),
        pl.BlockSpec((R, RET_DK), lambda h, n, first, posb: (posb[n], 0)),
        pl.BlockSpec((R, RET_DK), lambda h, n, first, posb: (posb[n], 0)),
        pl.BlockSpec((1, C, C), lambda h, n, first, posb: (h, 0, 0)),
        pl.BlockSpec((1, C, RET_DK), lambda h, n, first, posb: (h, 0, 0)),
        pl.BlockSpec((1, C, RET_DK), lambda h, n, first, posb: (h, 0, 0)),
        pl.BlockSpec((R, RET_DV), lambda h, n, first, posb: (n, h)),
        pl.BlockSpec((1, RET_DV), lambda h, n, first, posb: (0, h)),
    ]
    return pl.pallas_call(
        functools.partial(_ret_fwd_kernel, chunks=chunks),
        grid_spec=pltpu.PrefetchScalarGridSpec(
            num_scalar_prefetch=2, grid=(H, nc), in_specs=fwd_specs,
            out_specs=pl.BlockSpec((R, RET_DV), lambda h, n, first, posb: (n, h)),
            scratch_shapes=[pltpu.VMEM((RET_DK, RET_DV), F32)]),
        out_shape=jax.ShapeDtypeStruct((T, H * RET_DV), BF16),
        compiler_params=_params(("arbitrary", "arbitrary")),
        name="retention_forward",
    )(first, posblk, cdec, proj, proj, proj, proj, cos2, sin2, dmat, qf, kf, out_b,
      ret_gn_w.astype(F32).reshape(1, -1))


def _merge_kernel(att_ref, wa_ref, ret_ref, wr_ref, ga_ref, gr_ref, o_ref):
    a = jnp.dot(att_ref[...], wa_ref[...], preferred_element_type=F32)
    r = jnp.dot(ret_ref[...], wr_ref[...], preferred_element_type=F32)
    o_ref[...] = (jax.nn.sigmoid(ga_ref[...]) * a + jax.nn.sigmoid(gr_ref[...]) * r).astype(o_ref.dtype)


def _merge(att, ret, proj, w_att_out, w_ret_out, col_ga, col_gr, tm=1024, tn=512):
    T = att.shape[0]
    D = w_att_out.shape[1]
    tm, tn = _tile(T, tm), _tile(D, tn)
    assert col_ga % tn == 0 and col_gr % tn == 0
    ao, ro = col_ga // tn, col_gr // tn
    return pl.pallas_call(
        _merge_kernel,
        grid=(T // tm, D // tn),
        in_specs=[pl.BlockSpec((tm, att.shape[1]), lambda i, j: (i, 0)),
                  pl.BlockSpec((w_att_out.shape[0], tn), lambda i, j: (0, j)),
                  pl.BlockSpec((tm, ret.shape[1]), lambda i, j: (i, 0)),
                  pl.BlockSpec((w_ret_out.shape[0], tn), lambda i, j: (0, j)),
                  pl.BlockSpec((tm, tn), lambda i, j: (i, ao + j)),
                  pl.BlockSpec((tm, tn), lambda i, j: (i, ro + j))],
        out_specs=pl.BlockSpec((tm, tn), lambda i, j: (i, j)),
        out_shape=jax.ShapeDtypeStruct((T, D), BF16),
        compiler_params=_params(("arbitrary", "arbitrary")),
        name="branch_merge",
    )(att, w_att_out, ret, w_ret_out, proj, proj)


def _layernorm_rows(z, w, b):
    mu = jnp.mean(z, axis=1, keepdims=True)
    cen = z - mu
    var = jnp.mean(cen * cen, axis=1, keepdims=True)
    return cen * lax.rsqrt(var + LN_EPS) * w + b


def _router_kernel(z_ref, lw_ref, lb_ref, wh_ref, wl_ref, bias_ref, idx_ref, rank_ref, gate_ref, cnt_ref, x1_ref,
                   slab_ref, xb_ref, carry_ref, *, rows, pitch):
    i = pl.program_id(0)
    tm = z_ref.shape[0]
    E = N_EXPERTS
    gsz = E // N_GROUPS
    neg_inf = jnp.float32(-jnp.inf)

    @pl.when(i == 0)
    def _():
        carry_ref[...] = jnp.zeros_like(carry_ref)

    x1_ref[...] = _layernorm_rows(z_ref[...], lw_ref[...], lb_ref[...])
    _store_slabs(slab_ref, lambda c: x1_ref[:, c * LANES:(c + 1) * LANES], tm, rows, pitch)
    x = x1_ref[...]
    xh = x.astype(BF16)
    xb_ref[...] = xh
    xl = (x - xh.astype(F32)).astype(BF16)
    logits = (lax.dot_general(wh_ref[...], xh, NT_DIMS, preferred_element_type=F32)
              + lax.dot_general(wh_ref[...], xl, NT_DIMS, preferred_element_type=F32)
              + lax.dot_general(wl_ref[...], xh, NT_DIMS, preferred_element_type=F32))
    scores = jax.nn.sigmoid(logits)
    biased = scores + bias_ref[...]

    row = lax.broadcasted_iota(I32, (gsz, tm), 0)
    gscore = []
    for g in range(N_GROUPS):
        blk = biased[g * gsz:(g + 1) * gsz, :]
        m1 = jnp.max(blk, axis=0, keepdims=True)
        first = jnp.min(jnp.where(blk == m1, row, gsz), axis=0, keepdims=True)
        m2 = jnp.max(jnp.where(row == first, neg_inf, blk), axis=0, keepdims=True)
        gscore.append(m1 + m2)
    gid = lax.broadcasted_iota(I32, (E, tm), 0) // gsz
    keep = jnp.zeros((E, tm), I32)
    for g in range(N_GROUPS):
        beaten = jnp.zeros((1, tm), I32)
        for o in range(N_GROUPS):
            if o == g:
                continue
            wins = (gscore[o] >= gscore[g]) if o < g else (gscore[o] > gscore[g])
            beaten = beaten + wins.astype(I32)
        keep = jnp.where(gid == g, (beaten < TOPK_GROUPS).astype(I32), keep)
    masked = jnp.where(keep > 0, biased, neg_inf)

    eidx = lax.broadcasted_iota(I32, (E, tm), 0)
    sel = jnp.zeros((E, tm), jnp.bool_)
    picks, weights = [], []
    for _ in range(TOP_K):
        m = jnp.max(masked, axis=0, keepdims=True)
        pick = jnp.min(jnp.where(masked == m, eidx, E), axis=0, keepdims=True)
        onehot = eidx == pick
        weights.append(jnp.sum(jnp.where(onehot, scores, 0.0), axis=0, keepdims=True))
        picks.append(pick)
        masked = jnp.where(onehot, neg_inf, masked)
        sel = sel | onehot
    wsum = weights[0]
    for w in weights[1:]:
        wsum = wsum + w

    self32 = sel.astype(F32)
    tri = (lax.broadcasted_iota(I32, (tm, tm), 0) < lax.broadcasted_iota(I32, (tm, tm), 1)).astype(F32).astype(BF16)
    prefix = jnp.dot(self32.astype(BF16), tri, preferred_element_type=F32) + carry_ref[...]
    for kk in range(TOP_K):
        onehot = eidx == picks[kk]
        idx_ref[kk:kk + 1, :] = picks[kk]
        rank_ref[kk:kk + 1, :] = jnp.sum(jnp.where(onehot, prefix, 0.0), axis=0, keepdims=True).astype(I32)
        gate_ref[kk:kk + 1, :] = weights[kk] / wsum * ROUTED_SCALE
    carry_ref[...] = carry_ref[...] + jnp.sum(self32, axis=1, keepdims=True)
    cnt_ref[...] = carry_ref[...]


def _ln_router(z, ln_w, ln_b, w_router, router_bias, rows, pitch, tm=256):
    T, D = z.shape
    E = N_EXPERTS
    tm = _tile(T, tm)
    wt = w_router.astype(F32).T
    wh = wt.astype(BF16)
    wl = (wt - wh.astype(F32)).astype(BF16)
    return pl.pallas_call(
        functools.partial(_router_kernel, rows=rows, pitch=pitch),
        grid=(T // tm,),
        in_specs=[pl.BlockSpec((tm, D), lambda i: (i, 0)),
                  pl.BlockSpec((1, D), lambda i: (0, 0)),
                  pl.BlockSpec((1, D), lambda i: (0, 0)),
                  pl.BlockSpec((E, D), lambda i: (0, 0)),
                  pl.BlockSpec((E, D), lambda i: (0, 0)),
                  pl.BlockSpec((E, 1), lambda i: (0, 0))],
        out_specs=[pl.BlockSpec((TOP_K, tm), lambda i: (0, i)),
                   pl.BlockSpec((TOP_K, tm), lambda i: (0, i)),
                   pl.BlockSpec((TOP_K, tm), lambda i: (0, i)),
                   pl.BlockSpec((E, 1), lambda i: (0, 0)),
                   pl.BlockSpec((tm, D), lambda i: (i, 0)),
                   pl.BlockSpec((tm * pitch, LANES), lambda i: (i, 0)),
                   pl.BlockSpec((tm, D), lambda i: (i, 0))],
        out_shape=[jax.ShapeDtypeStruct((TOP_K, T), I32),
                   jax.ShapeDtypeStruct((TOP_K, T), I32),
                   jax.ShapeDtypeStruct((TOP_K, T), F32),
                   jax.ShapeDtypeStruct((E, 1), F32),
                   jax.ShapeDtypeStruct((T, D), F32),
                   jax.ShapeDtypeStruct((T * pitch, LANES), I32),
                   jax.ShapeDtypeStruct((T, D), BF16)],
        scratch_shapes=[pltpu.VMEM((E, 1), F32)],
        compiler_params=_params(("arbitrary",)),
        name="layernorm_router",
    )(z, ln_w.astype(F32).reshape(1, -1), ln_b.astype(F32).reshape(1, -1), wh, wl,
      router_bias.astype(F32).reshape(E, 1))


def _slot_kernel(start_ref, idx_ref, rank_ref, slot_ref):
    idx = idx_ref[...]
    acc = rank_ref[...]
    for e in range(N_EXPERTS):
        acc = acc + jnp.where(idx == e, start_ref[e], 0)
    slot_ref[...] = acc


def _slots(pad_start, idx, rank, tm=2048):
    T = idx.shape[1]
    tm = _tile(T, tm)
    return pl.pallas_call(
        _slot_kernel,
        grid=(T // tm,),
        in_specs=[pl.BlockSpec(memory_space=pltpu.SMEM),
                  pl.BlockSpec((TOP_K, tm), lambda i: (0, i)),
                  pl.BlockSpec((TOP_K, tm), lambda i: (0, i))],
        out_specs=pl.BlockSpec((TOP_K, tm), lambda i: (0, i)),
        out_shape=jax.ShapeDtypeStruct((TOP_K, T), I32),
        compiler_params=_params(("arbitrary",)),
        name="slot_index",
    )(pad_start, idx, rank)


def _zero_fill_plan(npad, start, te, fn):
    pos = start
    bit = te // 2
    while bit >= 1:
        take = (npad & bit) != 0

        @pl.when(take)
        def _(pos=pos, bit=bit):
            fn(pos, bit)

        pos = pos + jnp.where(take, bit, 0)
        bit //= 2


def _dispatch_kernel(fill_start_ref, fill_len_ref, slot_ref, x_ref, xs_ref, zero_ref, sem, zsem, *, te, rows, pitch):
    i = pl.program_id(0)
    tm = x_ref.shape[0] // pitch

    def start_token(t, carry):
        src = x_ref.at[pl.ds(pl.multiple_of(t * pitch, SUBLANES), rows)]
        for k in range(TOP_K):
            dst = xs_ref.at[pl.ds(pl.multiple_of(slot_ref[t * TOP_K + k] * pitch, SUBLANES), rows)]
            pltpu.make_async_copy(src, dst, sem).start(priority=k % 2)
        return carry

    lax.fori_loop(0, tm, start_token, 0)

    @pl.when(i == 0)
    def _():
        zero_ref[...] = jnp.zeros_like(zero_ref)

        def zero_copy(pos, n):
            return pltpu.make_async_copy(zero_ref.at[pl.ds(0, n * pitch)],
                                         xs_ref.at[pl.ds(pl.multiple_of(pos * pitch, SUBLANES), n * pitch)], zsem)

        def start_fill(e, carry):
            _zero_fill_plan(fill_len_ref[e], fill_start_ref[e], te, lambda pos, n: zero_copy(pos, n).start())
            return carry

        def wait_fill(e, carry):
            _zero_fill_plan(fill_len_ref[e], fill_start_ref[e], te, lambda pos, n: zero_copy(pos, n).wait())
            return carry

        lax.fori_loop(0, N_EXPERTS, start_fill, 0)
        lax.fori_loop(0, N_EXPERTS, wait_fill, 0)

    def wait_token(t, carry):
        for k in range(TOP_K):
            pltpu.make_async_copy(x_ref.at[pl.ds(0, rows)], xs_ref.at[pl.ds(0, rows)], sem).wait()
        return carry

    lax.fori_loop(0, tm, wait_token, 0)


def _dispatch(x_slab, slot_flat, fill_start, fill_len, n_slots, te, rows, pitch, tm=256):
    T = x_slab.shape[0] // pitch
    tm = _tile(T, tm)
    return pl.pallas_call(
        functools.partial(_dispatch_kernel, te=te, rows=rows, pitch=pitch),
        grid_spec=pltpu.PrefetchScalarGridSpec(
            num_scalar_prefetch=2,
            grid=(T // tm,),
            in_specs=[pl.BlockSpec((tm * TOP_K,), lambda i, fs, fl: (i,), memory_space=pltpu.SMEM),
                      pl.BlockSpec((tm * pitch, LANES), lambda i, fs, fl: (i, 0))],
            out_specs=pl.BlockSpec(memory_space=pl.ANY),
            scratch_shapes=[pltpu.VMEM((te // 2 * pitch, LANES), I32),
                            pltpu.SemaphoreType.DMA(()),
                            pltpu.SemaphoreType.DMA(())]),
        out_shape=jax.ShapeDtypeStruct((n_slots * pitch, LANES), I32),
        compiler_params=_params(("arbitrary",)),
        name="moe_dispatch",
    )(fill_start, fill_len, slot_flat, x_slab)


def _ffn_partial(xb, wg, wu, wd):
    hid = _silu(jnp.dot(xb, wg, preferred_element_type=F32)) * jnp.dot(xb, wu, preferred_element_type=F32)
    return jnp.dot(hid.astype(BF16), wd, preferred_element_type=F32)


def _expert_kernel(blk_e_ref, nvalid_ref, x_ref, wg_ref, wu_ref, wd_ref, o_ref, acc_ref, *, rows, pitch):
    b = pl.program_id(0)
    j = pl.program_id(1)
    te = acc_ref.shape[0]

    @pl.when(b < nvalid_ref[0])
    def _():
        @pl.when((b == 0) & (j == 0))
        def _():
            acc_ref[...] = jnp.zeros_like(acc_ref)

        def partial_sum():
            pairs = [_unpack_pair(x_ref[pl.ds(r, te, stride=pitch), :]) for r in range(rows)]
            xb = jnp.concatenate([lo.astype(BF16) for lo, _ in pairs] + [hi.astype(BF16) for _, hi in pairs], axis=1)
            return acc_ref[...] + _ffn_partial(xb, wg_ref[...], wu_ref[...], wd_ref[...])

        last = pl.num_programs(1) - 1

        @pl.when(j < last)
        def _():
            acc_ref[...] = partial_sum()

        @pl.when(j == last)
        def _():
            total = partial_sum()
            _store_slabs(o_ref, lambda c: total[:, c * LANES:(c + 1) * LANES], te, rows, pitch)
            acc_ref[...] = jnp.zeros_like(acc_ref)


def _routed_ffn(xs, blk_e, nvalid, wg, wu, wd, te, rows, pitch, tj=512):
    P = xs.shape[0] // pitch
    _, D, DE = wg.shape
    tj = _tile(DE, tj)
    nj = DE // tj
    nblk = P // te

    def bclamp(b, nv):
        return jnp.minimum(b, nv[0] - 1)

    def jclamp(b, j, nv):
        return jnp.where(b < nv[0], j, nj - 1)

    return pl.pallas_call(
        functools.partial(_expert_kernel, rows=rows, pitch=pitch),
        grid_spec=pltpu.PrefetchScalarGridSpec(
            num_scalar_prefetch=2,
            grid=(nblk, nj),
            in_specs=[pl.BlockSpec((te * pitch, LANES), lambda b, j, be, nv: (bclamp(b, nv), 0)),
                      pl.BlockSpec((None, D, tj), lambda b, j, be, nv: (be[bclamp(b, nv)], 0, jclamp(b, j, nv))),
                      pl.BlockSpec((None, D, tj), lambda b, j, be, nv: (be[bclamp(b, nv)], 0, jclamp(b, j, nv))),
                      pl.BlockSpec((None, tj, D), lambda b, j, be, nv: (be[bclamp(b, nv)], jclamp(b, j, nv), 0))],
            out_specs=pl.BlockSpec((te * pitch, LANES), lambda b, j, be, nv: (bclamp(b, nv), 0)),
            scratch_shapes=[pltpu.VMEM((te, D), F32)]),
        out_shape=jax.ShapeDtypeStruct((P * pitch, LANES), I32),
        compiler_params=_params(("arbitrary", "arbitrary"), V7X_VMEM_EXPERT_BYTES),
        name="routed_experts",
    )(blk_e, nvalid, xs, wg, wu, wd)


def _shared_kernel(x_ref, wg_ref, wu_ref, wd_ref, o_ref):
    @pl.when(pl.program_id(1) == 0)
    def _():
        o_ref[...] = jnp.zeros_like(o_ref)

    o_ref[...] += _ffn_partial(x_ref[...], wg_ref[...], wu_ref[...], wd_ref[...])


def _shared_ffn(xb, wg, wu, wd, tm=512, tj=256):
    T, D = xb.shape
    DS = wg.shape[1]
    tm, tj = _tile(T, tm), _tile(DS, tj)
    return pl.pallas_call(
        _shared_kernel,
        grid=(T // tm, DS // tj),
        in_specs=[pl.BlockSpec((tm, D), lambda i, j: (i, 0)),
                  pl.BlockSpec((D, tj), lambda i, j: (0, j)),
                  pl.BlockSpec((D, tj), lambda i, j: (0, j)),
                  pl.BlockSpec((tj, D), lambda i, j: (j, 0))],
        out_specs=pl.BlockSpec((tm, D), lambda i, j: (i, 0)),
        out_shape=jax.ShapeDtypeStruct((T, D), F32),
        compiler_params=_params(("arbitrary", "arbitrary")),
        name="shared_expert",
    )(xb, wg, wu, wd)


def _combine_kernel(slot_ref, slot_next_ref, gate_ref, x_ref, sh_ref, lw_ref, lb_ref, ys_ref, o1_ref, o2_ref,
                    buf_ref, z_ref, sem, *, rows, pitch, steps1):
    i = pl.program_id(0)
    nsteps = pl.num_programs(0)
    tm, d_model = x_ref.shape
    half = d_model // 2
    cur = i % 2

    def start_all(srefs, buf):
        def body(t, carry):
            for kk in range(TOP_K):
                src = ys_ref.at[pl.ds(pl.multiple_of(srefs[t * TOP_K + kk] * pitch, SUBLANES), rows)]
                dst = buf_ref.at[buf, kk, pl.ds(pl.multiple_of(t * pitch, SUBLANES), rows)]
                pltpu.make_async_copy(src, dst, sem.at[buf]).start(priority=kk % 2)
            return carry
        lax.fori_loop(0, tm, body, 0)

    @pl.when(i == 0)
    def _():
        start_all(slot_ref, 0)

    @pl.when(i + 1 < nsteps)
    def _():
        start_all(slot_next_ref, 1 - cur)

    def wait_body(t, carry):
        for kk in range(TOP_K):
            pltpu.make_async_copy(ys_ref.at[pl.ds(0, rows)], buf_ref.at[cur, 0, pl.ds(0, rows)], sem.at[cur]).wait()
        return carry

    lax.fori_loop(0, tm, wait_body, 0)

    gates = gate_ref[...]
    gk = [gates[:, kk:kk + 1] for kk in range(TOP_K)]
    for r in range(rows):
        lo_acc = hi_acc = None
        for kk in range(TOP_K):
            lo, hi = _unpack_pair(buf_ref[cur, kk, pl.ds(r, tm, stride=pitch), :])
            lo_acc = gk[kk] * lo if lo_acc is None else lo_acc + gk[kk] * lo
            hi_acc = gk[kk] * hi if hi_acc is None else hi_acc + gk[kk] * hi
        for c0, acc in ((r * LANES, lo_acc), (half + r * LANES, hi_acc)):
            cols = slice(c0, c0 + LANES)
            z_ref[:, cols] = ALPHA * x_ref[:, cols] + (acc + sh_ref[:, cols])
    y = _layernorm_rows(z_ref[...], lw_ref[...], lb_ref[...])

    @pl.when(i < steps1)
    def _():
        o1_ref[...] = y

    @pl.when(i >= steps1)
    def _():
        o2_ref[...] = y


def _combine(ys, slot_flat, gate_tk, x1, shared, ln_w, ln_b, n1, rows, pitch, tm=128):
    T, D = x1.shape
    tm = _tile(math.gcd(n1, T - n1), tm)
    nsteps = T // tm
    steps1 = n1 // tm
    return pl.pallas_call(
        functools.partial(_combine_kernel, rows=rows, pitch=pitch, steps1=steps1),
        grid=(nsteps,),
        in_specs=[pl.BlockSpec((tm * TOP_K,), lambda i: (i,), memory_space=pltpu.SMEM),
                  pl.BlockSpec((tm * TOP_K,), lambda i: (jnp.minimum(i + 1, nsteps - 1),), memory_space=pltpu.SMEM),
                  pl.BlockSpec((tm, TOP_K), lambda i: (i, 0)),
                  pl.BlockSpec((tm, D), lambda i: (i, 0)),
                  pl.BlockSpec((tm, D), lambda i: (i, 0)),
                  pl.BlockSpec((1, D), lambda i: (0, 0)),
                  pl.BlockSpec((1, D), lambda i: (0, 0)),
                  pl.BlockSpec(memory_space=pl.ANY)],
        out_specs=[pl.BlockSpec((tm, D), lambda i: (jnp.minimum(i, steps1 - 1), 0)),
                   pl.BlockSpec((tm, D), lambda i: (jnp.maximum(i - steps1, 0), 0))],
        out_shape=[jax.ShapeDtypeStruct((n1, D), F32), jax.ShapeDtypeStruct((T - n1, D), F32)],
        scratch_shapes=[pltpu.VMEM((2, TOP_K, tm * pitch, LANES), I32), pltpu.VMEM((tm, D), F32),
                        pltpu.SemaphoreType.DMA((2,))],
        compiler_params=_params(("arbitrary",)),
        name="moe_combine_layernorm",
    )(slot_flat, slot_flat, gate_tk, x1, shared, ln_w.astype(F32).reshape(1, -1), ln_b.astype(F32).reshape(1, -1),
      ys)


def _seq_tables(groups, blk):
    first, last, pos = [], [], []
    for nseq, slen in groups:
        nb = slen // blk
        for _ in range(nseq):
            for b in range(nb):
                first.append(int(b == 0))
                last.append(int(b == nb - 1))
                pos.append(b)
    return (jnp.asarray(np.array(first, np.int32)), jnp.asarray(np.array(last, np.int32)),
            jnp.asarray(np.array(pos, np.int32)))


def _expert_block_rows(d_model):
    return 512 if d_model >= 4096 else 128


def kernel(x_prompt, x_sample, rel_table, w_in, attn_sink, ret_decay, ret_gn_w, w_att_out, w_ret_out, w_o, ln1_w,
           ln1_b, w_router, router_bias, w_exp_gate, w_exp_up, w_exp_down, w_sh_gate, w_sh_up, w_sh_down, ln2_w,
           ln2_b):
    assert DEPTH == 1
    B1, S1, D = x_prompt.shape
    B2, S2, _ = x_sample.shape
    assert S1 % BLOCK == 0 and S2 % BLOCK == 0 and BLOCK == RET_CHUNK
    xa, xb = x_prompt.reshape(B1 * S1, D), x_sample.reshape(B2 * S2, D)
    n1 = B1 * S1
    T = n1 + B2 * S2
    groups = ((B1, S1), (B2, S2))
    first, last, _ = _seq_tables(groups, BLOCK)
    ret_chunks = math.gcd(8, math.gcd(S1 // RET_CHUNK, S2 // RET_CHUNK))
    rfirst, rlast, rpos = _seq_tables(groups, ret_chunks * RET_CHUNK)

    att_q, att_kv = ATT_HEADS * HEAD_DIM, ATT_KV_HEADS * HEAD_DIM
    ret_qk, ret_v = RET_HEADS * RET_DK, RET_HEADS * RET_DV
    cols = np.concatenate([[0], np.cumsum([att_q, att_kv, att_kv, ret_qk, ret_qk, ret_v, ret_v, D, D])])
    c_aq, c_ak, c_av, c_rq, c_rk, c_rv, c_rg, c_ga, c_gr = (int(c) for c in cols[:-1])
    assert int(cols[-1]) == w_in.shape[2]
    rows = _slab_rows(D)
    pitch = _slab_pitch(rows)

    proj = _matmul(_concat_cast(xa, xb, BF16), w_in[0].astype(BF16), F32)
    att = _attention(proj, first, last, attn_sink[0], rel_table, c_aq, c_ak, c_av)
    ret = _retention(proj, rfirst, rlast, rpos, ret_chunks, ret_decay[0], ret_gn_w[0], max(S1, S2),
                     c_rq, c_rk, c_rv, c_rg)
    merged = _merge(att, ret, proj, w_att_out[0].astype(BF16), w_ret_out[0].astype(BF16), c_ga, c_gr)
    z1 = _residual_matmul(merged, w_o[0].astype(BF16), xa, xb)

    idx, rank, gate, counts, x1, x1_slab, x1b = _ln_router(z1, ln1_w[0], ln1_b[0], w_router[0], router_bias[0],
                                                           rows, pitch)
    te = _expert_block_rows(D)
    cnt = counts[:, 0].astype(I32)
    padded = (cnt + te - 1) // te * te
    pad_end = jnp.cumsum(padded)
    pad_start = pad_end - padded
    nblk = (T * TOP_K) // te + N_EXPERTS
    blk_e = jnp.minimum(jnp.searchsorted(pad_end, jnp.arange(nblk, dtype=I32) * te, side='right'),
                        N_EXPERTS - 1).astype(I32)
    nvalid = (pad_end[-1:] // te).astype(I32)
    slot = _slots(pad_start.astype(I32), idx, rank)
    slot_flat = slot.T.reshape(-1)

    xs = _dispatch(x1_slab, slot_flat, (pad_start + cnt).astype(I32), (padded - cnt).astype(I32), nblk * te, te,
                   rows, pitch)
    ys = _routed_ffn(xs, blk_e, nvalid, w_exp_gate[0].astype(BF16), w_exp_up[0].astype(BF16),
                     w_exp_down[0].astype(BF16), te, rows, pitch)
    shared = _shared_ffn(x1b, w_sh_gate[0].astype(BF16), w_sh_up[0].astype(BF16), w_sh_down[0].astype(BF16))
    y1, y2 = _combine(ys, slot_flat, gate.T, x1, shared, ln2_w[0], ln2_b[0], n1, rows, pitch)
    return (y1.reshape(B1, S1, D), y2.reshape(B2, S2, D))
```

```python
import functools
import math

import numpy as np
import jax
import jax.numpy as jnp
from jax import lax
from jax.experimental import pallas as pl
from jax.experimental.pallas import tpu as pltpu

F32 = jnp.float32
BF16 = jnp.bfloat16
I32 = jnp.int32

HEAD_DIM = 128
ATT_HEADS = 16
ATT_KV_HEADS = 4
WINDOW = 128
BLOCK = 128
REL_BUCKETS = 32
REL_MAX_DIST = 128
RET_HEADS = 8
RET_DK = 128
RET_DV = 256
RET_CHUNK = 128
ROPE_BASE = 10000.0
N_EXPERTS = 128
TOP_K = 8
N_GROUPS = 8
TOPK_GROUPS = 4
ROUTED_SCALE = 2.5
LN_EPS = 1e-5
GN_EPS = 1e-5
DEPTH = 1
ALPHA = (2.0 * DEPTH) ** 0.25

V7X_VMEM_BUDGET_BYTES = 56 * 1024 * 1024
V7X_VMEM_EXPERT_BYTES = 63 * 1024 * 1024
LANES = 128
SUBLANES = 8

NT_DIMS = (((1,), (1,)), ((), ()))
TN_DIMS = (((0,), (0,)), ((), ()))


def _tile(dim, pref):
    t = min(dim, pref)
    assert dim % t == 0, (dim, pref)
    return t


def _params(sem, vmem=V7X_VMEM_BUDGET_BYTES):
    return pltpu.CompilerParams(dimension_semantics=sem, vmem_limit_bytes=vmem)


def _silu(v):
    return v * jax.nn.sigmoid(v)


def _slab_rows(d_model):
    rows = d_model // (2 * LANES)
    assert rows % SUBLANES == 0
    return rows


def _slab_pitch(rows):
    return rows if (rows // SUBLANES) % 2 == 1 else rows + SUBLANES


def _pack_pair(lo, hi):
    lo_bits = lax.bitcast_convert_type(lo.astype(BF16).astype(F32), I32)
    hi_bits = lax.bitcast_convert_type(hi.astype(BF16).astype(F32), I32)
    return hi_bits | lax.shift_right_logical(lo_bits, 16)


def _unpack_pair(w):
    lo = lax.bitcast_convert_type(lax.shift_left(w, 16), F32)
    hi = lax.bitcast_convert_type(w & jnp.int32(-65536), F32)
    return lo, hi


def _store_slabs(slab_ref, col, n_tok, rows, pitch):
    for r in range(rows):
        slab_ref[pl.ds(r, n_tok, stride=pitch), :] = _pack_pair(col(r), col(rows + r))
    for r in range(rows, pitch):
        slab_ref[pl.ds(r, n_tok, stride=pitch), :] = jnp.zeros((n_tok, LANES), I32)


def _matmul_kernel(x_ref, w_ref, o_ref):
    o_ref[...] = jnp.dot(x_ref[...], w_ref[...], preferred_element_type=F32).astype(o_ref.dtype)


def _concat_cast_kernel(xa_ref, xb_ref, o_ref, *, steps1):
    @pl.when(pl.program_id(0) < steps1)
    def _():
        o_ref[...] = xa_ref[...].astype(o_ref.dtype)

    @pl.when(pl.program_id(0) >= steps1)
    def _():
        o_ref[...] = xb_ref[...].astype(o_ref.dtype)


def _concat_cast(xa, xb, dtype, tm=256):
    n1, D = xa.shape
    T = n1 + xb.shape[0]
    tm = _tile(math.gcd(n1, T - n1), tm)
    steps1 = n1 // tm
    return pl.pallas_call(
        functools.partial(_concat_cast_kernel, steps1=steps1),
        grid=(T // tm,),
        in_specs=[pl.BlockSpec((tm, D), lambda i: (jnp.minimum(i, steps1 - 1), 0)),
                  pl.BlockSpec((tm, D), lambda i: (jnp.maximum(i - steps1, 0), 0))],
        out_specs=pl.BlockSpec((tm, D), lambda i: (i, 0)),
        out_shape=jax.ShapeDtypeStruct((T, D), dtype),
        compiler_params=_params(("arbitrary",)),
        name="concat_cast",
    )(xa, xb)


def _matmul(x, w, out_dtype, tm=1024, tn=1024):
    M, K = x.shape
    N = w.shape[1]
    tm, tn = _tile(M, tm), _tile(N, tn)
    return pl.pallas_call(
        _matmul_kernel,
        grid=(M // tm, N // tn),
        in_specs=[pl.BlockSpec((tm, K), lambda i, j: (i, 0)),
                  pl.BlockSpec((K, tn), lambda i, j: (0, j))],
        out_specs=pl.BlockSpec((tm, tn), lambda i, j: (i, j)),
        out_shape=jax.ShapeDtypeStruct((M, N), out_dtype),
        compiler_params=_params(("arbitrary", "arbitrary")),
        name="proj_matmul",
    )(x, w)


def _residual_matmul_kernel(m_ref, w_ref, xa_ref, xb_ref, o_ref, *, steps1):
    mm = jnp.dot(m_ref[...], w_ref[...], preferred_element_type=F32)

    @pl.when(pl.program_id(0) < steps1)
    def _():
        o_ref[...] = ALPHA * xa_ref[...] + mm

    @pl.when(pl.program_id(0) >= steps1)
    def _():
        o_ref[...] = ALPHA * xb_ref[...] + mm


def _residual_matmul(merged, w_o, xa, xb, tm=1024, tn=512):
    T, K = merged.shape
    D = w_o.shape[1]
    n1 = xa.shape[0]
    tm, tn = _tile(math.gcd(n1, T - n1), tm), _tile(D, tn)
    steps1, nt = n1 // tm, D // tn
    return pl.pallas_call(
        functools.partial(_residual_matmul_kernel, steps1=steps1),
        grid=(T // tm, nt),
        in_specs=[pl.BlockSpec((tm, K), lambda i, n: (i, 0)),
                  pl.BlockSpec((K, tn), lambda i, n: (0, n)),
                  pl.BlockSpec((tm, tn), lambda i, n: (jnp.minimum(i, steps1 - 1), jnp.where(i < steps1, n, nt - 1))),
                  pl.BlockSpec((tm, tn), lambda i, n: (jnp.maximum(i - steps1, 0), jnp.where(i < steps1, 0, n)))],
        out_specs=pl.BlockSpec((tm, tn), lambda i, n: (i, n)),
        out_shape=jax.ShapeDtypeStruct((T, D), F32),
        compiler_params=_params(("arbitrary", "arbitrary")),
        name="out_proj_residual",
    )(merged, w_o, xa, xb)


def _attn_kernel(first_ref, last_ref, sink_ref, q_ref, kp_ref, kc_ref, kn_ref, vp_ref, vc_ref, vn_ref,
                 bias_ref, o_ref, *, kv_heads, group, scale):
    n = pl.program_id(0)
    neg_inf = jnp.float32(-jnp.inf)
    pen_p = jnp.where(first_ref[n] == 1, neg_inf, jnp.float32(0.0))
    pen_n = jnp.where(last_ref[n] == 1, neg_inf, jnp.float32(0.0))
    lane = lax.broadcasted_iota(I32, (1, 3 * BLOCK), 1)
    pen = jnp.where(lane < BLOCK, pen_p, jnp.where(lane >= 2 * BLOCK, pen_n, jnp.float32(0.0)))
    for h in range(kv_heads):
        cols = slice(h * HEAD_DIM, (h + 1) * HEAD_DIM)
        kcat = jnp.concatenate([kp_ref[:, cols], kc_ref[:, cols], kn_ref[:, cols]], axis=0).astype(BF16)
        vcat = jnp.concatenate([vp_ref[:, cols], vc_ref[:, cols], vn_ref[:, cols]], axis=0).astype(BF16)
        heads = [h * group + g for g in range(group)]
        qcols = [slice(hd * HEAD_DIM, (hd + 1) * HEAD_DIM) for hd in heads]
        sinks = [sink_ref[hd] for hd in heads]
        s = [lax.dot_general(q_ref[:, c].astype(BF16), kcat, NT_DIMS, preferred_element_type=F32) * scale
             + bias_ref[hd] + pen for hd, c in zip(heads, qcols)]
        m = [jnp.maximum(jnp.max(si, axis=1, keepdims=True), sk) for si, sk in zip(s, sinks)]
        e = [jnp.exp(si - mi) for si, mi in zip(s, m)]
        denom = [jnp.sum(ei, axis=1, keepdims=True) + jnp.exp(sk - mi) for ei, sk, mi in zip(e, sinks, m)]
        p = [(ei * (1.0 / di)).astype(BF16) for ei, di in zip(e, denom)]
        for c, pi in zip(qcols, p):
            o_ref[:, c] = jnp.dot(pi, vcat, preferred_element_type=F32).astype(o_ref.dtype)


def _t5_bucket(rel):
    nb = REL_BUCKETS // 2
    max_exact = nb // 2
    ret = (rel > 0).astype(I32) * nb
    n = jnp.abs(rel)
    nf = jnp.maximum(n, max_exact).astype(F32)
    large = max_exact + (jnp.log(nf / max_exact) / math.log(REL_MAX_DIST / max_exact) * (nb - max_exact)).astype(I32)
    large = jnp.minimum(large, nb - 1)
    return ret + jnp.where(n < max_exact, n, large)


def _attention(proj, first, last, sink, rel_table, col_q, col_k, col_v):
    T = proj.shape[0]
    nb = T // BLOCK
    group = ATT_HEADS // ATT_KV_HEADS
    qw, kw = ATT_HEADS * HEAD_DIM, ATT_KV_HEADS * HEAD_DIM
    assert col_q % qw == 0 and col_k % kw == 0 and col_v % kw == 0
    qo, ko, vo = col_q // qw, col_k // kw, col_v // kw
    rel = jnp.arange(3 * BLOCK)[None, :] - BLOCK - jnp.arange(BLOCK)[:, None]
    onehot = (_t5_bucket(rel)[:, :, None] == jnp.arange(REL_BUCKETS)[None, None, :]).astype(F32)
    bias = jnp.einsum('qkb,bh->hqk', onehot, rel_table.astype(F32), precision=lax.Precision.HIGHEST)
    bias = jnp.where((jnp.abs(rel) <= WINDOW)[None, :, :], bias, -jnp.inf)

    def kv_spec(off, shift):
        def imap(n, first, last):
            return (jnp.clip(n + shift, 0, nb - 1), off)
        return pl.BlockSpec((BLOCK, kw), imap)

    grid_spec = pltpu.PrefetchScalarGridSpec(
        num_scalar_prefetch=2,
        grid=(nb,),
        in_specs=[pl.BlockSpec(memory_space=pltpu.SMEM),
                  pl.BlockSpec((BLOCK, qw), lambda n, first, last: (n, qo)),
                  kv_spec(ko, -1), kv_spec(ko, 0), kv_spec(ko, 1),
                  kv_spec(vo, -1), kv_spec(vo, 0), kv_spec(vo, 1),
                  pl.BlockSpec((ATT_HEADS, BLOCK, 3 * BLOCK), lambda n, first, last: (0, 0, 0))],
        out_specs=pl.BlockSpec((BLOCK, qw), lambda n, first, last: (n, 0)),
    )
    return pl.pallas_call(
        functools.partial(_attn_kernel, kv_heads=ATT_KV_HEADS, group=group, scale=HEAD_DIM ** -0.5),
        grid_spec=grid_spec,
        out_shape=jax.ShapeDtypeStruct((T, qw), BF16),
        compiler_params=_params(("arbitrary",)),
        name="window_attention",
    )(first, last, sink.astype(F32), proj, proj, proj, proj, proj, proj, proj, bias)


def _rope(x, cos2, sin2):
    return x * cos2 + pltpu.roll(x, RET_DK // 2, axis=1) * sin2


def _ret_bwd_kernel(last_ref, posblk_ref, cdec_ref, q_ref, k_ref, v_ref, cos_ref, sin_ref, qb_ref, kb_ref, o_ref,
                    state_ref, *, chunks):
    h = pl.program_id(0)
    n = pl.num_programs(1) - 1 - pl.program_id(1)
    C = RET_CHUNK

    @pl.when(last_ref[n] == 1)
    def _():
        state_ref[...] = jnp.zeros_like(state_ref)

    state = state_ref[...]
    cdec = cdec_ref[1, h]
    for c in range(chunks - 1, -1, -1):
        sl = slice(c * C, (c + 1) * C)
        cos2, sin2 = cos_ref[sl, :], sin_ref[sl, :]
        q = _rope(q_ref[sl, :], cos2, sin2)
        k = _rope(k_ref[sl, :], cos2, sin2) * (RET_DK ** -0.5)
        vb = v_ref[sl, :].astype(BF16)
        o_ref[sl, :] = jnp.dot((q * qb_ref[0]).astype(BF16), state.astype(BF16), preferred_element_type=F32)
        state = state * cdec + lax.dot_general((k * kb_ref[0]).astype(BF16), vb, TN_DIMS,
                                               preferred_element_type=F32)
    state_ref[...] = state


def _ret_fwd_kernel(first_ref, posblk_ref, cdec_ref, q_ref, k_ref, v_ref, g_ref, cos_ref, sin_ref, dmat_ref, qf_ref,
                    kf_ref, ob_ref, gnw_ref, o_ref, state_ref, *, chunks):
    h = pl.program_id(0)
    n = pl.program_id(1)
    C = RET_CHUNK

    @pl.when(first_ref[n] == 1)
    def _():
        state_ref[...] = jnp.zeros_like(state_ref)

    state = state_ref[...]
    cdec = cdec_ref[0, h]
    for c in range(chunks):
        sl = slice(c * C, (c + 1) * C)
        cos2, sin2 = cos_ref[sl, :], sin_ref[sl, :]
        q = _rope(q_ref[sl, :], cos2, sin2)
        k = _rope(k_ref[sl, :], cos2, sin2) * (RET_DK ** -0.5)
        vb = v_ref[sl, :].astype(BF16)
        intra = lax.dot_general(q.astype(BF16), k.astype(BF16), NT_DIMS, preferred_element_type=F32) * dmat_ref[0]
        out = (jnp.dot(intra.astype(BF16), vb, preferred_element_type=F32)
               + jnp.dot((q * qf_ref[0]).astype(BF16), state.astype(BF16), preferred_element_type=F32)
               + ob_ref[sl, :])
        state = state * cdec + lax.dot_general((k * kf_ref[0]).astype(BF16), vb, TN_DIMS,
                                               preferred_element_type=F32)
        mu = jnp.mean(out, axis=1, keepdims=True)
        cen = out - mu
        var = jnp.mean(cen * cen, axis=1, keepdims=True)
        y = cen * lax.rsqrt(var + GN_EPS) * gnw_ref[...]
        o_ref[sl, :] = (_silu(g_ref[sl, :]) * y).astype(o_ref.dtype)
    state_ref[...] = state


def _retention(proj, first, last, posblk, chunks, ret_decay, ret_gn_w, s_max, col_q, col_k, col_v, col_g):
    T = proj.shape[0]
    C = RET_CHUNK
    R = chunks * C
    nc = T // R
    H = RET_HEADS
    qo, ko, vo, go = col_q // RET_DK, col_k // RET_DK, col_v // RET_DV, col_g // RET_DV
    assert col_q % RET_DK == 0 and col_k % RET_DK == 0 and col_v % RET_DV == 0 and col_g % RET_DV == 0

    half = RET_DK // 2
    inv = ROPE_BASE ** (-jnp.arange(half, dtype=F32) * 2.0 / RET_DK)
    ang = jnp.arange(s_max, dtype=F32)[:, None] * inv[None, :]
    cos2 = jnp.concatenate([jnp.cos(ang), jnp.cos(ang)], axis=1)
    sin2 = jnp.concatenate([-jnp.sin(ang), jnp.sin(ang)], axis=1)

    lg = -jnp.exp(ret_decay.astype(F32))
    pos = jnp.arange(C, dtype=F32)
    diff = pos[:, None] - pos[None, :]
    dec_f = jnp.where((diff >= 0)[None], jnp.exp(jnp.maximum(diff, 0.0)[None] * lg[0][:, None, None]), 0.0)
    dec_b = jnp.where((diff < 0)[None], jnp.exp(jnp.maximum(-diff, 0.0)[None] * lg[1][:, None, None]), 0.0)
    dmat = dec_f + dec_b

    def rows(tab):
        return jnp.broadcast_to(tab[:, :, None], (H, C, RET_DK))

    qf = rows(jnp.exp((pos[None, :] + 1.0) * lg[0][:, None]))
    kf = rows(jnp.exp((C - 1.0 - pos)[None, :] * lg[0][:, None]))
    qb = rows(jnp.exp((C - pos)[None, :] * lg[1][:, None]))
    kb = rows(jnp.exp(pos[None, :] * lg[1][:, None]))
    cdec = jnp.exp(C * lg)

    smem = pl.BlockSpec(memory_space=pltpu.SMEM)
    rev = lambda c: nc - 1 - c
    bwd_specs = [
        smem,
        pl.BlockSpec((R, RET_DK), lambda h, c, last, posb: (rev(c), qo + h)),
        pl.BlockSpec((R, RET_DK), lambda h, c, last, posb: (rev(c), ko + h)),
        pl.BlockSpec((R, RET_DV), lambda h, c, last, posb: (rev(c), vo + h)),
        pl.BlockSpec((R, RET_DK), lambda h, c, last, posb: (posb[rev(c)], 0)),
        pl.BlockSpec((R, RET_DK), lambda h, c, last, posb: (posb[rev(c)], 0)),
        pl.BlockSpec((1, C, RET_DK), lambda h, c, last, posb: (h, 0, 0)),
        pl.BlockSpec((1, C, RET_DK), lambda h, c, last, posb: (h, 0, 0)),
    ]
    out_b = pl.pallas_call(
        functools.partial(_ret_bwd_kernel, chunks=chunks),
        grid_spec=pltpu.PrefetchScalarGridSpec(
            num_scalar_prefetch=2, grid=(H, nc), in_specs=bwd_specs,
            out_specs=pl.BlockSpec((R, RET_DV), lambda h, c, last, posb: (rev(c), h)),
            scratch_shapes=[pltpu.VMEM((RET_DK, RET_DV), F32)]),
        out_shape=jax.ShapeDtypeStruct((T, H * RET_DV), F32),
        compiler_params=_params(("arbitrary", "arbitrary")),
        name="retention_backward",
    )(last, posblk, cdec, proj, proj, proj, cos2, sin2, qb, kb)

    fwd_specs = [
        smem,
        pl.BlockSpec((R, RET_DK), lambda h, n, first, posb: (n, qo + h)),
        pl.BlockSpec((R, RET_DK), lambda h, n, first, posb: (n, ko + h)),
        pl.BlockSpec((R, RET_DV), lambda h, n, first, posb: (n, vo + h)),
        pl.BlockSpec((R, RET_DV), lambda h, n, first, posb: (n, go + h)),
        pl.BlockSpec((R, RET_DK), lambda h, n, first, posb: (posb[n], 0)),
        pl.BlockSpec((R, RET_DK), lambda h, n, first, posb: (posb[n], 0)),
        pl.BlockSpec((1, C, C), lambda h, n, first, posb: (h, 0, 0)),
        pl.BlockSpec((1, C, RET_DK), lambda h, n, first, posb: (h, 0, 0)),
        pl.BlockSpec((1, C, RET_DK), lambda h, n, first, posb: (h, 0, 0)),
        pl.BlockSpec((R, RET_DV), lambda h, n, first, posb: (n, h)),
        pl.BlockSpec((1, RET_DV), lambda h, n, first, posb: (0, h)),
    ]
    return pl.pallas_call(
        functools.partial(_ret_fwd_kernel, chunks=chunks),
        grid_spec=pltpu.PrefetchScalarGridSpec(
            num_scalar_prefetch=2, grid=(H, nc), in_specs=fwd_specs,
            out_specs=pl.BlockSpec((R, RET_DV), lambda h, n, first, posb: (n, h)),
            scratch_shapes=[pltpu.VMEM((RET_DK, RET_DV), F32)]),
        out_shape=jax.ShapeDtypeStruct((T, H * RET_DV), BF16),
        compiler_params=_params(("arbitrary", "arbitrary")),
        name="retention_forward",
    )(first, posblk, cdec, proj, proj, proj, proj, cos2, sin2, dmat, qf, kf, out_b,
      ret_gn_w.astype(F32).reshape(1, -1))


def _merge_kernel(att_ref, wa_ref, ret_ref, wr_ref, ga_ref, gr_ref, o_ref):
    a = jnp.dot(att_ref[...], wa_ref[...], preferred_element_type=F32)
    r = jnp.dot(ret_ref[...], wr_ref[...], preferred_element_type=F32)
    o_ref[...] = (jax.nn.sigmoid(ga_ref[...]) * a + jax.nn.sigmoid(gr_ref[...]) * r).astype(o_ref.dtype)


def _merge(att, ret, proj, w_att_out, w_ret_out, col_ga, col_gr, tm=1024, tn=512):
    T = att.shape[0]
    D = w_att_out.shape[1]
    tm, tn = _tile(T, tm), _tile(D, tn)
    assert col_ga % tn == 0 and col_gr % tn == 0
    ao, ro = col_ga // tn, col_gr // tn
    return pl.pallas_call(
        _merge_kernel,
        grid=(T // tm, D // tn),
        in_specs=[pl.BlockSpec((tm, att.shape[1]), lambda i, j: (i, 0)),
                  pl.BlockSpec((w_att_out.shape[0], tn), lambda i, j: (0, j)),
                  pl.BlockSpec((tm, ret.shape[1]), lambda i, j: (i, 0)),
                  pl.BlockSpec((w_ret_out.shape[0], tn), lambda i, j: (0, j)),
                  pl.BlockSpec((tm, tn), lambda i, j: (i, ao + j)),
                  pl.BlockSpec((tm, tn), lambda i, j: (i, ro + j))],
        out_specs=pl.BlockSpec((tm, tn), lambda i, j: (i, j)),
        out_shape=jax.ShapeDtypeStruct((T, D), BF16),
        compiler_params=_params(("arbitrary", "arbitrary")),
        name="branch_merge",
    )(att, w_att_out, ret, w_ret_out, proj, proj)


def _layernorm_rows(z, w, b):
    mu = jnp.mean(z, axis=1, keepdims=True)
    cen = z - mu
    var = jnp.mean(cen * cen, axis=1, keepdims=True)
    return cen * lax.rsqrt(var + LN_EPS) * w + b


def _router_kernel(z_ref, lw_ref, lb_ref, wh_ref, wl_ref, bias_ref, idx_ref, rank_ref, gate_ref, cnt_ref, x1_ref,
                   slab_ref, xb_ref, carry_ref, *, rows, pitch):
    i = pl.program_id(0)
    tm = z_ref.shape[0]
    E = N_EXPERTS
    gsz = E // N_GROUPS
    neg_inf = jnp.float32(-jnp.inf)

    @pl.when(i == 0)
    def _():
        carry_ref[...] = jnp.zeros_like(carry_ref)

    x1_ref[...] = _layernorm_rows(z_ref[...], lw_ref[...], lb_ref[...])
    _store_slabs(slab_ref, lambda c: x1_ref[:, c * LANES:(c + 1) * LANES], tm, rows, pitch)
    x = x1_ref[...]
    xh = x.astype(BF16)
    xb_ref[...] = xh
    xl = (x - xh.astype(F32)).astype(BF16)
    logits = (lax.dot_general(wh_ref[...], xh, NT_DIMS, preferred_element_type=F32)
              + lax.dot_general(wh_ref[...], xl, NT_DIMS, preferred_element_type=F32)
              + lax.dot_general(wl_ref[...], xh, NT_DIMS, preferred_element_type=F32))
    scores = jax.nn.sigmoid(logits)
    biased = scores + bias_ref[...]

    row = lax.broadcasted_iota(I32, (gsz, tm), 0)
    gscore = []
    for g in range(N_GROUPS):
        blk = biased[g * gsz:(g + 1) * gsz, :]
        m1 = jnp.max(blk, axis=0, keepdims=True)
        first = jnp.min(jnp.where(blk == m1, row, gsz), axis=0, keepdims=True)
        m2 = jnp.max(jnp.where(row == first, neg_inf, blk), axis=0, keepdims=True)
        gscore.append(m1 + m2)
    gid = lax.broadcasted_iota(I32, (E, tm), 0) // gsz
    keep = jnp.zeros((E, tm), I32)
    for g in range(N_GROUPS):
        beaten = jnp.zeros((1, tm), I32)
        for o in range(N_GROUPS):
            if o == g:
                continue
            wins = (gscore[o] >= gscore[g]) if o < g else (gscore[o] > gscore[g])
            beaten = beaten + wins.astype(I32)
        keep = jnp.where(gid == g, (beaten < TOPK_GROUPS).astype(I32), keep)
    masked = jnp.where(keep > 0, biased, neg_inf)

    eidx = lax.broadcasted_iota(I32, (E, tm), 0)
    sel = jnp.zeros((E, tm), jnp.bool_)
    picks, weights = [], []
    for _ in range(TOP_K):
        m = jnp.max(masked, axis=0, keepdims=True)
        pick = jnp.min(jnp.where(masked == m, eidx, E), axis=0, keepdims=True)
        onehot = eidx == pick
        weights.append(jnp.sum(jnp.where(onehot, scores, 0.0), axis=0, keepdims=True))
        picks.append(pick)
        masked = jnp.where(onehot, neg_inf, masked)
        sel = sel | onehot
    wsum = weights[0]
    for w in weights[1:]:
        wsum = wsum + w

    self32 = sel.astype(F32)
    tri = (lax.broadcasted_iota(I32, (tm, tm), 0) < lax.broadcasted_iota(I32, (tm, tm), 1)).astype(F32).astype(BF16)
    prefix = jnp.dot(self32.astype(BF16), tri, preferred_element_type=F32) + carry_ref[...]
    for kk in range(TOP_K):
        onehot = eidx == picks[kk]
        idx_ref[kk:kk + 1, :] = picks[kk]
        rank_ref[kk:kk + 1, :] = jnp.sum(jnp.where(onehot, prefix, 0.0), axis=0, keepdims=True).astype(I32)
        gate_ref[kk:kk + 1, :] = weights[kk] / wsum * ROUTED_SCALE
    carry_ref[...] = carry_ref[...] + jnp.sum(self32, axis=1, keepdims=True)
    cnt_ref[...] = carry_ref[...]


def _ln_router(z, ln_w, ln_b, w_router, router_bias, rows, pitch, tm=256):
    T, D = z.shape
    E = N_EXPERTS
    tm = _tile(T, tm)
    wt = w_router.astype(F32).T
    wh = wt.astype(BF16)
    wl = (wt - wh.astype(F32)).astype(BF16)
    return pl.pallas_call(
        functools.partial(_router_kernel, rows=rows, pitch=pitch),
        grid=(T // tm,),
        in_specs=[pl.BlockSpec((tm, D), lambda i: (i, 0)),
                  pl.BlockSpec((1, D), lambda i: (0, 0)),
                  pl.BlockSpec((1, D), lambda i: (0, 0)),
                  pl.BlockSpec((E, D), lambda i: (0, 0)),
                  pl.BlockSpec((E, D), lambda i: (0, 0)),
                  pl.BlockSpec((E, 1), lambda i: (0, 0))],
        out_specs=[pl.BlockSpec((TOP_K, tm), lambda i: (0, i)),
                   pl.BlockSpec((TOP_K, tm), lambda i: (0, i)),
                   pl.BlockSpec((TOP_K, tm), lambda i: (0, i)),
                   pl.BlockSpec((E, 1), lambda i: (0, 0)),
                   pl.BlockSpec((tm, D), lambda i: (i, 0)),
                   pl.BlockSpec((tm * pitch, LANES), lambda i: (i, 0)),
                   pl.BlockSpec((tm, D), lambda i: (i, 0))],
        out_shape=[jax.ShapeDtypeStruct((TOP_K, T), I32),
                   jax.ShapeDtypeStruct((TOP_K, T), I32),
                   jax.ShapeDtypeStruct((TOP_K, T), F32),
                   jax.ShapeDtypeStruct((E, 1), F32),
                   jax.ShapeDtypeStruct((T, D), F32),
                   jax.ShapeDtypeStruct((T * pitch, LANES), I32),
                   jax.ShapeDtypeStruct((T, D), BF16)],
        scratch_shapes=[pltpu.VMEM((E, 1), F32)],
        compiler_params=_params(("arbitrary",)),
        name="layernorm_router",
    )(z, ln_w.astype(F32).reshape(1, -1), ln_b.astype(F32).reshape(1, -1), wh, wl,
      router_bias.astype(F32).reshape(E, 1))


def _slot_kernel(start_ref, idx_ref, rank_ref, slot_ref):
    idx = idx_ref[...]
    acc = rank_ref[...]
    for e in range(N_EXPERTS):
        acc = acc + jnp.where(idx == e, start_ref[e], 0)
    slot_ref[...] = acc


def _slots(pad_start, idx, rank, tm=2048):
    T = idx.shape[1]
    tm = _tile(T, tm)
    return pl.pallas_call(
        _slot_kernel,
        grid=(T // tm,),
        in_specs=[pl.BlockSpec(memory_space=pltpu.SMEM),
                  pl.BlockSpec((TOP_K, tm), lambda i: (0, i)),
                  pl.BlockSpec((TOP_K, tm), lambda i: (0, i))],
        out_specs=pl.BlockSpec((TOP_K, tm), lambda i: (0, i)),
        out_shape=jax.ShapeDtypeStruct((TOP_K, T), I32),
        compiler_params=_params(("arbitrary",)),
        name="slot_index",
    )(pad_start, idx, rank)


def _zero_fill_plan(npad, start, te, fn):
    pos = start
    bit = te // 2
    while bit >= 1:
        take = (npad & bit) != 0

        @pl.when(take)
        def _(pos=pos, bit=bit):
            fn(pos, bit)

        pos = pos + jnp.where(take, bit, 0)
        bit //= 2


def _dispatch_kernel(fill_start_ref, fill_len_ref, slot_ref, x_ref, xs_ref, zero_ref, sem, zsem, *, te, rows, pitch):
    i = pl.program_id(0)
    tm = x_ref.shape[0] // pitch

    def start_token(t, carry):
        src = x_ref.at[pl.ds(pl.multiple_of(t * pitch, SUBLANES), rows)]
        for k in range(TOP_K):
            dst = xs_ref.at[pl.ds(pl.multiple_of(slot_ref[t * TOP_K + k] * pitch, SUBLANES), rows)]
            pltpu.make_async_copy(src, dst, sem).start(priority=k % 2)
        return carry

    lax.fori_loop(0, tm, start_token, 0)

    @pl.when(i == 0)
    def _():
        zero_ref[...] = jnp.zeros_like(zero_ref)

        def zero_copy(pos, n):
            return pltpu.make_async_copy(zero_ref.at[pl.ds(0, n * pitch)],
                                         xs_ref.at[pl.ds(pl.multiple_of(pos * pitch, SUBLANES), n * pitch)], zsem)

        def start_fill(e, carry):
            _zero_fill_plan(fill_len_ref[e], fill_start_ref[e], te, lambda pos, n: zero_copy(pos, n).start())
            return carry

        def wait_fill(e, carry):
            _zero_fill_plan(fill_len_ref[e], fill_start_ref[e], te, lambda pos, n: zero_copy(pos, n).wait())
            return carry

        lax.fori_loop(0, N_EXPERTS, start_fill, 0)
        lax.fori_loop(0, N_EXPERTS, wait_fill, 0)

    def wait_token(t, carry):
        for k in range(TOP_K):
            pltpu.make_async_copy(x_ref.at[pl.ds(0, rows)], xs_ref.at[pl.ds(0, rows)], sem).wait()
        return carry

    lax.fori_loop(0, tm, wait_token, 0)


def _dispatch(x_slab, slot_flat, fill_start, fill_len, n_slots, te, rows, pitch, tm=256):
    T = x_slab.shape[0] // pitch
    tm = _tile(T, tm)
    return pl.pallas_call(
        functools.partial(_dispatch_kernel, te=te, rows=rows, pitch=pitch),
        grid_spec=pltpu.PrefetchScalarGridSpec(
            num_scalar_prefetch=2,
            grid=(T // tm,),
            in_specs=[pl.BlockSpec((tm * TOP_K,), lambda i, fs, fl: (i,), memory_space=pltpu.SMEM),
                      pl.BlockSpec((tm * pitch, LANES), lambda i, fs, fl: (i, 0))],
            out_specs=pl.BlockSpec(memory_space=pl.ANY),
            scratch_shapes=[pltpu.VMEM((te // 2 * pitch, LANES), I32),
                            pltpu.SemaphoreType.DMA(()),
                            pltpu.SemaphoreType.DMA(())]),
        out_shape=jax.ShapeDtypeStruct((n_slots * pitch, LANES), I32),
        compiler_params=_params(("arbitrary",)),
        name="moe_dispatch",
    )(fill_start, fill_len, slot_flat, x_slab)


def _ffn_partial(xb, wg, wu, wd):
    hid = _silu(jnp.dot(xb, wg, preferred_element_type=F32)) * jnp.dot(xb, wu, preferred_element_type=F32)
    return jnp.dot(hid.astype(BF16), wd, preferred_element_type=F32)


def _expert_kernel(blk_e_ref, nvalid_ref, x_ref, wg_ref, wu_ref, wd_ref, o_ref, acc_ref, *, rows, pitch):
    b = pl.program_id(0)
    j = pl.program_id(1)
    te = acc_ref.shape[0]

    @pl.when(b < nvalid_ref[0])
    def _():
        @pl.when((b == 0) & (j == 0))
        def _():
            acc_ref[...] = jnp.zeros_like(acc_ref)

        def partial_sum():
            pairs = [_unpack_pair(x_ref[pl.ds(r, te, stride=pitch), :]) for r in range(rows)]
            xb = jnp.concatenate([lo.astype(BF16) for lo, _ in pairs] + [hi.astype(BF16) for _, hi in pairs], axis=1)
            return acc_ref[...] + _ffn_partial(xb, wg_ref[...], wu_ref[...], wd_ref[...])

        last = pl.num_programs(1) - 1

        @pl.when(j < last)
        def _():
            acc_ref[...] = partial_sum()

        @pl.when(j == last)
        def _():
            total = partial_sum()
            _store_slabs(o_ref, lambda c: total[:, c * LANES:(c + 1) * LANES], te, rows, rows)
            acc_ref[...] = jnp.zeros_like(acc_ref)


def _routed_ffn(xs, blk_e, nvalid, wg, wu, wd, te, rows, pitch, tj=512):
    P = xs.shape[0] // pitch
    _, D, DE = wg.shape
    tj = _tile(DE, tj)
    nj = DE // tj
    nblk = P // te

    def bclamp(b, nv):
        return jnp.minimum(b, nv[0] - 1)

    def jclamp(b, j, nv):
        return jnp.where(b < nv[0], j, nj - 1)

    return pl.pallas_call(
        functools.partial(_expert_kernel, rows=rows, pitch=pitch),
        grid_spec=pltpu.PrefetchScalarGridSpec(
            num_scalar_prefetch=2,
            grid=(nblk, nj),
            in_specs=[pl.BlockSpec((te * pitch, LANES), lambda b, j, be, nv: (bclamp(b, nv), 0)),
                      pl.BlockSpec((None, D, tj), lambda b, j, be, nv: (be[bclamp(b, nv)], 0, jclamp(b, j, nv))),
                      pl.BlockSpec((None, D, tj), lambda b, j, be, nv: (be[bclamp(b, nv)], 0, jclamp(b, j, nv))),
                      pl.BlockSpec((None, tj, D), lambda b, j, be, nv: (be[bclamp(b, nv)], jclamp(b, j, nv), 0))],
            out_specs=pl.BlockSpec((te * rows, LANES), lambda b, j, be, nv: (bclamp(b, nv), 0)),
            scratch_shapes=[pltpu.VMEM((te, D), F32)]),
        out_shape=jax.ShapeDtypeStruct((P * rows, LANES), I32),
        compiler_params=_params(("arbitrary", "arbitrary"), V7X_VMEM_EXPERT_BYTES),
        name="routed_experts",
    )(blk_e, nvalid, xs, wg, wu, wd)


def _shared_kernel(x_ref, wg_ref, wu_ref, wd_ref, o_ref):
    @pl.when(pl.program_id(1) == 0)
    def _():
        o_ref[...] = jnp.zeros_like(o_ref)

    o_ref[...] += _ffn_partial(x_ref[...], wg_ref[...], wu_ref[...], wd_ref[...])


def _shared_ffn(xb, wg, wu, wd, tm=512, tj=512):
    T, D = xb.shape
    DS = wg.shape[1]
    tm, tj = _tile(T, tm), _tile(DS, tj)
    return pl.pallas_call(
        _shared_kernel,
        grid=(T // tm, DS // tj),
        in_specs=[pl.BlockSpec((tm, D), lambda i, j: (i, 0)),
                  pl.BlockSpec((D, tj), lambda i, j: (0, j)),
                  pl.BlockSpec((D, tj), lambda i, j: (0, j)),
                  pl.BlockSpec((tj, D), lambda i, j: (j, 0))],
        out_specs=pl.BlockSpec((tm, D), lambda i, j: (i, 0)),
        out_shape=jax.ShapeDtypeStruct((T, D), F32),
        compiler_params=_params(("arbitrary", "arbitrary")),
        name="shared_expert",
    )(xb, wg, wu, wd)


def _combine_kernel(slot_ref, slot_next_ref, gate_ref, x_ref, sh_ref, lw_ref, lb_ref, ys_ref, o1_ref, o2_ref,
                    buf_ref, z_ref, sem, *, rows, pitch, steps1):
    i = pl.program_id(0)
    nsteps = pl.num_programs(0)
    tm, d_model = x_ref.shape
    half = d_model // 2
    cur = i % 2

    def start_all(srefs, buf):
        def body(t, carry):
            for kk in range(TOP_K):
                src = ys_ref.at[pl.ds(pl.multiple_of(srefs[t * TOP_K + kk] * rows, SUBLANES), rows)]
                dst = buf_ref.at[buf, kk, pl.ds(pl.multiple_of(t * pitch, SUBLANES), rows)]
                pltpu.make_async_copy(src, dst, sem.at[buf]).start(priority=kk % 2)
            return carry
        lax.fori_loop(0, tm, body, 0)

    @pl.when(i == 0)
    def _():
        start_all(slot_ref, 0)

    @pl.when(i + 1 < nsteps)
    def _():
        start_all(slot_next_ref, 1 - cur)

    def wait_body(t, carry):
        for kk in range(TOP_K):
            pltpu.make_async_copy(ys_ref.at[pl.ds(0, rows)], buf_ref.at[cur, 0, pl.ds(0, rows)], sem.at[cur]).wait()
        return carry

    lax.fori_loop(0, tm, wait_body, 0)

    gates = gate_ref[...]
    gk = [gates[:, kk:kk + 1] for kk in range(TOP_K)]
    for r in range(rows):
        lo_acc = hi_acc = None
        for kk in range(TOP_K):
            lo, hi = _unpack_pair(buf_ref[cur, kk, pl.ds(r, tm, stride=pitch), :])
            lo_acc = gk[kk] * lo if lo_acc is None else lo_acc + gk[kk] * lo
            hi_acc = gk[kk] * hi if hi_acc is None else hi_acc + gk[kk] * hi
        for c0, acc in ((r * LANES, lo_acc), (half + r * LANES, hi_acc)):
            cols = slice(c0, c0 + LANES)
            z_ref[:, cols] = ALPHA * x_ref[:, cols] + (acc + sh_ref[:, cols])
    y = _layernorm_rows(z_ref[...], lw_ref[...], lb_ref[...])

    @pl.when(i < steps1)
    def _():
        o1_ref[...] = y

    @pl.when(i >= steps1)
    def _():
        o2_ref[...] = y


def _combine(ys, slot_flat, gate_tk, x1, shared, ln_w, ln_b, n1, rows, pitch, tm=128):
    T, D = x1.shape
    tm = _tile(math.gcd(n1, T - n1), tm)
    nsteps = T // tm
    steps1 = n1 // tm
    return pl.pallas_call(
        functools.partial(_combine_kernel, rows=rows, pitch=pitch, steps1=steps1),
        grid=(nsteps,),
        in_specs=[pl.BlockSpec((tm * TOP_K,), lambda i: (i,), memory_space=pltpu.SMEM),
                  pl.BlockSpec((tm * TOP_K,), lambda i: (jnp.minimum(i + 1, nsteps - 1),), memory_space=pltpu.SMEM),
                  pl.BlockSpec((tm, TOP_K), lambda i: (i, 0)),
                  pl.BlockSpec((tm, D), lambda i: (i, 0)),
                  pl.BlockSpec((tm, D), lambda i: (i, 0)),
                  pl.BlockSpec((1, D), lambda i: (0, 0)),
                  pl.BlockSpec((1, D), lambda i: (0, 0)),
                  pl.BlockSpec(memory_space=pl.ANY)],
        out_specs=[pl.BlockSpec((tm, D), lambda i: (jnp.minimum(i, steps1 - 1), 0)),
                   pl.BlockSpec((tm, D), lambda i: (jnp.maximum(i - steps1, 0), 0))],
        out_shape=[jax.ShapeDtypeStruct((n1, D), F32), jax.ShapeDtypeStruct((T - n1, D), F32)],
        scratch_shapes=[pltpu.VMEM((2, TOP_K, tm * pitch, LANES), I32), pltpu.VMEM((tm, D), F32),
                        pltpu.SemaphoreType.DMA((2,))],
        compiler_params=_params(("arbitrary",)),
        name="moe_combine_layernorm",
    )(slot_flat, slot_flat, gate_tk, x1, shared, ln_w.astype(F32).reshape(1, -1), ln_b.astype(F32).reshape(1, -1),
      ys)


def _seq_tables(groups, blk):
    first, last, pos = [], [], []
    for nseq, slen in groups:
        nb = slen // blk
        for _ in range(nseq):
            for b in range(nb):
                first.append(int(b == 0))
                last.append(int(b == nb - 1))
                pos.append(b)
    return (jnp.asarray(np.array(first, np.int32)), jnp.asarray(np.array(last, np.int32)),
            jnp.asarray(np.array(pos, np.int32)))


def _expert_block_rows(d_model):
    return 512 if d_model >= 4096 else 128


def kernel(x_prompt, x_sample, rel_table, w_in, attn_sink, ret_decay, ret_gn_w, w_att_out, w_ret_out, w_o, ln1_w,
           ln1_b, w_router, router_bias, w_exp_gate, w_exp_up, w_exp_down, w_sh_gate, w_sh_up, w_sh_down, ln2_w,
           ln2_b):
    assert DEPTH == 1
    B1, S1, D = x_prompt.shape
    B2, S2, _ = x_sample.shape
    assert S1 % BLOCK == 0 and S2 % BLOCK == 0 and BLOCK == RET_CHUNK
    xa, xb = x_prompt.reshape(B1 * S1, D), x_sample.reshape(B2 * S2, D)
    n1 = B1 * S1
    T = n1 + B2 * S2
    groups = ((B1, S1), (B2, S2))
    first, last, _ = _seq_tables(groups, BLOCK)
    ret_chunks = math.gcd(8, math.gcd(S1 // RET_CHUNK, S2 // RET_CHUNK))
    rfirst, rlast, rpos = _seq_tables(groups, ret_chunks * RET_CHUNK)

    att_q, att_kv = ATT_HEADS * HEAD_DIM, ATT_KV_HEADS * HEAD_DIM
    ret_qk, ret_v = RET_HEADS * RET_DK, RET_HEADS * RET_DV
    cols = np.concatenate([[0], np.cumsum([att_q, att_kv, att_kv, ret_qk, ret_qk, ret_v, ret_v, D, D])])
    c_aq, c_ak, c_av, c_rq, c_rk, c_rv, c_rg, c_ga, c_gr = (int(c) for c in cols[:-1])
    assert int(cols[-1]) == w_in.shape[2]
    rows = _slab_rows(D)
    pitch = _slab_pitch(rows)

    proj = _matmul(_concat_cast(xa, xb, BF16), w_in[0].astype(BF16), F32)
    att = _attention(proj, first, last, attn_sink[0], rel_table, c_aq, c_ak, c_av)
    ret = _retention(proj, rfirst, rlast, rpos, ret_chunks, ret_decay[0], ret_gn_w[0], max(S1, S2),
                     c_rq, c_rk, c_rv, c_rg)
    merged = _merge(att, ret, proj, w_att_out[0].astype(BF16), w_ret_out[0].astype(BF16), c_ga, c_gr)
    z1 = _residual_matmul(merged, w_o[0].astype(BF16), xa, xb)

    idx, rank, gate, counts, x1, x1_slab, x1b = _ln_router(z1, ln1_w[0], ln1_b[0], w_router[0], router_bias[0],
                                                           rows, pitch)
    te = _expert_block_rows(D)
    cnt = counts[:, 0].astype(I32)
    padded = (cnt + te - 1) // te * te
    pad_end = jnp.cumsum(padded)
    pad_start = pad_end - padded
    nblk = (T * TOP_K) // te + N_EXPERTS
    blk_e = jnp.minimum(jnp.searchsorted(pad_end, jnp.arange(nblk, dtype=I32) * te, side='right'),
                        N_EXPERTS - 1).astype(I32)
    nvalid = (pad_end[-1:] // te).astype(I32)
    slot = _slots(pad_start.astype(I32), idx, rank)
    slot_flat = slot.T.reshape(-1)

    xs = _dispatch(x1_slab, slot_flat, (pad_start + cnt).astype(I32), (padded - cnt).astype(I32), nblk * te, te,
                   rows, pitch)
    ys = _routed_ffn(xs, blk_e, nvalid, w_exp_gate[0].astype(BF16), w_exp_up[0].astype(BF16),
                     w_exp_down[0].astype(BF16), te, rows, pitch)
    shared = _shared_ffn(x1b, w_sh_gate[0].astype(BF16), w_sh_up[0].astype(BF16), w_sh_down[0].astype(BF16))
    y1, y2 = _combine(ys, slot_flat, gate.T, x1, shared, ln2_w[0], ln2_b[0], n1, rows, pitch)
    return (y1.reshape(B1, S1, D), y2.reshape(B2, S2, D))
```

```python
import functools
import math

import numpy as np
import jax
import jax.numpy as jnp
from jax import lax
from jax.experimental import pallas as pl
from jax.experimental.pallas import tpu as pltpu

F32 = jnp.float32
BF16 = jnp.bfloat16
I32 = jnp.int32

HEAD_DIM = 128
ATT_HEADS = 16
ATT_KV_HEADS = 4
WINDOW = 128
BLOCK = 128
REL_BUCKETS = 32
REL_MAX_DIST = 128
RET_HEADS = 8
RET_DK = 128
RET_DV = 256
RET_CHUNK = 128
ROPE_BASE = 10000.0
N_EXPERTS = 128
TOP_K = 8
N_GROUPS = 8
TOPK_GROUPS = 4
ROUTED_SCALE = 2.5
LN_EPS = 1e-5
GN_EPS = 1e-5
DEPTH = 1
ALPHA = (2.0 * DEPTH) ** 0.25

V7X_VMEM_BUDGET_BYTES = 56 * 1024 * 1024
V7X_VMEM_EXPERT_BYTES = 63 * 1024 * 1024
LANES = 128
SUBLANES = 8

NT_DIMS = (((1,), (1,)), ((), ()))
TN_DIMS = (((0,), (0,)), ((), ()))


def _tile(dim, pref):
    t = min(dim, pref)
    assert dim % t == 0, (dim, pref)
    return t


def _params(sem, vmem=V7X_VMEM_BUDGET_BYTES):
    return pltpu.CompilerParams(dimension_semantics=sem, vmem_limit_bytes=vmem)


def _silu(v):
    return v * jax.nn.sigmoid(v)


def _slab_rows(d_model):
    rows = d_model // (2 * LANES)
    assert rows % SUBLANES == 0
    return rows


def _slab_pitch(rows):
    return rows if (rows // SUBLANES) % 2 == 1 else rows + SUBLANES


def _pack_pair(lo, hi):
    lo_bits = lax.bitcast_convert_type(lo.astype(BF16).astype(F32), I32)
    hi_bits = lax.bitcast_convert_type(hi.astype(BF16).astype(F32), I32)
    return hi_bits | lax.shift_right_logical(lo_bits, 16)


def _unpack_pair(w):
    lo = lax.bitcast_convert_type(lax.shift_left(w, 16), F32)
    hi = lax.bitcast_convert_type(w & jnp.int32(-65536), F32)
    return lo, hi


def _store_slabs(slab_ref, col, n_tok, rows, pitch):
    for r in range(rows):
        slab_ref[pl.ds(r, n_tok, stride=pitch), :] = _pack_pair(col(r), col(rows + r))
    for r in range(rows, pitch):
        slab_ref[pl.ds(r, n_tok, stride=pitch), :] = jnp.zeros((n_tok, LANES), I32)


def _matmul_kernel(x_ref, w_ref, o_ref):
    o_ref[...] = jnp.dot(x_ref[...], w_ref[...], preferred_element_type=F32).astype(o_ref.dtype)


def _concat_cast_kernel(xa_ref, xb_ref, o_ref, *, steps1):
    @pl.when(pl.program_id(0) < steps1)
    def _():
        o_ref[...] = xa_ref[...].astype(o_ref.dtype)

    @pl.when(pl.program_id(0) >= steps1)
    def _():
        o_ref[...] = xb_ref[...].astype(o_ref.dtype)


def _concat_cast(xa, xb, dtype, tm=256):
    n1, D = xa.shape
    T = n1 + xb.shape[0]
    tm = _tile(math.gcd(n1, T - n1), tm)
    steps1 = n1 // tm
    return pl.pallas_call(
        functools.partial(_concat_cast_kernel, steps1=steps1),
        grid=(T // tm,),
        in_specs=[pl.BlockSpec((tm, D), lambda i: (jnp.minimum(i, steps1 - 1), 0)),
                  pl.BlockSpec((tm, D), lambda i: (jnp.maximum(i - steps1, 0), 0))],
        out_specs=pl.BlockSpec((tm, D), lambda i: (i, 0)),
        out_shape=jax.ShapeDtypeStruct((T, D), dtype),
        compiler_params=_params(("arbitrary",)),
        name="concat_cast",
    )(xa, xb)


def _matmul(x, w, out_dtype, tm=1024, tn=1024):
    M, K = x.shape
    N = w.shape[1]
    tm, tn = _tile(M, tm), _tile(N, tn)
    return pl.pallas_call(
        _matmul_kernel,
        grid=(M // tm, N // tn),
        in_specs=[pl.BlockSpec((tm, K), lambda i, j: (i, 0)),
                  pl.BlockSpec((K, tn), lambda i, j: (0, j))],
        out_specs=pl.BlockSpec((tm, tn), lambda i, j: (i, j)),
        out_shape=jax.ShapeDtypeStruct((M, N), out_dtype),
        compiler_params=_params(("arbitrary", "arbitrary")),
        name="proj_matmul",
    )(x, w)


def _residual_matmul_kernel(m_ref, w_ref, xa_ref, xb_ref, o_ref, *, steps1):
    mm = jnp.dot(m_ref[...], w_ref[...], preferred_element_type=F32)

    @pl.when(pl.program_id(0) < steps1)
    def _():
        o_ref[...] = ALPHA * xa_ref[...] + mm

    @pl.when(pl.program_id(0) >= steps1)
    def _():
        o_ref[...] = ALPHA * xb_ref[...] + mm


def _residual_matmul(merged, w_o, xa, xb, tm=1024, tn=512):
    T, K = merged.shape
    D = w_o.shape[1]
    n1 = xa.shape[0]
    tm, tn = _tile(math.gcd(n1, T - n1), tm), _tile(D, tn)
    steps1, nt = n1 // tm, D // tn
    return pl.pallas_call(
        functools.partial(_residual_matmul_kernel, steps1=steps1),
        grid=(T // tm, nt),
        in_specs=[pl.BlockSpec((tm, K), lambda i, n: (i, 0)),
                  pl.BlockSpec((K, tn), lambda i, n: (0, n)),
                  pl.BlockSpec((tm, tn), lambda i, n: (jnp.minimum(i, steps1 - 1), jnp.where(i < steps1, n, nt - 1))),
                  pl.BlockSpec((tm, tn), lambda i, n: (jnp.maximum(i - steps1, 0), jnp.where(i < steps1, 0, n)))],
        out_specs=pl.BlockSpec((tm, tn), lambda i, n: (i, n)),
        out_shape=jax.ShapeDtypeStruct((T, D), F32),
        compiler_params=_params(("arbitrary", "arbitrary")),
        name="out_proj_residual",
    )(merged, w_o, xa, xb)


def _attn_kernel(first_ref, last_ref, sink_ref, q_ref, kp_ref, kc_ref, kn_ref, vp_ref, vc_ref, vn_ref,
                 bias_ref, o_ref, *, kv_heads, group, scale):
    n = pl.program_id(0)
    neg_inf = jnp.float32(-jnp.inf)
    pen_p = jnp.where(first_ref[n] == 1, neg_inf, jnp.float32(0.0))
    pen_n = jnp.where(last_ref[n] == 1, neg_inf, jnp.float32(0.0))
    lane = lax.broadcasted_iota(I32, (1, 3 * BLOCK), 1)
    pen = jnp.where(lane < BLOCK, pen_p, jnp.where(lane >= 2 * BLOCK, pen_n, jnp.float32(0.0)))
    for h in range(kv_heads):
        cols = slice(h * HEAD_DIM, (h + 1) * HEAD_DIM)
        kcat = jnp.concatenate([kp_ref[:, cols], kc_ref[:, cols], kn_ref[:, cols]], axis=0).astype(BF16)
        vcat = jnp.concatenate([vp_ref[:, cols], vc_ref[:, cols], vn_ref[:, cols]], axis=0).astype(BF16)
        heads = [h * group + g for g in range(group)]
        qcols = [slice(hd * HEAD_DIM, (hd + 1) * HEAD_DIM) for hd in heads]
        sinks = [sink_ref[hd] for hd in heads]
        s = [lax.dot_general(q_ref[:, c].astype(BF16), kcat, NT_DIMS, preferred_element_type=F32) * scale
             + bias_ref[hd] + pen for hd, c in zip(heads, qcols)]
        m = [jnp.maximum(jnp.max(si, axis=1, keepdims=True), sk) for si, sk in zip(s, sinks)]
        e = [jnp.exp(si - mi) for si, mi in zip(s, m)]
        denom = [jnp.sum(ei, axis=1, keepdims=True) + jnp.exp(sk - mi) for ei, sk, mi in zip(e, sinks, m)]
        p = [(ei * (1.0 / di)).astype(BF16) for ei, di in zip(e, denom)]
        for c, pi in zip(qcols, p):
            o_ref[:, c] = jnp.dot(pi, vcat, preferred_element_type=F32).astype(o_ref.dtype)


def _t5_bucket(rel):
    nb = REL_BUCKETS // 2
    max_exact = nb // 2
    ret = (rel > 0).astype(I32) * nb
    n = jnp.abs(rel)
    nf = jnp.maximum(n, max_exact).astype(F32)
    large = max_exact + (jnp.log(nf / max_exact) / math.log(REL_MAX_DIST / max_exact) * (nb - max_exact)).astype(I32)
    large = jnp.minimum(large, nb - 1)
    return ret + jnp.where(n < max_exact, n, large)


def _attention(proj, first, last, sink, rel_table, col_q, col_k, col_v):
    T = proj.shape[0]
    nb = T // BLOCK
    group = ATT_HEADS // ATT_KV_HEADS
    qw, kw = ATT_HEADS * HEAD_DIM, ATT_KV_HEADS * HEAD_DIM
    assert col_q % qw == 0 and col_k % kw == 0 and col_v % kw == 0
    qo, ko, vo = col_q // qw, col_k // kw, col_v // kw
    rel = jnp.arange(3 * BLOCK)[None, :] - BLOCK - jnp.arange(BLOCK)[:, None]
    onehot = (_t5_bucket(rel)[:, :, None] == jnp.arange(REL_BUCKETS)[None, None, :]).astype(F32)
    bias = jnp.einsum('qkb,bh->hqk', onehot, rel_table.astype(F32), precision=lax.Precision.HIGHEST)
    bias = jnp.where((jnp.abs(rel) <= WINDOW)[None, :, :], bias, -jnp.inf)

    def kv_spec(off, shift):
        def imap(n, first, last):
            return (jnp.clip(n + shift, 0, nb - 1), off)
        return pl.BlockSpec((BLOCK, kw), imap)

    grid_spec = pltpu.PrefetchScalarGridSpec(
        num_scalar_prefetch=2,
        grid=(nb,),
        in_specs=[pl.BlockSpec(memory_space=pltpu.SMEM),
                  pl.BlockSpec((BLOCK, qw), lambda n, first, last: (n, qo)),
                  kv_spec(ko, -1), kv_spec(ko, 0), kv_spec(ko, 1),
                  kv_spec(vo, -1), kv_spec(vo, 0), kv_spec(vo, 1),
                  pl.BlockSpec((ATT_HEADS, BLOCK, 3 * BLOCK), lambda n, first, last: (0, 0, 0))],
        out_specs=pl.BlockSpec((BLOCK, qw), lambda n, first, last: (n, 0)),
    )
    return pl.pallas_call(
        functools.partial(_attn_kernel, kv_heads=ATT_KV_HEADS, group=group, scale=HEAD_DIM ** -0.5),
        grid_spec=grid_spec,
        out_shape=jax.ShapeDtypeStruct((T, qw), BF16),
        compiler_params=_params(("arbitrary",)),
        name="window_attention",
    )(first, last, sink.astype(F32), proj, proj, proj, proj, proj, proj, proj, bias)


def _rope(x, cos2, sin2):
    return x * cos2 + pltpu.roll(x, RET_DK // 2, axis=1) * sin2


def _ret_bwd_kernel(last_ref, posblk_ref, cdec_ref, q_ref, k_ref, v_ref, cos_ref, sin_ref, qb_ref, kb_ref, o_ref,
                    state_ref, *, chunks):
    h = pl.program_id(0)
    n = pl.num_programs(1) - 1 - pl.program_id(1)
    C = RET_CHUNK

    @pl.when(last_ref[n] == 1)
    def _():
        state_ref[...] = jnp.zeros_like(state_ref)

    state = state_ref[...]
    cdec = cdec_ref[1, h]
    for c in range(chunks - 1, -1, -1):
        sl = slice(c * C, (c + 1) * C)
        cos2, sin2 = cos_ref[sl, :], sin_ref[sl, :]
        q = _rope(q_ref[sl, :], cos2, sin2)
        k = _rope(k_ref[sl, :], cos2, sin2) * (RET_DK ** -0.5)
        vb = v_ref[sl, :].astype(BF16)
        o_ref[sl, :] = jnp.dot((q * qb_ref[0]).astype(BF16), state.astype(BF16), preferred_element_type=F32)
        state = state * cdec + lax.dot_general((k * kb_ref[0]).astype(BF16), vb, TN_DIMS,
                                               preferred_element_type=F32)
    state_ref[...] = state


def _ret_fwd_kernel(first_ref, posblk_ref, cdec_ref, q_ref, k_ref, v_ref, g_ref, cos_ref, sin_ref, dmat_ref, qf_ref,
                    kf_ref, ob_ref, gnw_ref, o_ref, state_ref, *, chunks):
    h = pl.program_id(0)
    n = pl.program_id(1)
    C = RET_CHUNK

    @pl.when(first_ref[n] == 1)
    def _():
        state_ref[...] = jnp.zeros_like(state_ref)

    state = state_ref[...]
    cdec = cdec_ref[0, h]
    for c in range(chunks):
        sl = slice(c * C, (c + 1) * C)
        cos2, sin2 = cos_ref[sl, :], sin_ref[sl, :]
        q = _rope(q_ref[sl, :], cos2, sin2)
        k = _rope(k_ref[sl, :], cos2, sin2) * (RET_DK ** -0.5)
        vb = v_ref[sl, :].astype(BF16)
        intra = lax.dot_general(q.astype(BF16), k.astype(BF16), NT_DIMS, preferred_element_type=F32) * dmat_ref[0]
        out = (jnp.dot(intra.astype(BF16), vb, preferred_element_type=F32)
               + jnp.dot((q * qf_ref[0]).astype(BF16), state.astype(BF16), preferred_element_type=F32)
               + ob_ref[sl, :])
        state = state * cdec + lax.dot_general((k * kf_ref[0]).astype(BF16), vb, TN_DIMS,
                                               preferred_element_type=F32)
        mu = jnp.mean(out, axis=1, keepdims=True)
        cen = out - mu
        var = jnp.mean(cen * cen, axis=1, keepdims=True)
        y = cen * lax.rsqrt(var + GN_EPS) * gnw_ref[...]
        o_ref[sl, :] = (_silu(g_ref[sl, :]) * y).astype(o_ref.dtype)
    state_ref[...] = state


def _retention(proj, first, last, posblk, chunks, ret_decay, ret_gn_w, s_max, col_q, col_k, col_v, col_g):
    T = proj.shape[0]
    C = RET_CHUNK
    R = chunks * C
    nc = T // R
    H = RET_HEADS
    qo, ko, vo, go = col_q // RET_DK, col_k // RET_DK, col_v // RET_DV, col_g // RET_DV
    assert col_q % RET_DK == 0 and col_k % RET_DK == 0 and col_v % RET_DV == 0 and col_g % RET_DV == 0

    half = RET_DK // 2
    inv = ROPE_BASE ** (-jnp.arange(half, dtype=F32) * 2.0 / RET_DK)
    ang = jnp.arange(s_max, dtype=F32)[:, None] * inv[None, :]
    cos2 = jnp.concatenate([jnp.cos(ang), jnp.cos(ang)], axis=1)
    sin2 = jnp.concatenate([-jnp.sin(ang), jnp.sin(ang)], axis=1)

    lg = -jnp.exp(ret_decay.astype(F32))
    pos = jnp.arange(C, dtype=F32)
    diff = pos[:, None] - pos[None, :]
    dec_f = jnp.where((diff >= 0)[None], jnp.exp(jnp.maximum(diff, 0.0)[None] * lg[0][:, None, None]), 0.0)
    dec_b = jnp.where((diff < 0)[None], jnp.exp(jnp.maximum(-diff, 0.0)[None] * lg[1][:, None, None]), 0.0)
    dmat = dec_f + dec_b

    def rows(tab):
        return jnp.broadcast_to(tab[:, :, None], (H, C, RET_DK))

    qf = rows(jnp.exp((pos[None, :] + 1.0) * lg[0][:, None]))
    kf = rows(jnp.exp((C - 1.0 - pos)[None, :] * lg[0][:, None]))
    qb = rows(jnp.exp((C - pos)[None, :] * lg[1][:, None]))
    kb = rows(jnp.exp(pos[None, :] * lg[1][:, None]))
    cdec = jnp.exp(C * lg)

    smem = pl.BlockSpec(memory_space=pltpu.SMEM)
    rev = lambda c: nc - 1 - c
    bwd_specs = [
        smem,
        pl.BlockSpec((R, RET_DK), lambda h, c, last, posb: (rev(c), qo + h)),
        pl.BlockSpec((R, RET_DK), lambda h, c, last, posb: (rev(c), ko + h)),
        pl.BlockSpec((R, RET_DV), lambda h, c, last, posb: (rev(c), vo + h)),
        pl.BlockSpec((R, RET_DK), lambda h, c, last, posb: (posb[rev(c)], 0)),
        pl.BlockSpec((R, RET_DK), lambda h, c, last, posb: (posb[rev(c)], 0)),
        pl.BlockSpec((1, C, RET_DK), lambda h, c, last, posb: (h, 0, 0)),
        pl.BlockSpec((1, C, RET_DK), lambda h, c, last, posb: (h, 0, 0)),
    ]
    out_b = pl.pallas_call(
        functools.partial(_ret_bwd_kernel, chunks=chunks),
        grid_spec=pltpu.PrefetchScalarGridSpec(
            num_scalar_prefetch=2, grid=(H, nc), in_specs=bwd_specs,
            out_specs=pl.BlockSpec((R, RET_DV), lambda h, c, last, posb: (rev(c), h)),
            scratch_shapes=[pltpu.VMEM((RET_DK, RET_DV), F32)]),
        out_shape=jax.ShapeDtypeStruct((T, H * RET_DV), F32),
        compiler_params=_params(("arbitrary", "arbitrary")),
        name="retention_backward",
    )(last, posblk, cdec, proj, proj, proj, cos2, sin2, qb, kb)

    fwd_specs = [
        smem,
        pl.BlockSpec((R, RET_DK), lambda h, n, first, posb: (n, qo + h)),
        pl.BlockSpec((R, RET_DK), lambda h, n, first, posb: (n, ko + h)),
        pl.BlockSpec((R, RET_DV), lambda h, n, first, posb: (n, vo + h)),
        pl.BlockSpec((R, RET_DV), lambda h, n, first, posb: (n, go + h)),
        pl.BlockSpec((R, RET_DK), lambda h, n, first, posb: (posb[n], 0)),
        pl.BlockSpec((R, RET_DK), lambda h, n, first, posb: (posb[n], 0)),
        pl.BlockSpec((1, C, C), lambda h, n, first, posb: (h, 0, 0)),
        pl.BlockSpec((1, C, RET_DK), lambda h, n, first, posb: (h, 0, 0)),
        pl.BlockSpec((1, C, RET_DK), lambda h, n, first, posb: (h, 0, 0)),
        pl.BlockSpec((R, RET_DV), lambda h, n, first, posb: (n, h)),
        pl.BlockSpec((1, RET_DV), lambda h, n, first, posb: (0, h)),
    ]
    return pl.pallas_call(
        functools.partial(_ret_fwd_kernel, chunks=chunks),
        grid_spec=pltpu.PrefetchScalarGridSpec(
            num_scalar_prefetch=2, grid=(H, nc), in_specs=fwd_specs,
            out_specs=pl.BlockSpec((R, RET_DV), lambda h, n, first, posb: (n, h)),
            scratch_shapes=[pltpu.VMEM((RET_DK, RET_DV), F32)]),
        out_shape=jax.ShapeDtypeStruct((T, H * RET_DV), BF16),
        compiler_params=_params(("arbitrary", "arbitrary")),
        name="retention_forward",
    )(first, posblk, cdec, proj, proj, proj, proj, cos2, sin2, dmat, qf, kf, out_b,
      ret_gn_w.astype(F32).reshape(1, -1))


def _merge_kernel(att_ref, wa_ref, ret_ref, wr_ref, ga_ref, gr_ref, o_ref):
    a = jnp.dot(att_ref[...], wa_ref[...], preferred_element_type=F32)
    r = jnp.dot(ret_ref[...], wr_ref[...], preferred_element_type=F32)
    o_ref[...] = (jax.nn.sigmoid(ga_ref[...]) * a + jax.nn.sigmoid(gr_ref[...]) * r).astype(o_ref.dtype)


def _merge(att, ret, proj, w_att_out, w_ret_out, col_ga, col_gr, tm=1024, tn=512):
    T = att.shape[0]
    D = w_att_out.shape[1]
    tm, tn = _tile(T, tm), _tile(D, tn)
    assert col_ga % tn == 0 and col_gr % tn == 0
    ao, ro = col_ga // tn, col_gr // tn
    return pl.pallas_call(
        _merge_kernel,
        grid=(T // tm, D // tn),
        in_specs=[pl.BlockSpec((tm, att.shape[1]), lambda i, j: (i, 0)),
                  pl.BlockSpec((w_att_out.shape[0], tn), lambda i, j: (0, j)),
                  pl.BlockSpec((tm, ret.shape[1]), lambda i, j: (i, 0)),
                  pl.BlockSpec((w_ret_out.shape[0], tn), lambda i, j: (0, j)),
                  pl.BlockSpec((tm, tn), lambda i, j: (i, ao + j)),
                  pl.BlockSpec((tm, tn), lambda i, j: (i, ro + j))],
        out_specs=pl.BlockSpec((tm, tn), lambda i, j: (i, j)),
        out_shape=jax.ShapeDtypeStruct((T, D), BF16),
        compiler_params=_params(("arbitrary", "arbitrary")),
        name="branch_merge",
    )(att, w_att_out, ret, w_ret_out, proj, proj)


def _layernorm_rows(z, w, b):
    mu = jnp.mean(z, axis=1, keepdims=True)
    cen = z - mu
    var = jnp.mean(cen * cen, axis=1, keepdims=True)
    return cen * lax.rsqrt(var + LN_EPS) * w + b


def _router_kernel(z_ref, lw_ref, lb_ref, wh_ref, wl_ref, bias_ref, idx_ref, rank_ref, gate_ref, cnt_ref, x1_ref,
                   slab_ref, xb_ref, carry_ref, *, rows, pitch):
    i = pl.program_id(0)
    tm = z_ref.shape[0]
    E = N_EXPERTS
    gsz = E // N_GROUPS
    neg_inf = jnp.float32(-jnp.inf)

    @pl.when(i == 0)
    def _():
        carry_ref[...] = jnp.zeros_like(carry_ref)

    x1_ref[...] = _layernorm_rows(z_ref[...], lw_ref[...], lb_ref[...])
    _store_slabs(slab_ref, lambda c: x1_ref[:, c * LANES:(c + 1) * LANES], tm, rows, pitch)
    x = x1_ref[...]
    xh = x.astype(BF16)
    xb_ref[...] = xh
    xl = (x - xh.astype(F32)).astype(BF16)
    logits = (lax.dot_general(wh_ref[...], xh, NT_DIMS, preferred_element_type=F32)
              + lax.dot_general(wh_ref[...], xl, NT_DIMS, preferred_element_type=F32)
              + lax.dot_general(wl_ref[...], xh, NT_DIMS, preferred_element_type=F32))
    scores = jax.nn.sigmoid(logits)
    biased = scores + bias_ref[...]

    row = lax.broadcasted_iota(I32, (gsz, tm), 0)
    gscore = []
    for g in range(N_GROUPS):
        blk = biased[g * gsz:(g + 1) * gsz, :]
        m1 = jnp.max(blk, axis=0, keepdims=True)
        first = jnp.min(jnp.where(blk == m1, row, gsz), axis=0, keepdims=True)
        m2 = jnp.max(jnp.where(row == first, neg_inf, blk), axis=0, keepdims=True)
        gscore.append(m1 + m2)
    gid = lax.broadcasted_iota(I32, (E, tm), 0) // gsz
    keep = jnp.zeros((E, tm), I32)
    for g in range(N_GROUPS):
        beaten = jnp.zeros((1, tm), I32)
        for o in range(N_GROUPS):
            if o == g:
                continue
            wins = (gscore[o] >= gscore[g]) if o < g else (gscore[o] > gscore[g])
            beaten = beaten + wins.astype(I32)
        keep = jnp.where(gid == g, (beaten < TOPK_GROUPS).astype(I32), keep)
    masked = jnp.where(keep > 0, biased, neg_inf)

    eidx = lax.broadcasted_iota(I32, (E, tm), 0)
    sel = jnp.zeros((E, tm), jnp.bool_)
    picks, weights = [], []
    for _ in range(TOP_K):
        m = jnp.max(masked, axis=0, keepdims=True)
        pick = jnp.min(jnp.where(masked == m, eidx, E), axis=0, keepdims=True)
        onehot = eidx == pick
        weights.append(jnp.sum(jnp.where(onehot, scores, 0.0), axis=0, keepdims=True))
        picks.append(pick)
        masked = jnp.where(onehot, neg_inf, masked)
        sel = sel | onehot
    wsum = weights[0]
    for w in weights[1:]:
        wsum = wsum + w

    self32 = sel.astype(F32)
    tri = (lax.broadcasted_iota(I32, (tm, tm), 0) < lax.broadcasted_iota(I32, (tm, tm), 1)).astype(F32).astype(BF16)
    prefix = jnp.dot(self32.astype(BF16), tri, preferred_element_type=F32) + carry_ref[...]
    for kk in range(TOP_K):
        onehot = eidx == picks[kk]
        idx_ref[kk:kk + 1, :] = picks[kk]
        rank_ref[kk:kk + 1, :] = jnp.sum(jnp.where(onehot, prefix, 0.0), axis=0, keepdims=True).astype(I32)
        gate_ref[kk:kk + 1, :] = weights[kk] / wsum * ROUTED_SCALE
    carry_ref[...] = carry_ref[...] + jnp.sum(self32, axis=1, keepdims=True)
    cnt_ref[...] = carry_ref[...]


def _ln_router(z, ln_w, ln_b, w_router, router_bias, rows, pitch, tm=256):
    T, D = z.shape
    E = N_EXPERTS
    tm = _tile(T, tm)
    wt = w_router.astype(F32).T
    wh = wt.astype(BF16)
    wl = (wt - wh.astype(F32)).astype(BF16)
    return pl.pallas_call(
        functools.partial(_router_kernel, rows=rows, pitch=pitch),
        grid=(T // tm,),
        in_specs=[pl.BlockSpec((tm, D), lambda i: (i, 0)),
                  pl.BlockSpec((1, D), lambda i: (0, 0)),
                  pl.BlockSpec((1, D), lambda i: (0, 0)),
                  pl.BlockSpec((E, D), lambda i: (0, 0)),
                  pl.BlockSpec((E, D), lambda i: (0, 0)),
                  pl.BlockSpec((E, 1), lambda i: (0, 0))],
        out_specs=[pl.BlockSpec((TOP_K, tm), lambda i: (0, i)),
                   pl.BlockSpec((TOP_K, tm), lambda i: (0, i)),
                   pl.BlockSpec((TOP_K, tm), lambda i: (0, i)),
                   pl.BlockSpec((E, 1), lambda i: (0, 0)),
                   pl.BlockSpec((tm, D), lambda i: (i, 0)),
                   pl.BlockSpec((tm * pitch, LANES), lambda i: (i, 0)),
                   pl.BlockSpec((tm, D), lambda i: (i, 0))],
        out_shape=[jax.ShapeDtypeStruct((TOP_K, T), I32),
                   jax.ShapeDtypeStruct((TOP_K, T), I32),
                   jax.ShapeDtypeStruct((TOP_K, T), F32),
                   jax.ShapeDtypeStruct((E, 1), F32),
                   jax.ShapeDtypeStruct((T, D), F32),
                   jax.ShapeDtypeStruct((T * pitch, LANES), I32),
                   jax.ShapeDtypeStruct((T, D), BF16)],
        scratch_shapes=[pltpu.VMEM((E, 1), F32)],
        compiler_params=_params(("arbitrary",)),
        name="layernorm_router",
    )(z, ln_w.astype(F32).reshape(1, -1), ln_b.astype(F32).reshape(1, -1), wh, wl,
      router_bias.astype(F32).reshape(E, 1))


def _slot_kernel(start_ref, idx_ref, rank_ref, slot_ref):
    idx = idx_ref[...]
    acc = rank_ref[...]
    for e in range(N_EXPERTS):
        acc = acc + jnp.where(idx == e, start_ref[e], 0)
    slot_ref[...] = acc


def _slots(pad_start, idx, rank, tm=2048):
    T = idx.shape[1]
    tm = _tile(T, tm)
    return pl.pallas_call(
        _slot_kernel,
        grid=(T // tm,),
        in_specs=[pl.BlockSpec(memory_space=pltpu.SMEM),
                  pl.BlockSpec((TOP_K, tm), lambda i: (0, i)),
                  pl.BlockSpec((TOP_K, tm), lambda i: (0, i))],
        out_specs=pl.BlockSpec((TOP_K, tm), lambda i: (0, i)),
        out_shape=jax.ShapeDtypeStruct((TOP_K, T), I32),
        compiler_params=_params(("arbitrary",)),
        name="slot_index",
    )(pad_start, idx, rank)


def _zero_fill_plan(npad, start, te, fn):
    pos = start
    bit = te // 2
    while bit >= 1:
        take = (npad & bit) != 0

        @pl.when(take)
        def _(pos=pos, bit=bit):
            fn(pos, bit)

        pos = pos + jnp.where(take, bit, 0)
        bit //= 2


def _dispatch_kernel(fill_start_ref, fill_len_ref, slot_ref, x_ref, xs_ref, zero_ref, sem, zsem, *, te, rows, pitch):
    i = pl.program_id(0)
    tm = x_ref.shape[0] // pitch

    def start_token(t, carry):
        src = x_ref.at[pl.ds(pl.multiple_of(t * pitch, SUBLANES), rows)]
        for k in range(TOP_K):
            dst = xs_ref.at[pl.ds(pl.multiple_of(slot_ref[t * TOP_K + k] * pitch, SUBLANES), rows)]
            pltpu.make_async_copy(src, dst, sem).start(priority=k % 2)
        return carry

    lax.fori_loop(0, tm, start_token, 0)

    @pl.when(i == 0)
    def _():
        zero_ref[...] = jnp.zeros_like(zero_ref)

        def zero_copy(pos, n):
            return pltpu.make_async_copy(zero_ref.at[pl.ds(0, n * pitch)],
                                         xs_ref.at[pl.ds(pl.multiple_of(pos * pitch, SUBLANES), n * pitch)], zsem)

        def start_fill(e, carry):
            _zero_fill_plan(fill_len_ref[e], fill_start_ref[e], te, lambda pos, n: zero_copy(pos, n).start())
            return carry

        def wait_fill(e, carry):
            _zero_fill_plan(fill_len_ref[e], fill_start_ref[e], te, lambda pos, n: zero_copy(pos, n).wait())
            return carry

        lax.fori_loop(0, N_EXPERTS, start_fill, 0)
        lax.fori_loop(0, N_EXPERTS, wait_fill, 0)

    def wait_token(t, carry):
        for k in range(TOP_K):
            pltpu.make_async_copy(x_ref.at[pl.ds(0, rows)], xs_ref.at[pl.ds(0, rows)], sem).wait()
        return carry

    lax.fori_loop(0, tm, wait_token, 0)


def _dispatch(x_slab, slot_flat, fill_start, fill_len, n_slots, te, rows, pitch, tm=256):
    T = x_slab.shape[0] // pitch
    tm = _tile(T, tm)
    return pl.pallas_call(
        functools.partial(_dispatch_kernel, te=te, rows=rows, pitch=pitch),
        grid_spec=pltpu.PrefetchScalarGridSpec(
            num_scalar_prefetch=2,
            grid=(T // tm,),
            in_specs=[pl.BlockSpec((tm * TOP_K,), lambda i, fs, fl: (i,), memory_space=pltpu.SMEM),
                      pl.BlockSpec((tm * pitch, LANES), lambda i, fs, fl: (i, 0))],
            out_specs=pl.BlockSpec(memory_space=pl.ANY),
            scratch_shapes=[pltpu.VMEM((te // 2 * pitch, LANES), I32),
                            pltpu.SemaphoreType.DMA(()),
                            pltpu.SemaphoreType.DMA(())]),
        out_shape=jax.ShapeDtypeStruct((n_slots * pitch, LANES), I32),
        compiler_params=_params(("arbitrary",)),
        name="moe_dispatch",
    )(fill_start, fill_len, slot_flat, x_slab)


def _ffn_partial(xb, wg, wu, wd):
    hid = _silu(jnp.dot(xb, wg, preferred_element_type=F32)) * jnp.dot(xb, wu, preferred_element_type=F32)
    return jnp.dot(hid.astype(BF16), wd, preferred_element_type=F32)


def _expert_up_kernel(blk_ref, chunk_ref, first_ref, blk_e_ref, nsteps_ref, x_ref, wg_ref, wu_ref, h_ref,
                      wgb_ref, wub_ref, *, rows, pitch):
    s = pl.program_id(0)
    te = h_ref.shape[0]

    @pl.when(s < nsteps_ref[0])
    def _():
        @pl.when(first_ref[s] == 1)
        def _():
            wgb_ref[...] = wg_ref[...].astype(BF16)
            wub_ref[...] = wu_ref[...].astype(BF16)

        pairs = [_unpack_pair(x_ref[pl.ds(r, te, stride=pitch), :]) for r in range(rows)]
        xb = jnp.concatenate([lo.astype(BF16) for lo, _ in pairs] + [hi.astype(BF16) for _, hi in pairs], axis=1)
        hid = (_silu(jnp.dot(xb, wgb_ref[...], preferred_element_type=F32))
               * jnp.dot(xb, wub_ref[...], preferred_element_type=F32))
        h_ref[...] = hid.astype(h_ref.dtype)


def _expert_down_kernel(blk_e_ref, first_ref, nvalid_ref, h_ref, wd_ref, o_ref, wdb_ref, *, rows, pitch):
    b = pl.program_id(0)
    te = h_ref.shape[0]
    half = wdb_ref.shape[1] // 2
    group = min(4, rows)

    @pl.when(b < nvalid_ref[0])
    def _():
        @pl.when(first_ref[b] == 1)
        def _():
            wdb_ref[...] = wd_ref[...].astype(BF16)

        hid = h_ref[...]
        for g0 in range(0, rows, group):
            width = group * LANES
            ylo = jnp.dot(hid, wdb_ref[:, g0 * LANES:g0 * LANES + width], preferred_element_type=F32)
            yhi = jnp.dot(hid, wdb_ref[:, half + g0 * LANES:half + g0 * LANES + width], preferred_element_type=F32)
            for r in range(group):
                cols = slice(r * LANES, (r + 1) * LANES)
                o_ref[pl.ds(g0 + r, te, stride=pitch), :] = _pack_pair(ylo[:, cols], yhi[:, cols])
        for r in range(rows, pitch):
            o_ref[pl.ds(r, te, stride=pitch), :] = jnp.zeros((te, LANES), I32)


def _routed_ffn(xs, blk_e, nvalid, pad_start, padded, wg, wu, wd, te, rows, pitch, tj=512):
    P = xs.shape[0] // pitch
    E, D, DE = wg.shape
    tj = _tile(DE, tj)
    nj = DE // tj
    nblk = P // te
    nsteps_max = nblk * nj

    b = jnp.arange(nblk, dtype=I32)
    first_blk = (pad_start // te).astype(I32)[blk_e]
    n_blk = (padded // te).astype(I32)[blk_e]
    step = nj * first_blk[None, :] + jnp.arange(nj, dtype=I32)[:, None] * n_blk[None, :] + (b - first_blk)[None, :]
    step = jnp.where((b < nvalid[0])[None, :], step, nsteps_max)
    nsteps = nvalid * nj
    blk_of = jnp.full((nsteps_max,), nvalid[0] - 1, I32).at[step].set(jnp.broadcast_to(b[None, :], step.shape),
                                                                      mode='drop')
    chunk_of = jnp.full((nsteps_max,), nj - 1, I32).at[step].set(
        jnp.broadcast_to(jnp.arange(nj, dtype=I32)[:, None], step.shape), mode='drop')
    first_of = jnp.zeros((nsteps_max,), I32).at[step].set(
        jnp.broadcast_to((b == first_blk).astype(I32)[None, :], step.shape), mode='drop')

    hid = pl.pallas_call(
        functools.partial(_expert_up_kernel, rows=rows, pitch=pitch),
        grid_spec=pltpu.PrefetchScalarGridSpec(
            num_scalar_prefetch=5,
            grid=(nsteps_max,),
            in_specs=[pl.BlockSpec((te * pitch, LANES), lambda s, bo, co, fo, be, ns: (bo[s], 0)),
                      pl.BlockSpec((None, D, tj), lambda s, bo, co, fo, be, ns: (be[bo[s]], 0, co[s])),
                      pl.BlockSpec((None, D, tj), lambda s, bo, co, fo, be, ns: (be[bo[s]], 0, co[s]))],
            out_specs=pl.BlockSpec((te, tj), lambda s, bo, co, fo, be, ns: (bo[s], co[s])),
            scratch_shapes=[pltpu.VMEM((D, tj), BF16), pltpu.VMEM((D, tj), BF16)]),
        out_shape=jax.ShapeDtypeStruct((P, DE), BF16),
        compiler_params=_params(("arbitrary",), V7X_VMEM_EXPERT_BYTES),
        name="routed_experts_up",
    )(blk_of, chunk_of, first_of, blk_e, nsteps, xs, wg, wu)

    def bclamp(b, nv):
        return jnp.minimum(b, nv[0] - 1)

    first_down = jnp.concatenate([jnp.ones((1,), I32), (blk_e[1:] != blk_e[:-1]).astype(I32)])
    return pl.pallas_call(
        functools.partial(_expert_down_kernel, rows=rows, pitch=pitch),
        grid_spec=pltpu.PrefetchScalarGridSpec(
            num_scalar_prefetch=3,
            grid=(nblk,),
            in_specs=[pl.BlockSpec((te, DE), lambda b, be, fd, nv: (bclamp(b, nv), 0)),
                      pl.BlockSpec((None, DE, D), lambda b, be, fd, nv: (be[bclamp(b, nv)], 0, 0))],
            out_specs=pl.BlockSpec((te * pitch, LANES), lambda b, be, fd, nv: (bclamp(b, nv), 0)),
            scratch_shapes=[pltpu.VMEM((DE, D), BF16)]),
        out_shape=jax.ShapeDtypeStruct((P * pitch, LANES), I32),
        compiler_params=_params(("arbitrary",), V7X_VMEM_EXPERT_BYTES),
        name="routed_experts_down",
    )(blk_e, first_down, nvalid, hid, wd)


def _shared_kernel(x_ref, wg_ref, wu_ref, wd_ref, o_ref):
    @pl.when(pl.program_id(1) == 0)
    def _():
        o_ref[...] = jnp.zeros_like(o_ref)

    o_ref[...] += _ffn_partial(x_ref[...], wg_ref[...], wu_ref[...], wd_ref[...])


def _shared_ffn(xb, wg, wu, wd, tm=512, tj=512):
    T, D = xb.shape
    DS = wg.shape[1]
    tm, tj = _tile(T, tm), _tile(DS, tj)
    return pl.pallas_call(
        _shared_kernel,
        grid=(T // tm, DS // tj),
        in_specs=[pl.BlockSpec((tm, D), lambda i, j: (i, 0)),
                  pl.BlockSpec((D, tj), lambda i, j: (0, j)),
                  pl.BlockSpec((D, tj), lambda i, j: (0, j)),
                  pl.BlockSpec((tj, D), lambda i, j: (j, 0))],
        out_specs=pl.BlockSpec((tm, D), lambda i, j: (i, 0)),
        out_shape=jax.ShapeDtypeStruct((T, D), F32),
        compiler_params=_params(("arbitrary", "arbitrary")),
        name="shared_expert",
    )(xb, wg, wu, wd)


def _combine_kernel(slot_ref, slot_next_ref, gate_ref, x_ref, sh_ref, lw_ref, lb_ref, ys_ref, o1_ref, o2_ref,
                    buf_ref, z_ref, sem, *, rows, pitch, steps1):
    i = pl.program_id(0)
    nsteps = pl.num_programs(0)
    tm, d_model = x_ref.shape
    half = d_model // 2
    cur = i % 2

    def start_all(srefs, buf):
        def body(t, carry):
            for kk in range(TOP_K):
                src = ys_ref.at[pl.ds(pl.multiple_of(srefs[t * TOP_K + kk] * pitch, SUBLANES), rows)]
                dst = buf_ref.at[buf, kk, pl.ds(pl.multiple_of(t * pitch, SUBLANES), rows)]
                pltpu.make_async_copy(src, dst, sem.at[buf]).start(priority=kk % 2)
            return carry
        lax.fori_loop(0, tm, body, 0)

    @pl.when(i == 0)
    def _():
        start_all(slot_ref, 0)

    @pl.when(i + 1 < nsteps)
    def _():
        start_all(slot_next_ref, 1 - cur)

    def wait_body(t, carry):
        for kk in range(TOP_K):
            pltpu.make_async_copy(ys_ref.at[pl.ds(0, rows)], buf_ref.at[cur, 0, pl.ds(0, rows)], sem.at[cur]).wait()
        return carry

    lax.fori_loop(0, tm, wait_body, 0)

    gates = gate_ref[...]
    gk = [gates[:, kk:kk + 1] for kk in range(TOP_K)]
    for r in range(rows):
        lo_acc = hi_acc = None
        for kk in range(TOP_K):
            lo, hi = _unpack_pair(buf_ref[cur, kk, pl.ds(r, tm, stride=pitch), :])
            lo_acc = gk[kk] * lo if lo_acc is None else lo_acc + gk[kk] * lo
            hi_acc = gk[kk] * hi if hi_acc is None else hi_acc + gk[kk] * hi
        for c0, acc in ((r * LANES, lo_acc), (half + r * LANES, hi_acc)):
            cols = slice(c0, c0 + LANES)
            z_ref[:, cols] = ALPHA * x_ref[:, cols] + (acc + sh_ref[:, cols])
    y = _layernorm_rows(z_ref[...], lw_ref[...], lb_ref[...])

    @pl.when(i < steps1)
    def _():
        o1_ref[...] = y

    @pl.when(i >= steps1)
    def _():
        o2_ref[...] = y


def _combine(ys, slot_flat, gate_tk, x1, shared, ln_w, ln_b, n1, rows, pitch, tm=128):
    T, D = x1.shape
    tm = _tile(math.gcd(n1, T - n1), tm)
    nsteps = T // tm
    steps1 = n1 // tm
    return pl.pallas_call(
        functools.partial(_combine_kernel, rows=rows, pitch=pitch, steps1=steps1),
        grid=(nsteps,),
        in_specs=[pl.BlockSpec((tm * TOP_K,), lambda i: (i,), memory_space=pltpu.SMEM),
                  pl.BlockSpec((tm * TOP_K,), lambda i: (jnp.minimum(i + 1, nsteps - 1),), memory_space=pltpu.SMEM),
                  pl.BlockSpec((tm, TOP_K), lambda i: (i, 0)),
                  pl.BlockSpec((tm, D), lambda i: (i, 0)),
                  pl.BlockSpec((tm, D), lambda i: (i, 0)),
                  pl.BlockSpec((1, D), lambda i: (0, 0)),
                  pl.BlockSpec((1, D), lambda i: (0, 0)),
                  pl.BlockSpec(memory_space=pl.ANY)],
        out_specs=[pl.BlockSpec((tm, D), lambda i: (jnp.minimum(i, steps1 - 1), 0)),
                   pl.BlockSpec((tm, D), lambda i: (jnp.maximum(i - steps1, 0), 0))],
        out_shape=[jax.ShapeDtypeStruct((n1, D), F32), jax.ShapeDtypeStruct((T - n1, D), F32)],
        scratch_shapes=[pltpu.VMEM((2, TOP_K, tm * pitch, LANES), I32), pltpu.VMEM((tm, D), F32),
                        pltpu.SemaphoreType.DMA((2,))],
        compiler_params=_params(("arbitrary",)),
        name="moe_combine_layernorm",
    )(slot_flat, slot_flat, gate_tk, x1, shared, ln_w.astype(F32).reshape(1, -1), ln_b.astype(F32).reshape(1, -1),
      ys)


def _seq_tables(groups, blk):
    first, last, pos = [], [], []
    for nseq, slen in groups:
        nb = slen // blk
        for _ in range(nseq):
            for b in range(nb):
                first.append(int(b == 0))
                last.append(int(b == nb - 1))
                pos.append(b)
    return (jnp.asarray(np.array(first, np.int32)), jnp.asarray(np.array(last, np.int32)),
            jnp.asarray(np.array(pos, np.int32)))


def _expert_block_rows(d_model):
    return 512 if d_model >= 4096 else 128


def kernel(x_prompt, x_sample, rel_table, w_in, attn_sink, ret_decay, ret_gn_w, w_att_out, w_ret_out, w_o, ln1_w,
           ln1_b, w_router, router_bias, w_exp_gate, w_exp_up, w_exp_down, w_sh_gate, w_sh_up, w_sh_down, ln2_w,
           ln2_b):
    assert DEPTH == 1
    B1, S1, D = x_prompt.shape
    B2, S2, _ = x_sample.shape
    assert S1 % BLOCK == 0 and S2 % BLOCK == 0 and BLOCK == RET_CHUNK
    xa, xb = x_prompt.reshape(B1 * S1, D), x_sample.reshape(B2 * S2, D)
    n1 = B1 * S1
    T = n1 + B2 * S2
    groups = ((B1, S1), (B2, S2))
    first, last, _ = _seq_tables(groups, BLOCK)
    ret_chunks = math.gcd(8, math.gcd(S1 // RET_CHUNK, S2 // RET_CHUNK))
    rfirst, rlast, rpos = _seq_tables(groups, ret_chunks * RET_CHUNK)

    att_q, att_kv = ATT_HEADS * HEAD_DIM, ATT_KV_HEADS * HEAD_DIM
    ret_qk, ret_v = RET_HEADS * RET_DK, RET_HEADS * RET_DV
    cols = np.concatenate([[0], np.cumsum([att_q, att_kv, att_kv, ret_qk, ret_qk, ret_v, ret_v, D, D])])
    c_aq, c_ak, c_av, c_rq, c_rk, c_rv, c_rg, c_ga, c_gr = (int(c) for c in cols[:-1])
    assert int(cols[-1]) == w_in.shape[2]
    rows = _slab_rows(D)
    pitch = _slab_pitch(rows)

    proj = _matmul(_concat_cast(xa, xb, BF16), w_in[0].astype(BF16), F32)
    att = _attention(proj, first, last, attn_sink[0], rel_table, c_aq, c_ak, c_av)
    ret = _retention(proj, rfirst, rlast, rpos, ret_chunks, ret_decay[0], ret_gn_w[0], max(S1, S2),
                     c_rq, c_rk, c_rv, c_rg)
    merged = _merge(att, ret, proj, w_att_out[0].astype(BF16), w_ret_out[0].astype(BF16), c_ga, c_gr)
    z1 = _residual_matmul(merged, w_o[0].astype(BF16), xa, xb)

    idx, rank, gate, counts, x1, x1_slab, x1b = _ln_router(z1, ln1_w[0], ln1_b[0], w_router[0], router_bias[0],
                                                           rows, pitch)
    te = _expert_block_rows(D)
    cnt = counts[:, 0].astype(I32)
    padded = (cnt + te - 1) // te * te
    pad_end = jnp.cumsum(padded)
    pad_start = pad_end - padded
    nblk = (T * TOP_K) // te + N_EXPERTS
    blk_e = jnp.minimum(jnp.searchsorted(pad_end, jnp.arange(nblk, dtype=I32) * te, side='right'),
                        N_EXPERTS - 1).astype(I32)
    nvalid = (pad_end[-1:] // te).astype(I32)
    slot = _slots(pad_start.astype(I32), idx, rank)
    slot_flat = slot.T.reshape(-1)

    xs = _dispatch(x1_slab, slot_flat, (pad_start + cnt).astype(I32), (padded - cnt).astype(I32), nblk * te, te,
                   rows, pitch)
    ys = _routed_ffn(xs, blk_e, nvalid, pad_start, padded, w_exp_gate[0], w_exp_up[0], w_exp_down[0],
                     te, rows, pitch)
    shared = _shared_ffn(x1b, w_sh_gate[0].astype(BF16), w_sh_up[0].astype(BF16), w_sh_down[0].astype(BF16))
    y1, y2 = _combine(ys, slot_flat, gate.T, x1, shared, ln2_w[0], ln2_b[0], n1, rows, pitch)
    return (y1.reshape(B1, S1, D), y2.reshape(B2, S2, D))
```

```python
import functools
import math

import numpy as np
import jax
import jax.numpy as jnp
from jax import lax
from jax.experimental import pallas as pl
from jax.experimental.pallas import tpu as pltpu

F32 = jnp.float32
BF16 = jnp.bfloat16
I32 = jnp.int32

HEAD_DIM = 128
ATT_HEADS = 16
ATT_KV_HEADS = 4
WINDOW = 128
BLOCK = 128
REL_BUCKETS = 32
REL_MAX_DIST = 128
RET_HEADS = 8
RET_DK = 128
RET_DV = 256
RET_CHUNK = 128
ROPE_BASE = 10000.0
N_EXPERTS = 128
TOP_K = 8
N_GROUPS = 8
TOPK_GROUPS = 4
ROUTED_SCALE = 2.5
LN_EPS = 1e-5
GN_EPS = 1e-5
DEPTH = 1
ALPHA = (2.0 * DEPTH) ** 0.25

V7X_VMEM_BUDGET_BYTES = 56 * 1024 * 1024
V7X_VMEM_EXPERT_BYTES = 63 * 1024 * 1024
LANES = 128
SUBLANES = 8

NT_DIMS = (((1,), (1,)), ((), ()))
TN_DIMS = (((0,), (0,)), ((), ()))


def _tile(dim, pref):
    t = min(dim, pref)
    assert dim % t == 0, (dim, pref)
    return t


def _params(sem, vmem=V7X_VMEM_BUDGET_BYTES):
    return pltpu.CompilerParams(dimension_semantics=sem, vmem_limit_bytes=vmem)


def _silu(v):
    return v * jax.nn.sigmoid(v)


def _slab_rows(d_model):
    rows = d_model // (2 * LANES)
    assert rows % SUBLANES == 0
    return rows


def _slab_pitch(rows):
    return rows if (rows // SUBLANES) % 2 == 1 else rows + SUBLANES


def _pack_pair(lo, hi):
    lo_bits = lax.bitcast_convert_type(lo.astype(BF16).astype(F32), I32)
    hi_bits = lax.bitcast_convert_type(hi.astype(BF16).astype(F32), I32)
    return hi_bits | lax.shift_right_logical(lo_bits, 16)


def _unpack_pair(w):
    lo = lax.bitcast_convert_type(lax.shift_left(w, 16), F32)
    hi = lax.bitcast_convert_type(w & jnp.int32(-65536), F32)
    return lo, hi


def _store_slabs(slab_ref, col, n_tok, rows, pitch):
    for r in range(rows):
        slab_ref[pl.ds(r, n_tok, stride=pitch), :] = _pack_pair(col(r), col(rows + r))
    for r in range(rows, pitch):
        slab_ref[pl.ds(r, n_tok, stride=pitch), :] = jnp.zeros((n_tok, LANES), I32)


def _matmul_kernel(x_ref, w_ref, o_ref):
    o_ref[...] = jnp.dot(x_ref[...], w_ref[...], preferred_element_type=F32).astype(o_ref.dtype)


def _concat_cast_kernel(xa_ref, xb_ref, o_ref, *, steps1):
    @pl.when(pl.program_id(0) < steps1)
    def _():
        o_ref[...] = xa_ref[...].astype(o_ref.dtype)

    @pl.when(pl.program_id(0) >= steps1)
    def _():
        o_ref[...] = xb_ref[...].astype(o_ref.dtype)


def _concat_cast(xa, xb, dtype, tm=256):
    n1, D = xa.shape
    T = n1 + xb.shape[0]
    tm = _tile(math.gcd(n1, T - n1), tm)
    steps1 = n1 // tm
    return pl.pallas_call(
        functools.partial(_concat_cast_kernel, steps1=steps1),
        grid=(T // tm,),
        in_specs=[pl.BlockSpec((tm, D), lambda i: (jnp.minimum(i, steps1 - 1), 0)),
                  pl.BlockSpec((tm, D), lambda i: (jnp.maximum(i - steps1, 0), 0))],
        out_specs=pl.BlockSpec((tm, D), lambda i: (i, 0)),
        out_shape=jax.ShapeDtypeStruct((T, D), dtype),
        compiler_params=_params(("arbitrary",)),
        name="concat_cast",
    )(xa, xb)


def _matmul(x, w, out_dtype, tm=1024, tn=1024):
    M, K = x.shape
    N = w.shape[1]
    tm, tn = _tile(M, tm), _tile(N, tn)
    return pl.pallas_call(
        _matmul_kernel,
        grid=(M // tm, N // tn),
        in_specs=[pl.BlockSpec((tm, K), lambda i, j: (i, 0)),
                  pl.BlockSpec((K, tn), lambda i, j: (0, j))],
        out_specs=pl.BlockSpec((tm, tn), lambda i, j: (i, j)),
        out_shape=jax.ShapeDtypeStruct((M, N), out_dtype),
        compiler_params=_params(("arbitrary", "arbitrary")),
        name="proj_matmul",
    )(x, w)


def _residual_matmul_kernel(m_ref, w_ref, xa_ref, xb_ref, o_ref, *, steps1):
    mm = jnp.dot(m_ref[...], w_ref[...], preferred_element_type=F32)

    @pl.when(pl.program_id(0) < steps1)
    def _():
        o_ref[...] = ALPHA * xa_ref[...] + mm

    @pl.when(pl.program_id(0) >= steps1)
    def _():
        o_ref[...] = ALPHA * xb_ref[...] + mm


def _residual_matmul(merged, w_o, xa, xb, tm=1024, tn=512):
    T, K = merged.shape
    D = w_o.shape[1]
    n1 = xa.shape[0]
    tm, tn = _tile(math.gcd(n1, T - n1), tm), _tile(D, tn)
    steps1, nt = n1 // tm, D // tn
    return pl.pallas_call(
        functools.partial(_residual_matmul_kernel, steps1=steps1),
        grid=(T // tm, nt),
        in_specs=[pl.BlockSpec((tm, K), lambda i, n: (i, 0)),
                  pl.BlockSpec((K, tn), lambda i, n: (0, n)),
                  pl.BlockSpec((tm, tn), lambda i, n: (jnp.minimum(i, steps1 - 1), jnp.where(i < steps1, n, nt - 1))),
                  pl.BlockSpec((tm, tn), lambda i, n: (jnp.maximum(i - steps1, 0), jnp.where(i < steps1, 0, n)))],
        out_specs=pl.BlockSpec((tm, tn), lambda i, n: (i, n)),
        out_shape=jax.ShapeDtypeStruct((T, D), F32),
        compiler_params=_params(("arbitrary", "arbitrary")),
        name="out_proj_residual",
    )(merged, w_o, xa, xb)


def _attn_kernel(first_ref, last_ref, sink_ref, q_ref, kp_ref, kc_ref, kn_ref, vp_ref, vc_ref, vn_ref,
                 bias_ref, o_ref, *, kv_heads, group, scale):
    n = pl.program_id(0)
    neg_inf = jnp.float32(-jnp.inf)
    pen_p = jnp.where(first_ref[n] == 1, neg_inf, jnp.float32(0.0))
    pen_n = jnp.where(last_ref[n] == 1, neg_inf, jnp.float32(0.0))
    lane = lax.broadcasted_iota(I32, (1, 3 * BLOCK), 1)
    pen = jnp.where(lane < BLOCK, pen_p, jnp.where(lane >= 2 * BLOCK, pen_n, jnp.float32(0.0)))
    for h in range(kv_heads):
        cols = slice(h * HEAD_DIM, (h + 1) * HEAD_DIM)
        kcat = jnp.concatenate([kp_ref[:, cols], kc_ref[:, cols], kn_ref[:, cols]], axis=0).astype(BF16)
        vcat = jnp.concatenate([vp_ref[:, cols], vc_ref[:, cols], vn_ref[:, cols]], axis=0).astype(BF16)
        heads = [h * group + g for g in range(group)]
        qcols = [slice(hd * HEAD_DIM, (hd + 1) * HEAD_DIM) for hd in heads]
        sinks = [sink_ref[hd] for hd in heads]
        s = [lax.dot_general(q_ref[:, c].astype(BF16), kcat, NT_DIMS, preferred_element_type=F32) * scale
             + bias_ref[hd] + pen for hd, c in zip(heads, qcols)]
        m = [jnp.maximum(jnp.max(si, axis=1, keepdims=True), sk) for si, sk in zip(s, sinks)]
        e = [jnp.exp(si - mi) for si, mi in zip(s, m)]
        denom = [jnp.sum(ei, axis=1, keepdims=True) + jnp.exp(sk - mi) for ei, sk, mi in zip(e, sinks, m)]
        p = [(ei * (1.0 / di)).astype(BF16) for ei, di in zip(e, denom)]
        for c, pi in zip(qcols, p):
            o_ref[:, c] = jnp.dot(pi, vcat, preferred_element_type=F32).astype(o_ref.dtype)


def _t5_bucket(rel):
    nb = REL_BUCKETS // 2
    max_exact = nb // 2
    ret = (rel > 0).astype(I32) * nb
    n = jnp.abs(rel)
    nf = jnp.maximum(n, max_exact).astype(F32)
    large = max_exact + (jnp.log(nf / max_exact) / math.log(REL_MAX_DIST / max_exact) * (nb - max_exact)).astype(I32)
    large = jnp.minimum(large, nb - 1)
    return ret + jnp.where(n < max_exact, n, large)


def _attention(proj, first, last, sink, rel_table, col_q, col_k, col_v):
    T = proj.shape[0]
    nb = T // BLOCK
    group = ATT_HEADS // ATT_KV_HEADS
    qw, kw = ATT_HEADS * HEAD_DIM, ATT_KV_HEADS * HEAD_DIM
    assert col_q % qw == 0 and col_k % kw == 0 and col_v % kw == 0
    qo, ko, vo = col_q // qw, col_k // kw, col_v // kw
    rel = jnp.arange(3 * BLOCK)[None, :] - BLOCK - jnp.arange(BLOCK)[:, None]
    onehot = (_t5_bucket(rel)[:, :, None] == jnp.arange(REL_BUCKETS)[None, None, :]).astype(F32)
    bias = jnp.einsum('qkb,bh->hqk', onehot, rel_table.astype(F32), precision=lax.Precision.HIGHEST)
    bias = jnp.where((jnp.abs(rel) <= WINDOW)[None, :, :], bias, -jnp.inf)

    def kv_spec(off, shift):
        def imap(n, first, last):
            return (jnp.clip(n + shift, 0, nb - 1), off)
        return pl.BlockSpec((BLOCK, kw), imap)

    grid_spec = pltpu.PrefetchScalarGridSpec(
        num_scalar_prefetch=2,
        grid=(nb,),
        in_specs=[pl.BlockSpec(memory_space=pltpu.SMEM),
                  pl.BlockSpec((BLOCK, qw), lambda n, first, last: (n, qo)),
                  kv_spec(ko, -1), kv_spec(ko, 0), kv_spec(ko, 1),
                  kv_spec(vo, -1), kv_spec(vo, 0), kv_spec(vo, 1),
                  pl.BlockSpec((ATT_HEADS, BLOCK, 3 * BLOCK), lambda n, first, last: (0, 0, 0))],
        out_specs=pl.BlockSpec((BLOCK, qw), lambda n, first, last: (n, 0)),
    )
    return pl.pallas_call(
        functools.partial(_attn_kernel, kv_heads=ATT_KV_HEADS, group=group, scale=HEAD_DIM ** -0.5),
        grid_spec=grid_spec,
        out_shape=jax.ShapeDtypeStruct((T, qw), BF16),
        compiler_params=_params(("arbitrary",)),
        name="window_attention",
    )(first, last, sink.astype(F32), proj, proj, proj, proj, proj, proj, proj, bias)


def _rope(x, cos2, sin2):
    return x * cos2 + pltpu.roll(x, RET_DK // 2, axis=1) * sin2


def _ret_bwd_kernel(last_ref, posblk_ref, cdec_ref, q_ref, k_ref, v_ref, cos_ref, sin_ref, qb_ref, kb_ref, o_ref,
                    state_ref, *, chunks):
    h = pl.program_id(0)
    n = pl.num_programs(1) - 1 - pl.program_id(1)
    C = RET_CHUNK

    @pl.when(last_ref[n] == 1)
    def _():
        state_ref[...] = jnp.zeros_like(state_ref)

    state = state_ref[...]
    cdec = cdec_ref[1, h]
    for c in range(chunks - 1, -1, -1):
        sl = slice(c * C, (c + 1) * C)
        cos2, sin2 = cos_ref[sl, :], sin_ref[sl, :]
        q = _rope(q_ref[sl, :], cos2, sin2)
        k = _rope(k_ref[sl, :], cos2, sin2) * (RET_DK ** -0.5)
        vb = v_ref[sl, :].astype(BF16)
        o_ref[sl, :] = jnp.dot((q * qb_ref[0]).astype(BF16), state.astype(BF16), preferred_element_type=F32)
        state = state * cdec + lax.dot_general((k * kb_ref[0]).astype(BF16), vb, TN_DIMS,
                                               preferred_element_type=F32)
    state_ref[...] = state


def _ret_fwd_kernel(first_ref, posblk_ref, cdec_ref, q_ref, k_ref, v_ref, g_ref, cos_ref, sin_ref, dmat_ref, qf_ref,
                    kf_ref, ob_ref, gnw_ref, o_ref, state_ref, *, chunks):
    h = pl.program_id(0)
    n = pl.program_id(1)
    C = RET_CHUNK

    @pl.when(first_ref[n] == 1)
    def _():
        state_ref[...] = jnp.zeros_like(state_ref)

    state = state_ref[...]
    cdec = cdec_ref[0, h]
    for c in range(chunks):
        sl = slice(c * C, (c + 1) * C)
        cos2, sin2 = cos_ref[sl, :], sin_ref[sl, :]
        q = _rope(q_ref[sl, :], cos2, sin2)
        k = _rope(k_ref[sl, :], cos2, sin2) * (RET_DK ** -0.5)
        vb = v_ref[sl, :].astype(BF16)
        intra = lax.dot_general(q.astype(BF16), k.astype(BF16), NT_DIMS, preferred_element_type=F32) * dmat_ref[0]
        out = (jnp.dot(intra.astype(BF16), vb, preferred_element_type=F32)
               + jnp.dot((q * qf_ref[0]).astype(BF16), state.astype(BF16), preferred_element_type=F32)
               + ob_ref[sl, :])
        state = state * cdec + lax.dot_general((k * kf_ref[0]).astype(BF16), vb, TN_DIMS,
                                               preferred_element_type=F32)
        mu = jnp.mean(out, axis=1, keepdims=True)
        cen = out - mu
        var = jnp.mean(cen * cen, axis=1, keepdims=True)
        y = cen * lax.rsqrt(var + GN_EPS) * gnw_ref[...]
        o_ref[sl, :] = (_silu(g_ref[sl, :]) * y).astype(o_ref.dtype)
    state_ref[...] = state


def _retention(proj, first, last, posblk, chunks, ret_decay, ret_gn_w, s_max, col_q, col_k, col_v, col_g):
    T = proj.shape[0]
    C = RET_CHUNK
    R = chunks * C
    nc = T // R
    H = RET_HEADS
    qo, ko, vo, go = col_q // RET_DK, col_k // RET_DK, col_v // RET_DV, col_g // RET_DV
    assert col_q % RET_DK == 0 and col_k % RET_DK == 0 and col_v % RET_DV == 0 and col_g % RET_DV == 0

    half = RET_DK // 2
    inv = ROPE_BASE ** (-jnp.arange(half, dtype=F32) * 2.0 / RET_DK)
    ang = jnp.arange(s_max, dtype=F32)[:, None] * inv[None, :]
    cos2 = jnp.concatenate([jnp.cos(ang), jnp.cos(ang)], axis=1)
    sin2 = jnp.concatenate([-jnp.sin(ang), jnp.sin(ang)], axis=1)

    lg = -jnp.exp(ret_decay.astype(F32))
    pos = jnp.arange(C, dtype=F32)
    diff = pos[:, None] - pos[None, :]
    dec_f = jnp.where((diff >= 0)[None], jnp.exp(jnp.maximum(diff, 0.0)[None] * lg[0][:, None, None]), 0.0)
    dec_b = jnp.where((diff < 0)[None], jnp.exp(jnp.maximum(-diff, 0.0)[None] * lg[1][:, None, None]), 0.0)
    dmat = dec_f + dec_b

    def rows(tab):
        return jnp.broadcast_to(tab[:, :, None], (H, C, RET_DK))

    qf = rows(jnp.exp((pos[None, :] + 1.0) * lg[0][:, None]))
    kf = rows(jnp.exp((C - 1.0 - pos)[None, :] * lg[0][:, None]))
    qb = rows(jnp.exp((C - pos)[None, :] * lg[1][:, None]))
    kb = rows(jnp.exp(pos[None, :] * lg[1][:, None]))
    cdec = jnp.exp(C * lg)

    smem = pl.BlockSpec(memory_space=pltpu.SMEM)
    rev = lambda c: nc - 1 - c
    bwd_specs = [
        smem,
        pl.BlockSpec((R, RET_DK), lambda h, c, last, posb: (rev(c), qo + h)),
        pl.BlockSpec((R, RET_DK), lambda h, c, last, posb: (rev(c), ko + h)),
        pl.BlockSpec((R, RET_DV), lambda h, c, last, posb: (rev(c), vo + h)),
        pl.BlockSpec((R, RET_DK), lambda h, c, last, posb: (posb[rev(c)], 0)),
        pl.BlockSpec((R, RET_DK), lambda h, c, last, posb: (posb[rev(c)], 0)),
        pl.BlockSpec((1, C, RET_DK), lambda h, c, last, posb: (h, 0, 0)),
        pl.BlockSpec((1, C, RET_DK), lambda h, c, last, posb: (h, 0, 0)),
    ]
    out_b = pl.pallas_call(
        functools.partial(_ret_bwd_kernel, chunks=chunks),
        grid_spec=pltpu.PrefetchScalarGridSpec(
            num_scalar_prefetch=2, grid=(H, nc), in_specs=bwd_specs,
            out_specs=pl.BlockSpec((R, RET_DV), lambda h, c, last, posb: (rev(c), h)),
            scratch_shapes=[pltpu.VMEM((RET_DK, RET_DV), F32)]),
        out_shape=jax.ShapeDtypeStruct((T, H * RET_DV), F32),
        compiler_params=_params(("arbitrary", "arbitrary")),
        name="retention_backward",
    )(last, posblk, cdec, proj, proj, proj, cos2, sin2, qb, kb)

    fwd_specs = [
        smem,
        pl.BlockSpec((R, RET_DK), lambda h, n, first, posb: (n, qo + h)),
        pl.BlockSpec((R, RET_DK), lambda h, n, first, posb: (n, ko + h)),
        pl.BlockSpec((R, RET_DV), lambda h, n, first, posb: (n, vo + h)),
        pl.BlockSpec((R, RET_DV), lambda h, n, first, posb: (n, go + h)),
        pl.BlockSpec((R, RET_DK), lambda h, n, first, posb: (posb[n], 0)),
        pl.BlockSpec((R, RET_DK), lambda h, n, first, posb: (posb[n], 0)),
        pl.BlockSpec((1, C, C), lambda h, n, first, posb: (h, 0, 0)),
        pl.BlockSpec((1, C, RET_DK), lambda h, n, first, posb: (h, 0, 0)),
        pl.BlockSpec((1, C, RET_DK), lambda h, n, first, posb: (h, 0, 0)),
        pl.BlockSpec((R, RET_DV), lambda h, n, first, posb: (n, h)),
        pl.BlockSpec((1, RET_DV), lambda h, n, first, posb: (0, h)),
    ]
    return pl.pallas_call(
        functools.partial(_ret_fwd_kernel, chunks=chunks),
        grid_spec=pltpu.PrefetchScalarGridSpec(
            num_scalar_prefetch=2, grid=(H, nc), in_specs=fwd_specs,
            out_specs=pl.BlockSpec((R, RET_DV), lambda h, n, first, posb: (n, h)),
            scratch_shapes=[pltpu.VMEM((RET_DK, RET_DV), F32)]),
        out_shape=jax.ShapeDtypeStruct((T, H * RET_DV), BF16),
        compiler_params=_params(("arbitrary", "arbitrary")),
        name="retention_forward",
    )(first, posblk, cdec, proj, proj, proj, proj, cos2, sin2, dmat, qf, kf, out_b,
      ret_gn_w.astype(F32).reshape(1, -1))


def _merge_kernel(att_ref, wa_ref, ret_ref, wr_ref, ga_ref, gr_ref, o_ref):
    a = jnp.dot(att_ref[...], wa_ref[...], preferred_element_type=F32)
    r = jnp.dot(ret_ref[...], wr_ref[...], preferred_element_type=F32)
    o_ref[...] = (jax.nn.sigmoid(ga_ref[...]) * a + jax.nn.sigmoid(gr_ref[...]) * r).astype(o_ref.dtype)


def _merge(att, ret, proj, w_att_out, w_ret_out, col_ga, col_gr, tm=1024, tn=512):
    T = att.shape[0]
    D = w_att_out.shape[1]
    tm, tn = _tile(T, tm), _tile(D, tn)
    assert col_ga % tn == 0 and col_gr % tn == 0
    ao, ro = col_ga // tn, col_gr // tn
    return pl.pallas_call(
        _merge_kernel,
        grid=(T // tm, D // tn),
        in_specs=[pl.BlockSpec((tm, att.shape[1]), lambda i, j: (i, 0)),
                  pl.BlockSpec((w_att_out.shape[0], tn), lambda i, j: (0, j)),
                  pl.BlockSpec((tm, ret.shape[1]), lambda i, j: (i, 0)),
                  pl.BlockSpec((w_ret_out.shape[0], tn), lambda i, j: (0, j)),
                  pl.BlockSpec((tm, tn), lambda i, j: (i, ao + j)),
                  pl.BlockSpec((tm, tn), lambda i, j: (i, ro + j))],
        out_specs=pl.BlockSpec((tm, tn), lambda i, j: (i, j)),
        out_shape=jax.ShapeDtypeStruct((T, D), BF16),
        compiler_params=_params(("arbitrary", "arbitrary")),
        name="branch_merge",
    )(att, w_att_out, ret, w_ret_out, proj, proj)


def _layernorm_rows(z, w, b):
    mu = jnp.mean(z, axis=1, keepdims=True)
    cen = z - mu
    var = jnp.mean(cen * cen, axis=1, keepdims=True)
    return cen * lax.rsqrt(var + LN_EPS) * w + b


def _router_kernel(z_ref, lw_ref, lb_ref, wh_ref, wl_ref, bias_ref, idx_ref, rank_ref, gate_ref, cnt_ref, x1_ref,
                   slab_ref, xb_ref, carry_ref, *, rows, pitch):
    i = pl.program_id(0)
    tm = z_ref.shape[0]
    E = N_EXPERTS
    gsz = E // N_GROUPS
    neg_inf = jnp.float32(-jnp.inf)

    @pl.when(i == 0)
    def _():
        carry_ref[...] = jnp.zeros_like(carry_ref)

    x1_ref[...] = _layernorm_rows(z_ref[...], lw_ref[...], lb_ref[...])
    _store_slabs(slab_ref, lambda c: x1_ref[:, c * LANES:(c + 1) * LANES], tm, rows, pitch)
    x = x1_ref[...]
    xh = x.astype(BF16)
    xb_ref[...] = xh
    xl = (x - xh.astype(F32)).astype(BF16)
    logits = (lax.dot_general(wh_ref[...], xh, NT_DIMS, preferred_element_type=F32)
              + lax.dot_general(wh_ref[...], xl, NT_DIMS, preferred_element_type=F32)
              + lax.dot_general(wl_ref[...], xh, NT_DIMS, preferred_element_type=F32))
    scores = jax.nn.sigmoid(logits)
    biased = scores + bias_ref[...]

    row = lax.broadcasted_iota(I32, (gsz, tm), 0)
    gscore = []
    for g in range(N_GROUPS):
        blk = biased[g * gsz:(g + 1) * gsz, :]
        m1 = jnp.max(blk, axis=0, keepdims=True)
        first = jnp.min(jnp.where(blk == m1, row, gsz), axis=0, keepdims=True)
        m2 = jnp.max(jnp.where(row == first, neg_inf, blk), axis=0, keepdims=True)
        gscore.append(m1 + m2)
    gid = lax.broadcasted_iota(I32, (E, tm), 0) // gsz
    keep = jnp.zeros((E, tm), I32)
    for g in range(N_GROUPS):
        beaten = jnp.zeros((1, tm), I32)
        for o in range(N_GROUPS):
            if o == g:
                continue
            wins = (gscore[o] >= gscore[g]) if o < g else (gscore[o] > gscore[g])
            beaten = beaten + wins.astype(I32)
        keep = jnp.where(gid == g, (beaten < TOPK_GROUPS).astype(I32), keep)
    masked = jnp.where(keep > 0, biased, neg_inf)

    eidx = lax.broadcasted_iota(I32, (E, tm), 0)
    sel = jnp.zeros((E, tm), jnp.bool_)
    picks, weights = [], []
    for _ in range(TOP_K):
        m = jnp.max(masked, axis=0, keepdims=True)
        pick = jnp.min(jnp.where(masked == m, eidx, E), axis=0, keepdims=True)
        onehot = eidx == pick
        weights.append(jnp.sum(jnp.where(onehot, scores, 0.0), axis=0, keepdims=True))
        picks.append(pick)
        masked = jnp.where(onehot, neg_inf, masked)
        sel = sel | onehot
    wsum = weights[0]
    for w in weights[1:]:
        wsum = wsum + w

    self32 = sel.astype(F32)
    tri = (lax.broadcasted_iota(I32, (tm, tm), 0) < lax.broadcasted_iota(I32, (tm, tm), 1)).astype(F32).astype(BF16)
    prefix = jnp.dot(self32.astype(BF16), tri, preferred_element_type=F32) + carry_ref[...]
    for kk in range(TOP_K):
        onehot = eidx == picks[kk]
        idx_ref[kk:kk + 1, :] = picks[kk]
        rank_ref[kk:kk + 1, :] = jnp.sum(jnp.where(onehot, prefix, 0.0), axis=0, keepdims=True).astype(I32)
        gate_ref[kk:kk + 1, :] = weights[kk] / wsum * ROUTED_SCALE
    carry_ref[...] = carry_ref[...] + jnp.sum(self32, axis=1, keepdims=True)
    cnt_ref[...] = carry_ref[...]


def _ln_router(z, ln_w, ln_b, w_router, router_bias, rows, pitch, tm=256):
    T, D = z.shape
    E = N_EXPERTS
    tm = _tile(T, tm)
    wt = w_router.astype(F32).T
    wh = wt.astype(BF16)
    wl = (wt - wh.astype(F32)).astype(BF16)
    return pl.pallas_call(
        functools.partial(_router_kernel, rows=rows, pitch=pitch),
        grid=(T // tm,),
        in_specs=[pl.BlockSpec((tm, D), lambda i: (i, 0)),
                  pl.BlockSpec((1, D), lambda i: (0, 0)),
                  pl.BlockSpec((1, D), lambda i: (0, 0)),
                  pl.BlockSpec((E, D), lambda i: (0, 0)),
                  pl.BlockSpec((E, D), lambda i: (0, 0)),
                  pl.BlockSpec((E, 1), lambda i: (0, 0))],
        out_specs=[pl.BlockSpec((TOP_K, tm), lambda i: (0, i)),
                   pl.BlockSpec((TOP_K, tm), lambda i: (0, i)),
                   pl.BlockSpec((TOP_K, tm), lambda i: (0, i)),
                   pl.BlockSpec((E, 1), lambda i: (0, 0)),
                   pl.BlockSpec((tm, D), lambda i: (i, 0)),
                   pl.BlockSpec((tm * pitch, LANES), lambda i: (i, 0)),
                   pl.BlockSpec((tm, D), lambda i: (i, 0))],
        out_shape=[jax.ShapeDtypeStruct((TOP_K, T), I32),
                   jax.ShapeDtypeStruct((TOP_K, T), I32),
                   jax.ShapeDtypeStruct((TOP_K, T), F32),
                   jax.ShapeDtypeStruct((E, 1), F32),
                   jax.ShapeDtypeStruct((T, D), F32),
                   jax.ShapeDtypeStruct((T * pitch, LANES), I32),
                   jax.ShapeDtypeStruct((T, D), BF16)],
        scratch_shapes=[pltpu.VMEM((E, 1), F32)],
        compiler_params=_params(("arbitrary",)),
        name="layernorm_router",
    )(z, ln_w.astype(F32).reshape(1, -1), ln_b.astype(F32).reshape(1, -1), wh, wl,
      router_bias.astype(F32).reshape(E, 1))


def _slot_kernel(start_ref, idx_ref, rank_ref, slot_ref):
    idx = idx_ref[...]
    acc = rank_ref[...]
    for e in range(N_EXPERTS):
        acc = acc + jnp.where(idx == e, start_ref[e], 0)
    slot_ref[...] = acc


def _slots(pad_start, idx, rank, tm=2048):
    T = idx.shape[1]
    tm = _tile(T, tm)
    return pl.pallas_call(
        _slot_kernel,
        grid=(T // tm,),
        in_specs=[pl.BlockSpec(memory_space=pltpu.SMEM),
                  pl.BlockSpec((TOP_K, tm), lambda i: (0, i)),
                  pl.BlockSpec((TOP_K, tm), lambda i: (0, i))],
        out_specs=pl.BlockSpec((TOP_K, tm), lambda i: (0, i)),
        out_shape=jax.ShapeDtypeStruct((TOP_K, T), I32),
        compiler_params=_params(("arbitrary",)),
        name="slot_index",
    )(pad_start, idx, rank)


def _zero_fill_plan(npad, start, te, fn):
    pos = start
    bit = te // 2
    while bit >= 1:
        take = (npad & bit) != 0

        @pl.when(take)
        def _(pos=pos, bit=bit):
            fn(pos, bit)

        pos = pos + jnp.where(take, bit, 0)
        bit //= 2


def _dispatch_kernel(fill_start_ref, fill_len_ref, slot_ref, x_ref, xs_ref, zero_ref, sem, zsem, *, te, rows, pitch):
    i = pl.program_id(0)
    tm = x_ref.shape[0] // pitch

    def start_token(t, carry):
        src = x_ref.at[pl.ds(pl.multiple_of(t * pitch, SUBLANES), rows)]
        for k in range(TOP_K):
            dst = xs_ref.at[pl.ds(pl.multiple_of(slot_ref[t * TOP_K + k] * pitch, SUBLANES), rows)]
            pltpu.make_async_copy(src, dst, sem).start(priority=k % 2)
        return carry

    lax.fori_loop(0, tm, start_token, 0)

    @pl.when(i == 0)
    def _():
        zero_ref[...] = jnp.zeros_like(zero_ref)

        def zero_copy(pos, n):
            return pltpu.make_async_copy(zero_ref.at[pl.ds(0, n * pitch)],
                                         xs_ref.at[pl.ds(pl.multiple_of(pos * pitch, SUBLANES), n * pitch)], zsem)

        def start_fill(e, carry):
            _zero_fill_plan(fill_len_ref[e], fill_start_ref[e], te, lambda pos, n: zero_copy(pos, n).start())
            return carry

        def wait_fill(e, carry):
            _zero_fill_plan(fill_len_ref[e], fill_start_ref[e], te, lambda pos, n: zero_copy(pos, n).wait())
            return carry

        lax.fori_loop(0, N_EXPERTS, start_fill, 0)
        lax.fori_loop(0, N_EXPERTS, wait_fill, 0)

    def wait_token(t, carry):
        for k in range(TOP_K):
            pltpu.make_async_copy(x_ref.at[pl.ds(0, rows)], xs_ref.at[pl.ds(0, rows)], sem).wait()
        return carry

    lax.fori_loop(0, tm, wait_token, 0)


def _dispatch(x_slab, slot_flat, fill_start, fill_len, n_slots, te, rows, pitch, tm=256):
    T = x_slab.shape[0] // pitch
    tm = _tile(T, tm)
    return pl.pallas_call(
        functools.partial(_dispatch_kernel, te=te, rows=rows, pitch=pitch),
        grid_spec=pltpu.PrefetchScalarGridSpec(
            num_scalar_prefetch=2,
            grid=(T // tm,),
            in_specs=[pl.BlockSpec((tm * TOP_K,), lambda i, fs, fl: (i,), memory_space=pltpu.SMEM),
                      pl.BlockSpec((tm * pitch, LANES), lambda i, fs, fl: (i, 0))],
            out_specs=pl.BlockSpec(memory_space=pl.ANY),
            scratch_shapes=[pltpu.VMEM((te // 2 * pitch, LANES), I32),
                            pltpu.SemaphoreType.DMA(()),
                            pltpu.SemaphoreType.DMA(())]),
        out_shape=jax.ShapeDtypeStruct((n_slots * pitch, LANES), I32),
        compiler_params=_params(("arbitrary",)),
        name="moe_dispatch",
    )(fill_start, fill_len, slot_flat, x_slab)


def _ffn_partial(xb, wg, wu, wd):
    hid = _silu(jnp.dot(xb, wg, preferred_element_type=F32)) * jnp.dot(xb, wu, preferred_element_type=F32)
    return jnp.dot(hid.astype(BF16), wd, preferred_element_type=F32)


def _expert_up_kernel(blk_ref, chunk_ref, first_ref, exp_ref, exp_look_ref, chunk_look_ref, nsteps_ref, x_ref, wg_ref,
                      wu_ref, h_ref, wgb_ref, wub_ref, *, rows, pitch):
    s = pl.program_id(0)
    te = h_ref.shape[0]

    @pl.when(s < nsteps_ref[0])
    def _():
        @pl.when(first_ref[s] == 1)
        def _():
            wgb_ref[...] = wg_ref[...].astype(BF16)
            wub_ref[...] = wu_ref[...].astype(BF16)

        pairs = [_unpack_pair(x_ref[pl.ds(r, te, stride=pitch), :]) for r in range(rows)]
        xb = jnp.concatenate([lo.astype(BF16) for lo, _ in pairs] + [hi.astype(BF16) for _, hi in pairs], axis=1)
        hid = (_silu(jnp.dot(xb, wgb_ref[...], preferred_element_type=F32))
               * jnp.dot(xb, wub_ref[...], preferred_element_type=F32))
        h_ref[...] = hid.astype(h_ref.dtype)


def _expert_down_kernel(exp_ref, exp_look_ref, first_ref, nvalid_ref, h_ref, wda_ref, wdb_half_ref, o_ref, wdb_ref, *,
                        rows, pitch):
    b = pl.program_id(0)
    te = h_ref.shape[0]
    half = wdb_ref.shape[1] // 2
    khalf = wda_ref.shape[0]
    group = min(4, rows)

    @pl.when(b < nvalid_ref[0])
    def _():
        @pl.when(first_ref[b] == 1)
        def _():
            wdb_ref[0:khalf, :] = wda_ref[...].astype(BF16)
            wdb_ref[khalf:, :] = wdb_half_ref[...].astype(BF16)

        hid = h_ref[...]
        for g0 in range(0, rows, group):
            width = group * LANES
            ylo = jnp.dot(hid, wdb_ref[:, g0 * LANES:g0 * LANES + width], preferred_element_type=F32)
            yhi = jnp.dot(hid, wdb_ref[:, half + g0 * LANES:half + g0 * LANES + width], preferred_element_type=F32)
            for r in range(group):
                cols = slice(r * LANES, (r + 1) * LANES)
                o_ref[pl.ds(g0 + r, te, stride=pitch), :] = _pack_pair(ylo[:, cols], yhi[:, cols])
        for r in range(rows, pitch):
            o_ref[pl.ds(r, te, stride=pitch), :] = jnp.zeros((te, LANES), I32)


def _routed_ffn(xs, blk_e, nvalid, pad_start, padded, wg, wu, wd, te, rows, pitch, tj=512):
    P = xs.shape[0] // pitch
    E, D, DE = wg.shape
    tj = _tile(DE, tj)
    nj = DE // tj
    nblk = P // te
    nsteps_max = nblk * nj

    b = jnp.arange(nblk, dtype=I32)
    first_blk = (pad_start // te).astype(I32)[blk_e]
    n_blk = (padded // te).astype(I32)[blk_e]
    step = nj * first_blk[None, :] + jnp.arange(nj, dtype=I32)[:, None] * n_blk[None, :] + (b - first_blk)[None, :]
    step = jnp.where((b < nvalid[0])[None, :], step, nsteps_max)
    nsteps = nvalid * nj
    blk_of = jnp.full((nsteps_max,), nvalid[0] - 1, I32).at[step].set(jnp.broadcast_to(b[None, :], step.shape),
                                                                      mode='drop')
    chunk_of = jnp.full((nsteps_max,), nj - 1, I32).at[step].set(
        jnp.broadcast_to(jnp.arange(nj, dtype=I32)[:, None], step.shape), mode='drop')
    first_of = jnp.zeros((nsteps_max,), I32).at[step].set(
        jnp.broadcast_to((b == first_blk).astype(I32)[None, :], step.shape), mode='drop')

    exp_of = blk_e[blk_of]

    def look_ahead(tab):
        nxt = jnp.concatenate([tab[1:], tab[-1:]])
        return jnp.where(first_of == 1, tab, nxt)

    exp_look, chunk_look = look_ahead(exp_of), look_ahead(chunk_of)
    hid = pl.pallas_call(
        functools.partial(_expert_up_kernel, rows=rows, pitch=pitch),
        grid_spec=pltpu.PrefetchScalarGridSpec(
            num_scalar_prefetch=7,
            grid=(nsteps_max,),
            in_specs=[pl.BlockSpec((te * pitch, LANES), lambda s, bo, co, fo, eo, el, cl, ns: (bo[s], 0)),
                      pl.BlockSpec((None, D, tj), lambda s, bo, co, fo, eo, el, cl, ns: (eo[s], 0, co[s])),
                      pl.BlockSpec((None, D, tj), lambda s, bo, co, fo, eo, el, cl, ns: (el[s], 0, cl[s]))],
            out_specs=pl.BlockSpec((te, tj), lambda s, bo, co, fo, eo, el, cl, ns: (bo[s], co[s])),
            scratch_shapes=[pltpu.VMEM((D, tj), BF16), pltpu.VMEM((D, tj), BF16)]),
        out_shape=jax.ShapeDtypeStruct((P, DE), BF16),
        compiler_params=_params(("arbitrary",), V7X_VMEM_EXPERT_BYTES),
        name="routed_experts_up",
    )(blk_of, chunk_of, first_of, exp_of, exp_look, chunk_look, nsteps, xs, wg, wu)

    def bclamp(b, nv):
        return jnp.minimum(b, nv[0] - 1)

    bc = jnp.minimum(b, nvalid[0] - 1)
    exp_d = blk_e[bc]
    first_down = jnp.concatenate([jnp.ones((1,), I32), (exp_d[1:] != exp_d[:-1]).astype(I32)])
    exp_d_look = jnp.where(first_down == 1, exp_d, jnp.concatenate([exp_d[1:], exp_d[-1:]]))
    return pl.pallas_call(
        functools.partial(_expert_down_kernel, rows=rows, pitch=pitch),
        grid_spec=pltpu.PrefetchScalarGridSpec(
            num_scalar_prefetch=4,
            grid=(nblk,),
            in_specs=[pl.BlockSpec((te, DE), lambda b, ed, el, fd, nv: (bclamp(b, nv), 0)),
                      pl.BlockSpec((None, DE // 2, D), lambda b, ed, el, fd, nv: (ed[b], 0, 0)),
                      pl.BlockSpec((None, DE // 2, D), lambda b, ed, el, fd, nv: (el[b], 1, 0))],
            out_specs=pl.BlockSpec((te * pitch, LANES), lambda b, ed, el, fd, nv: (bclamp(b, nv), 0)),
            scratch_shapes=[pltpu.VMEM((DE, D), BF16)]),
        out_shape=jax.ShapeDtypeStruct((P * pitch, LANES), I32),
        compiler_params=_params(("arbitrary",), V7X_VMEM_EXPERT_BYTES),
        name="routed_experts_down",
    )(exp_d, exp_d_look, first_down, nvalid, hid, wd, wd)


def _shared_kernel(x_ref, wg_ref, wu_ref, wd_ref, o_ref):
    @pl.when(pl.program_id(1) == 0)
    def _():
        o_ref[...] = jnp.zeros_like(o_ref)

    o_ref[...] += _ffn_partial(x_ref[...], wg_ref[...], wu_ref[...], wd_ref[...])


def _shared_ffn(xb, wg, wu, wd, tm=512, tj=512):
    T, D = xb.shape
    DS = wg.shape[1]
    tm, tj = _tile(T, tm), _tile(DS, tj)
    return pl.pallas_call(
        _shared_kernel,
        grid=(T // tm, DS // tj),
        in_specs=[pl.BlockSpec((tm, D), lambda i, j: (i, 0)),
                  pl.BlockSpec((D, tj), lambda i, j: (0, j)),
                  pl.BlockSpec((D, tj), lambda i, j: (0, j)),
                  pl.BlockSpec((tj, D), lambda i, j: (j, 0))],
        out_specs=pl.BlockSpec((tm, D), lambda i, j: (i, 0)),
        out_shape=jax.ShapeDtypeStruct((T, D), F32),
        compiler_params=_params(("arbitrary", "arbitrary")),
        name="shared_expert",
    )(xb, wg, wu, wd)


def _combine_kernel(slot_ref, slot_next_ref, gate_ref, x_ref, sh_ref, lw_ref, lb_ref, ys_ref, o1_ref, o2_ref,
                    buf_ref, z_ref, sem, *, rows, pitch, steps1):
    i = pl.program_id(0)
    nsteps = pl.num_programs(0)
    tm, d_model = x_ref.shape
    half = d_model // 2
    cur = i % 2

    def start_all(srefs, buf):
        def body(t, carry):
            for kk in range(TOP_K):
                src = ys_ref.at[pl.ds(pl.multiple_of(srefs[t * TOP_K + kk] * pitch, SUBLANES), rows)]
                dst = buf_ref.at[buf, kk, pl.ds(pl.multiple_of(t * pitch, SUBLANES), rows)]
                pltpu.make_async_copy(src, dst, sem.at[buf]).start(priority=kk % 2)
            return carry
        lax.fori_loop(0, tm, body, 0)

    @pl.when(i == 0)
    def _():
        start_all(slot_ref, 0)

    @pl.when(i + 1 < nsteps)
    def _():
        start_all(slot_next_ref, 1 - cur)

    def wait_body(t, carry):
        for kk in range(TOP_K):
            pltpu.make_async_copy(ys_ref.at[pl.ds(0, rows)], buf_ref.at[cur, 0, pl.ds(0, rows)], sem.at[cur]).wait()
        return carry

    lax.fori_loop(0, tm, wait_body, 0)

    gates = gate_ref[...]
    gk = [gates[:, kk:kk + 1] for kk in range(TOP_K)]
    for r in range(rows):
        lo_acc = hi_acc = None
        for kk in range(TOP_K):
            lo, hi = _unpack_pair(buf_ref[cur, kk, pl.ds(r, tm, stride=pitch), :])
            lo_acc = gk[kk] * lo if lo_acc is None else lo_acc + gk[kk] * lo
            hi_acc = gk[kk] * hi if hi_acc is None else hi_acc + gk[kk] * hi
        for c0, acc in ((r * LANES, lo_acc), (half + r * LANES, hi_acc)):
            cols = slice(c0, c0 + LANES)
            z_ref[:, cols] = ALPHA * x_ref[:, cols] + (acc + sh_ref[:, cols])
    y = _layernorm_rows(z_ref[...], lw_ref[...], lb_ref[...])

    @pl.when(i < steps1)
    def _():
        o1_ref[...] = y

    @pl.when(i >= steps1)
    def _():
        o2_ref[...] = y


def _combine(ys, slot_flat, gate_tk, x1, shared, ln_w, ln_b, n1, rows, pitch, tm=128):
    T, D = x1.shape
    tm = _tile(math.gcd(n1, T - n1), tm)
    nsteps = T // tm
    steps1 = n1 // tm
    return pl.pallas_call(
        functools.partial(_combine_kernel, rows=rows, pitch=pitch, steps1=steps1),
        grid=(nsteps,),
        in_specs=[pl.BlockSpec((tm * TOP_K,), lambda i: (i,), memory_space=pltpu.SMEM),
                  pl.BlockSpec((tm * TOP_K,), lambda i: (jnp.minimum(i + 1, nsteps - 1),), memory_space=pltpu.SMEM),
                  pl.BlockSpec((tm, TOP_K), lambda i: (i, 0)),
                  pl.BlockSpec((tm, D), lambda i: (i, 0)),
                  pl.BlockSpec((tm, D), lambda i: (i, 0)),
                  pl.BlockSpec((1, D), lambda i: (0, 0)),
                  pl.BlockSpec((1, D), lambda i: (0, 0)),
                  pl.BlockSpec(memory_space=pl.ANY)],
        out_specs=[pl.BlockSpec((tm, D), lambda i: (jnp.minimum(i, steps1 - 1), 0)),
                   pl.BlockSpec((tm, D), lambda i: (jnp.maximum(i - steps1, 0), 0))],
        out_shape=[jax.ShapeDtypeStruct((n1, D), F32), jax.ShapeDtypeStruct((T - n1, D), F32)],
        scratch_shapes=[pltpu.VMEM((2, TOP_K, tm * pitch, LANES), I32), pltpu.VMEM((tm, D), F32),
                        pltpu.SemaphoreType.DMA((2,))],
        compiler_params=_params(("arbitrary",)),
        name="moe_combine_layernorm",
    )(slot_flat, slot_flat, gate_tk, x1, shared, ln_w.astype(F32).reshape(1, -1), ln_b.astype(F32).reshape(1, -1),
      ys)


def _seq_tables(groups, blk):
    first, last, pos = [], [], []
    for nseq, slen in groups:
        nb = slen // blk
        for _ in range(nseq):
            for b in range(nb):
                first.append(int(b == 0))
                last.append(int(b == nb - 1))
                pos.append(b)
    return (jnp.asarray(np.array(first, np.int32)), jnp.asarray(np.array(last, np.int32)),
            jnp.asarray(np.array(pos, np.int32)))


def _expert_block_rows(d_model):
    return 512 if d_model >= 4096 else 128


def kernel(x_prompt, x_sample, rel_table, w_in, attn_sink, ret_decay, ret_gn_w, w_att_out, w_ret_out, w_o, ln1_w,
           ln1_b, w_router, router_bias, w_exp_gate, w_exp_up, w_exp_down, w_sh_gate, w_sh_up, w_sh_down, ln2_w,
           ln2_b):
    assert DEPTH == 1
    B1, S1, D = x_prompt.shape
    B2, S2, _ = x_sample.shape
    assert S1 % BLOCK == 0 and S2 % BLOCK == 0 and BLOCK == RET_CHUNK
    xa, xb = x_prompt.reshape(B1 * S1, D), x_sample.reshape(B2 * S2, D)
    n1 = B1 * S1
    T = n1 + B2 * S2
    groups = ((B1, S1), (B2, S2))
    first, last, _ = _seq_tables(groups, BLOCK)
    ret_chunks = math.gcd(8, math.gcd(S1 // RET_CHUNK, S2 // RET_CHUNK))
    rfirst, rlast, rpos = _seq_tables(groups, ret_chunks * RET_CHUNK)

    att_q, att_kv = ATT_HEADS * HEAD_DIM, ATT_KV_HEADS * HEAD_DIM
    ret_qk, ret_v = RET_HEADS * RET_DK, RET_HEADS * RET_DV
    cols = np.concatenate([[0], np.cumsum([att_q, att_kv, att_kv, ret_qk, ret_qk, ret_v, ret_v, D, D])])
    c_aq, c_ak, c_av, c_rq, c_rk, c_rv, c_rg, c_ga, c_gr = (int(c) for c in cols[:-1])
    assert int(cols[-1]) == w_in.shape[2]
    rows = _slab_rows(D)
    pitch = _slab_pitch(rows)

    proj = _matmul(_concat_cast(xa, xb, BF16), w_in[0].astype(BF16), F32)
    att = _attention(proj, first, last, attn_sink[0], rel_table, c_aq, c_ak, c_av)
    ret = _retention(proj, rfirst, rlast, rpos, ret_chunks, ret_decay[0], ret_gn_w[0], max(S1, S2),
                     c_rq, c_rk, c_rv, c_rg)
    merged = _merge(att, ret, proj, w_att_out[0].astype(BF16), w_ret_out[0].astype(BF16), c_ga, c_gr)
    z1 = _residual_matmul(merged, w_o[0].astype(BF16), xa, xb)

    idx, rank, gate, counts, x1, x1_slab, x1b = _ln_router(z1, ln1_w[0], ln1_b[0], w_router[0], router_bias[0],
                                                           rows, pitch)
    te = _expert_block_rows(D)
    cnt = counts[:, 0].astype(I32)
    padded = (cnt + te - 1) // te * te
    pad_end = jnp.cumsum(padded)
    pad_start = pad_end - padded
    nblk = (T * TOP_K) // te + N_EXPERTS
    blk_e = jnp.minimum(jnp.searchsorted(pad_end, jnp.arange(nblk, dtype=I32) * te, side='right'),
                        N_EXPERTS - 1).astype(I32)
    nvalid = (pad_end[-1:] // te).astype(I32)
    slot = _slots(pad_start.astype(I32), idx, rank)
    slot_flat = slot.T.reshape(-1)

    xs = _dispatch(x1_slab, slot_flat, (pad_start + cnt).astype(I32), (padded - cnt).astype(I32), nblk * te, te,
                   rows, pitch)
    ys = _routed_ffn(xs, blk_e, nvalid, pad_start, padded, w_exp_gate[0], w_exp_up[0], w_exp_down[0],
                     te, rows, pitch)
    shared = _shared_ffn(x1b, w_sh_gate[0].astype(BF16), w_sh_up[0].astype(BF16), w_sh_down[0].astype(BF16))
    y1, y2 = _combine(ys, slot_flat, gate.T, x1, shared, ln2_w[0], ln2_b[0], n1, rows, pitch)
    return (y1.reshape(B1, S1, D), y2.reshape(B2, S2, D))
```

```python
import functools
import math

import numpy as np
import jax
import jax.numpy as jnp
from jax import lax
from jax.experimental import pallas as pl
from jax.experimental.pallas import tpu as pltpu

F32 = jnp.float32
BF16 = jnp.bfloat16
I32 = jnp.int32

HEAD_DIM = 128
ATT_HEADS = 16
ATT_KV_HEADS = 4
WINDOW = 128
BLOCK = 128
REL_BUCKETS = 32
REL_MAX_DIST = 128
RET_HEADS = 8
RET_DK = 128
RET_DV = 256
RET_CHUNK = 128
ROPE_BASE = 10000.0
N_EXPERTS = 128
TOP_K = 8
N_GROUPS = 8
TOPK_GROUPS = 4
ROUTED_SCALE = 2.5
LN_EPS = 1e-5
GN_EPS = 1e-5
DEPTH = 1
ALPHA = (2.0 * DEPTH) ** 0.25

V7X_VMEM_BUDGET_BYTES = 56 * 1024 * 1024
V7X_VMEM_EXPERT_BYTES = 63 * 1024 * 1024
LANES = 128
SUBLANES = 8

NT_DIMS = (((1,), (1,)), ((), ()))
TN_DIMS = (((0,), (0,)), ((), ()))


def _tile(dim, pref):
    t = min(dim, pref)
    assert dim % t == 0, (dim, pref)
    return t


def _params(sem, vmem=V7X_VMEM_BUDGET_BYTES):
    return pltpu.CompilerParams(dimension_semantics=sem, vmem_limit_bytes=vmem)


def _silu(v):
    return v * jax.nn.sigmoid(v)


def _slab_rows(d_model):
    rows = d_model // (2 * LANES)
    assert rows % SUBLANES == 0
    return rows


def _slab_pitch(rows):
    return rows if (rows // SUBLANES) % 2 == 1 else rows + SUBLANES


def _pack_pair(lo, hi):
    lo_bits = lax.bitcast_convert_type(lo.astype(BF16).astype(F32), I32)
    hi_bits = lax.bitcast_convert_type(hi.astype(BF16).astype(F32), I32)
    return hi_bits | lax.shift_right_logical(lo_bits, 16)


def _unpack_pair(w):
    lo = lax.bitcast_convert_type(lax.shift_left(w, 16), F32)
    hi = lax.bitcast_convert_type(w & jnp.int32(-65536), F32)
    return lo, hi


def _store_slabs(slab_ref, col, n_tok, rows, pitch):
    for r in range(rows):
        slab_ref[pl.ds(r, n_tok, stride=pitch), :] = _pack_pair(col(r), col(rows + r))
    for r in range(rows, pitch):
        slab_ref[pl.ds(r, n_tok, stride=pitch), :] = jnp.zeros((n_tok, LANES), I32)


def _matmul_kernel(x_ref, w_ref, o_ref):
    o_ref[...] = jnp.dot(x_ref[...], w_ref[...], preferred_element_type=F32).astype(o_ref.dtype)


def _concat_cast_kernel(xa_ref, xb_ref, o_ref, *, steps1):
    @pl.when(pl.program_id(0) < steps1)
    def _():
        o_ref[...] = xa_ref[...].astype(o_ref.dtype)

    @pl.when(pl.program_id(0) >= steps1)
    def _():
        o_ref[...] = xb_ref[...].astype(o_ref.dtype)


def _concat_cast(xa, xb, dtype, tm=256):
    n1, D = xa.shape
    T = n1 + xb.shape[0]
    tm = _tile(math.gcd(n1, T - n1), tm)
    steps1 = n1 // tm
    return pl.pallas_call(
        functools.partial(_concat_cast_kernel, steps1=steps1),
        grid=(T // tm,),
        in_specs=[pl.BlockSpec((tm, D), lambda i: (jnp.minimum(i, steps1 - 1), 0)),
                  pl.BlockSpec((tm, D), lambda i: (jnp.maximum(i - steps1, 0), 0))],
        out_specs=pl.BlockSpec((tm, D), lambda i: (i, 0)),
        out_shape=jax.ShapeDtypeStruct((T, D), dtype),
        compiler_params=_params(("arbitrary",)),
        name="concat_cast",
    )(xa, xb)


def _matmul(x, w, out_dtype, tm=1024, tn=1024):
    M, K = x.shape
    N = w.shape[1]
    tm, tn = _tile(M, tm), _tile(N, tn)
    return pl.pallas_call(
        _matmul_kernel,
        grid=(M // tm, N // tn),
        in_specs=[pl.BlockSpec((tm, K), lambda i, j: (i, 0)),
                  pl.BlockSpec((K, tn), lambda i, j: (0, j))],
        out_specs=pl.BlockSpec((tm, tn), lambda i, j: (i, j)),
        out_shape=jax.ShapeDtypeStruct((M, N), out_dtype),
        compiler_params=_params(("arbitrary", "arbitrary")),
        name="proj_matmul",
    )(x, w)


def _residual_matmul_kernel(m_ref, w_ref, xa_ref, xb_ref, o_ref, *, steps1):
    mm = jnp.dot(m_ref[...], w_ref[...], preferred_element_type=F32)

    @pl.when(pl.program_id(0) < steps1)
    def _():
        o_ref[...] = ALPHA * xa_ref[...] + mm

    @pl.when(pl.program_id(0) >= steps1)
    def _():
        o_ref[...] = ALPHA * xb_ref[...] + mm


def _residual_matmul(merged, w_o, xa, xb, tm=1024, tn=512):
    T, K = merged.shape
    D = w_o.shape[1]
    n1 = xa.shape[0]
    tm, tn = _tile(math.gcd(n1, T - n1), tm), _tile(D, tn)
    steps1, nt = n1 // tm, D // tn
    return pl.pallas_call(
        functools.partial(_residual_matmul_kernel, steps1=steps1),
        grid=(T // tm, nt),
        in_specs=[pl.BlockSpec((tm, K), lambda i, n: (i, 0)),
                  pl.BlockSpec((K, tn), lambda i, n: (0, n)),
                  pl.BlockSpec((tm, tn), lambda i, n: (jnp.minimum(i, steps1 - 1), jnp.where(i < steps1, n, nt - 1))),
                  pl.BlockSpec((tm, tn), lambda i, n: (jnp.maximum(i - steps1, 0), jnp.where(i < steps1, 0, n)))],
        out_specs=pl.BlockSpec((tm, tn), lambda i, n: (i, n)),
        out_shape=jax.ShapeDtypeStruct((T, D), F32),
        compiler_params=_params(("arbitrary", "arbitrary")),
        name="out_proj_residual",
    )(merged, w_o, xa, xb)


def _attn_kernel(first_ref, last_ref, sink_ref, q_ref, kp_ref, kc_ref, kn_ref, vp_ref, vc_ref, vn_ref,
                 bias_ref, o_ref, *, kv_heads, group, scale):
    n = pl.program_id(0)
    neg_inf = jnp.float32(-jnp.inf)
    pen_p = jnp.where(first_ref[n] == 1, neg_inf, jnp.float32(0.0))
    pen_n = jnp.where(last_ref[n] == 1, neg_inf, jnp.float32(0.0))
    lane = lax.broadcasted_iota(I32, (1, 3 * BLOCK), 1)
    pen = jnp.where(lane < BLOCK, pen_p, jnp.where(lane >= 2 * BLOCK, pen_n, jnp.float32(0.0)))
    for h in range(kv_heads):
        cols = slice(h * HEAD_DIM, (h + 1) * HEAD_DIM)
        kcat = jnp.concatenate([kp_ref[:, cols], kc_ref[:, cols], kn_ref[:, cols]], axis=0).astype(BF16)
        vcat = jnp.concatenate([vp_ref[:, cols], vc_ref[:, cols], vn_ref[:, cols]], axis=0).astype(BF16)
        heads = [h * group + g for g in range(group)]
        qcols = [slice(hd * HEAD_DIM, (hd + 1) * HEAD_DIM) for hd in heads]
        sinks = [sink_ref[hd] for hd in heads]
        s = [lax.dot_general(q_ref[:, c].astype(BF16), kcat, NT_DIMS, preferred_element_type=F32) * scale
             + bias_ref[hd] + pen for hd, c in zip(heads, qcols)]
        m = [jnp.maximum(jnp.max(si, axis=1, keepdims=True), sk) for si, sk in zip(s, sinks)]
        e = [jnp.exp(si - mi) for si, mi in zip(s, m)]
        denom = [jnp.sum(ei, axis=1, keepdims=True) + jnp.exp(sk - mi) for ei, sk, mi in zip(e, sinks, m)]
        p = [(ei * (1.0 / di)).astype(BF16) for ei, di in zip(e, denom)]
        for c, pi in zip(qcols, p):
            o_ref[:, c] = jnp.dot(pi, vcat, preferred_element_type=F32).astype(o_ref.dtype)


def _t5_bucket(rel):
    nb = REL_BUCKETS // 2
    max_exact = nb // 2
    ret = (rel > 0).astype(I32) * nb
    n = jnp.abs(rel)
    nf = jnp.maximum(n, max_exact).astype(F32)
    large = max_exact + (jnp.log(nf / max_exact) / math.log(REL_MAX_DIST / max_exact) * (nb - max_exact)).astype(I32)
    large = jnp.minimum(large, nb - 1)
    return ret + jnp.where(n < max_exact, n, large)


def _attention(proj, first, last, sink, rel_table, col_q, col_k, col_v):
    T = proj.shape[0]
    nb = T // BLOCK
    group = ATT_HEADS // ATT_KV_HEADS
    qw, kw = ATT_HEADS * HEAD_DIM, ATT_KV_HEADS * HEAD_DIM
    assert col_q % qw == 0 and col_k % kw == 0 and col_v % kw == 0
    qo, ko, vo = col_q // qw, col_k // kw, col_v // kw
    rel = jnp.arange(3 * BLOCK)[None, :] - BLOCK - jnp.arange(BLOCK)[:, None]
    onehot = (_t5_bucket(rel)[:, :, None] == jnp.arange(REL_BUCKETS)[None, None, :]).astype(F32)
    bias = jnp.einsum('qkb,bh->hqk', onehot, rel_table.astype(F32), precision=lax.Precision.HIGHEST)
    bias = jnp.where((jnp.abs(rel) <= WINDOW)[None, :, :], bias, -jnp.inf)

    def kv_spec(off, shift):
        def imap(n, first, last):
            return (jnp.clip(n + shift, 0, nb - 1), off)
        return pl.BlockSpec((BLOCK, kw), imap)

    grid_spec = pltpu.PrefetchScalarGridSpec(
        num_scalar_prefetch=2,
        grid=(nb,),
        in_specs=[pl.BlockSpec(memory_space=pltpu.SMEM),
                  pl.BlockSpec((BLOCK, qw), lambda n, first, last: (n, qo)),
                  kv_spec(ko, -1), kv_spec(ko, 0), kv_spec(ko, 1),
                  kv_spec(vo, -1), kv_spec(vo, 0), kv_spec(vo, 1),
                  pl.BlockSpec((ATT_HEADS, BLOCK, 3 * BLOCK), lambda n, first, last: (0, 0, 0))],
        out_specs=pl.BlockSpec((BLOCK, qw), lambda n, first, last: (n, 0)),
    )
    return pl.pallas_call(
        functools.partial(_attn_kernel, kv_heads=ATT_KV_HEADS, group=group, scale=HEAD_DIM ** -0.5),
        grid_spec=grid_spec,
        out_shape=jax.ShapeDtypeStruct((T, qw), BF16),
        compiler_params=_params(("arbitrary",)),
        name="window_attention",
    )(first, last, sink.astype(F32), proj, proj, proj, proj, proj, proj, proj, bias)


def _rope(x, cos2, sin2):
    return x * cos2 + pltpu.roll(x, RET_DK // 2, axis=1) * sin2


def _ret_bwd_kernel(last_ref, posblk_ref, cdec_ref, q_ref, k_ref, v_ref, cos_ref, sin_ref, qb_ref, kb_ref, o_ref,
                    state_ref, *, chunks):
    h = pl.program_id(0)
    n = pl.num_programs(1) - 1 - pl.program_id(1)
    C = RET_CHUNK

    @pl.when(last_ref[n] == 1)
    def _():
        state_ref[...] = jnp.zeros_like(state_ref)

    state = state_ref[...]
    cdec = cdec_ref[1, h]
    for c in range(chunks - 1, -1, -1):
        sl = slice(c * C, (c + 1) * C)
        cos2, sin2 = cos_ref[sl, :], sin_ref[sl, :]
        q = _rope(q_ref[sl, :], cos2, sin2)
        k = _rope(k_ref[sl, :], cos2, sin2) * (RET_DK ** -0.5)
        vb = v_ref[sl, :].astype(BF16)
        o_ref[sl, :] = jnp.dot((q * qb_ref[0]).astype(BF16), state.astype(BF16), preferred_element_type=F32)
        state = state * cdec + lax.dot_general((k * kb_ref[0]).astype(BF16), vb, TN_DIMS,
                                               preferred_element_type=F32)
    state_ref[...] = state


def _ret_fwd_kernel(first_ref, posblk_ref, cdec_ref, q_ref, k_ref, v_ref, g_ref, cos_ref, sin_ref, dmat_ref, qf_ref,
                    kf_ref, ob_ref, gnw_ref, o_ref, state_ref, *, chunks):
    h = pl.program_id(0)
    n = pl.program_id(1)
    C = RET_CHUNK

    @pl.when(first_ref[n] == 1)
    def _():
        state_ref[...] = jnp.zeros_like(state_ref)

    state = state_ref[...]
    cdec = cdec_ref[0, h]
    for c in range(chunks):
        sl = slice(c * C, (c + 1) * C)
        cos2, sin2 = cos_ref[sl, :], sin_ref[sl, :]
        q = _rope(q_ref[sl, :], cos2, sin2)
        k = _rope(k_ref[sl, :], cos2, sin2) * (RET_DK ** -0.5)
        vb = v_ref[sl, :].astype(BF16)
        intra = lax.dot_general(q.astype(BF16), k.astype(BF16), NT_DIMS, preferred_element_type=F32) * dmat_ref[0]
        out = (jnp.dot(intra.astype(BF16), vb, preferred_element_type=F32)
               + jnp.dot((q * qf_ref[0]).astype(BF16), state.astype(BF16), preferred_element_type=F32)
               + ob_ref[sl, :])
        state = state * cdec + lax.dot_general((k * kf_ref[0]).astype(BF16), vb, TN_DIMS,
                                               preferred_element_type=F32)
        mu = jnp.mean(out, axis=1, keepdims=True)
        cen = out - mu
        var = jnp.mean(cen * cen, axis=1, keepdims=True)
        y = cen * lax.rsqrt(var + GN_EPS) * gnw_ref[...]
        o_ref[sl, :] = (_silu(g_ref[sl, :]) * y).astype(o_ref.dtype)
    state_ref[...] = state


def _retention(proj, first, last, posblk, chunks, ret_decay, ret_gn_w, s_max, col_q, col_k, col_v, col_g):
    T = proj.shape[0]
    C = RET_CHUNK
    R = chunks * C
    nc = T // R
    H = RET_HEADS
    qo, ko, vo, go = col_q // RET_DK, col_k // RET_DK, col_v // RET_DV, col_g // RET_DV
    assert col_q % RET_DK == 0 and col_k % RET_DK == 0 and col_v % RET_DV == 0 and col_g % RET_DV == 0

    half = RET_DK // 2
    inv = ROPE_BASE ** (-jnp.arange(half, dtype=F32) * 2.0 / RET_DK)
    ang = jnp.arange(s_max, dtype=F32)[:, None] * inv[None, :]
    cos2 = jnp.concatenate([jnp.cos(ang), jnp.cos(ang)], axis=1)
    sin2 = jnp.concatenate([-jnp.sin(ang), jnp.sin(ang)], axis=1)

    lg = -jnp.exp(ret_decay.astype(F32))
    pos = jnp.arange(C, dtype=F32)
    diff = pos[:, None] - pos[None, :]
    dec_f = jnp.where((diff >= 0)[None], jnp.exp(jnp.maximum(diff, 0.0)[None] * lg[0][:, None, None]), 0.0)
    dec_b = jnp.where((diff < 0)[None], jnp.exp(jnp.maximum(-diff, 0.0)[None] * lg[1][:, None, None]), 0.0)
    dmat = dec_f + dec_b

    def rows(tab):
        return jnp.broadcast_to(tab[:, :, None], (H, C, RET_DK))

    qf = rows(jnp.exp((pos[None, :] + 1.0) * lg[0][:, None]))
    kf = rows(jnp.exp((C - 1.0 - pos)[None, :] * lg[0][:, None]))
    qb = rows(jnp.exp((C - pos)[None, :] * lg[1][:, None]))
    kb = rows(jnp.exp(pos[None, :] * lg[1][:, None]))
    cdec = jnp.exp(C * lg)

    smem = pl.BlockSpec(memory_space=pltpu.SMEM)
    rev = lambda c: nc - 1 - c
    bwd_specs = [
        smem,
        pl.BlockSpec((R, RET_DK), lambda h, c, last, posb: (rev(c), qo + h)),
        pl.BlockSpec((R, RET_DK), lambda h, c, last, posb: (rev(c), ko + h)),
        pl.BlockSpec((R, RET_DV), lambda h, c, last, posb: (rev(c), vo + h)),
        pl.BlockSpec((R, RET_DK), lambda h, c, last, posb: (posb[rev(c)], 0)),
        pl.BlockSpec((R, RET_DK), lambda h, c, last, posb: (posb[rev(c)], 0)),
        pl.BlockSpec((1, C, RET_DK), lambda h, c, last, posb: (h, 0, 0)),
        pl.BlockSpec((1, C, RET_DK), lambda h, c, last, posb: (h, 0, 0)),
    ]
    out_b = pl.pallas_call(
        functools.partial(_ret_bwd_kernel, chunks=chunks),
        grid_spec=pltpu.PrefetchScalarGridSpec(
            num_scalar_prefetch=2, grid=(H, nc), in_specs=bwd_specs,
            out_specs=pl.BlockSpec((R, RET_DV), lambda h, c, last, posb: (rev(c), h)),
            scratch_shapes=[pltpu.VMEM((RET_DK, RET_DV), F32)]),
        out_shape=jax.ShapeDtypeStruct((T, H * RET_DV), F32),
        compiler_params=_params(("arbitrary", "arbitrary")),
        name="retention_backward",
    )(last, posblk, cdec, proj, proj, proj, cos2, sin2, qb, kb)

    fwd_specs = [
        smem,
        pl.BlockSpec((R, RET_DK), lambda h, n, first, posb: (n, qo + h)),
        pl.BlockSpec((R, RET_DK), lambda h, n, first, posb: (n, ko + h)),
        pl.BlockSpec((R, RET_DV), lambda h, n, first, posb: (n, vo + h)),
        pl.BlockSpec((R, RET_DV), lambda h, n, first, posb: (n, go + h)),
        pl.BlockSpec((R, RET_DK), lambda h, n, first, posb: (posb[n], 0)),
        pl.BlockSpec((R, RET_DK), lambda h, n, first, posb: (posb[n], 0)),
        pl.BlockSpec((1, C, C), lambda h, n, first, posb: (h, 0, 0)),
        pl.BlockSpec((1, C, RET_DK), lambda h, n, first, posb: (h, 0, 0)),
        pl.BlockSpec((1, C, RET_DK), lambda h, n, first, posb: (h, 0, 0)),
        pl.BlockSpec((R, RET_DV), lambda h, n, first, posb: (n, h)),
        pl.BlockSpec((1, RET_DV), lambda h, n, first, posb: (0, h)),
    ]
    return pl.pallas_call(
        functools.partial(_ret_fwd_kernel, chunks=chunks),
        grid_spec=pltpu.PrefetchScalarGridSpec(
            num_scalar_prefetch=2, grid=(H, nc), in_specs=fwd_specs,
            out_specs=pl.BlockSpec((R, RET_DV), lambda h, n, first, posb: (n, h)),
            scratch_shapes=[pltpu.VMEM((RET_DK, RET_DV), F32)]),
        out_shape=jax.ShapeDtypeStruct((T, H * RET_DV), BF16),
        compiler_params=_params(("arbitrary", "arbitrary")),
        name="retention_forward",
    )(first, posblk, cdec, proj, proj, proj, proj, cos2, sin2, dmat, qf, kf, out_b,
      ret_gn_w.astype(F32).reshape(1, -1))


def _merge_kernel(att_ref, wa_ref, ret_ref, wr_ref, ga_ref, gr_ref, o_ref):
    a = jnp.dot(att_ref[...], wa_ref[...], preferred_element_type=F32)
    r = jnp.dot(ret_ref[...], wr_ref[...], preferred_element_type=F32)
    o_ref[...] = (jax.nn.sigmoid(ga_ref[...]) * a + jax.nn.sigmoid(gr_ref[...]) * r).astype(o_ref.dtype)


def _merge(att, ret, proj, w_att_out, w_ret_out, col_ga, col_gr, tm=1024, tn=512):
    T = att.shape[0]
    D = w_att_out.shape[1]
    tm, tn = _tile(T, tm), _tile(D, tn)
    assert col_ga % tn == 0 and col_gr % tn == 0
    ao, ro = col_ga // tn, col_gr // tn
    return pl.pallas_call(
        _merge_kernel,
        grid=(T // tm, D // tn),
        in_specs=[pl.BlockSpec((tm, att.shape[1]), lambda i, j: (i, 0)),
                  pl.BlockSpec((w_att_out.shape[0], tn), lambda i, j: (0, j)),
                  pl.BlockSpec((tm, ret.shape[1]), lambda i, j: (i, 0)),
                  pl.BlockSpec((w_ret_out.shape[0], tn), lambda i, j: (0, j)),
                  pl.BlockSpec((tm, tn), lambda i, j: (i, ao + j)),
                  pl.BlockSpec((tm, tn), lambda i, j: (i, ro + j))],
        out_specs=pl.BlockSpec((tm, tn), lambda i, j: (i, j)),
        out_shape=jax.ShapeDtypeStruct((T, D), BF16),
        compiler_params=_params(("arbitrary", "arbitrary")),
        name="branch_merge",
    )(att, w_att_out, ret, w_ret_out, proj, proj)


def _layernorm_rows(z, w, b):
    mu = jnp.mean(z, axis=1, keepdims=True)
    cen = z - mu
    var = jnp.mean(cen * cen, axis=1, keepdims=True)
    return cen * lax.rsqrt(var + LN_EPS) * w + b


def _router_kernel(z_ref, lw_ref, lb_ref, wh_ref, wl_ref, bias_ref, idx_ref, rank_ref, gate_ref, cnt_ref, x1_ref,
                   slab_ref, xb_ref, carry_ref, *, rows, pitch):
    i = pl.program_id(0)
    tm = z_ref.shape[0]
    E = N_EXPERTS
    gsz = E // N_GROUPS
    neg_inf = jnp.float32(-jnp.inf)

    @pl.when(i == 0)
    def _():
        carry_ref[...] = jnp.zeros_like(carry_ref)

    x1_ref[...] = _layernorm_rows(z_ref[...], lw_ref[...], lb_ref[...])
    _store_slabs(slab_ref, lambda c: x1_ref[:, c * LANES:(c + 1) * LANES], tm, rows, pitch)
    x = x1_ref[...]
    xh = x.astype(BF16)
    xb_ref[...] = xh
    xl = (x - xh.astype(F32)).astype(BF16)
    logits = (lax.dot_general(wh_ref[...], xh, NT_DIMS, preferred_element_type=F32)
              + lax.dot_general(wh_ref[...], xl, NT_DIMS, preferred_element_type=F32)
              + lax.dot_general(wl_ref[...], xh, NT_DIMS, preferred_element_type=F32))
    scores = jax.nn.sigmoid(logits)
    biased = scores + bias_ref[...]

    row = lax.broadcasted_iota(I32, (gsz, tm), 0)
    gscore = []
    for g in range(N_GROUPS):
        blk = biased[g * gsz:(g + 1) * gsz, :]
        m1 = jnp.max(blk, axis=0, keepdims=True)
        first = jnp.min(jnp.where(blk == m1, row, gsz), axis=0, keepdims=True)
        m2 = jnp.max(jnp.where(row == first, neg_inf, blk), axis=0, keepdims=True)
        gscore.append(m1 + m2)
    gid = lax.broadcasted_iota(I32, (E, tm), 0) // gsz
    keep = jnp.zeros((E, tm), I32)
    for g in range(N_GROUPS):
        beaten = jnp.zeros((1, tm), I32)
        for o in range(N_GROUPS):
            if o == g:
                continue
            wins = (gscore[o] >= gscore[g]) if o < g else (gscore[o] > gscore[g])
            beaten = beaten + wins.astype(I32)
        keep = jnp.where(gid == g, (beaten < TOPK_GROUPS).astype(I32), keep)
    masked = jnp.where(keep > 0, biased, neg_inf)

    eidx = lax.broadcasted_iota(I32, (E, tm), 0)
    sel = jnp.zeros((E, tm), jnp.bool_)
    picks, weights = [], []
    for _ in range(TOP_K):
        m = jnp.max(masked, axis=0, keepdims=True)
        pick = jnp.min(jnp.where(masked == m, eidx, E), axis=0, keepdims=True)
        onehot = eidx == pick
        weights.append(jnp.sum(jnp.where(onehot, scores, 0.0), axis=0, keepdims=True))
        picks.append(pick)
        masked = jnp.where(onehot, neg_inf, masked)
        sel = sel | onehot
    wsum = weights[0]
    for w in weights[1:]:
        wsum = wsum + w

    self32 = sel.astype(F32)
    tri = (lax.broadcasted_iota(I32, (tm, tm), 0) < lax.broadcasted_iota(I32, (tm, tm), 1)).astype(F32).astype(BF16)
    prefix = jnp.dot(self32.astype(BF16), tri, preferred_element_type=F32) + carry_ref[...]
    for kk in range(TOP_K):
        onehot = eidx == picks[kk]
        idx_ref[kk:kk + 1, :] = picks[kk]
        rank_ref[kk:kk + 1, :] = jnp.sum(jnp.where(onehot, prefix, 0.0), axis=0, keepdims=True).astype(I32)
        gate_ref[kk:kk + 1, :] = weights[kk] / wsum * ROUTED_SCALE
    carry_ref[...] = carry_ref[...] + jnp.sum(self32, axis=1, keepdims=True)
    cnt_ref[...] = carry_ref[...]


def _ln_router(z, ln_w, ln_b, w_router, router_bias, rows, pitch, tm=256):
    T, D = z.shape
    E = N_EXPERTS
    tm = _tile(T, tm)
    wt = w_router.astype(F32).T
    wh = wt.astype(BF16)
    wl = (wt - wh.astype(F32)).astype(BF16)
    return pl.pallas_call(
        functools.partial(_router_kernel, rows=rows, pitch=pitch),
        grid=(T // tm,),
        in_specs=[pl.BlockSpec((tm, D), lambda i: (i, 0)),
                  pl.BlockSpec((1, D), lambda i: (0, 0)),
                  pl.BlockSpec((1, D), lambda i: (0, 0)),
                  pl.BlockSpec((E, D), lambda i: (0, 0)),
                  pl.BlockSpec((E, D), lambda i: (0, 0)),
                  pl.BlockSpec((E, 1), lambda i: (0, 0))],
        out_specs=[pl.BlockSpec((TOP_K, tm), lambda i: (0, i)),
                   pl.BlockSpec((TOP_K, tm), lambda i: (0, i)),
                   pl.BlockSpec((TOP_K, tm), lambda i: (0, i)),
                   pl.BlockSpec((E, 1), lambda i: (0, 0)),
                   pl.BlockSpec((tm, D), lambda i: (i, 0)),
                   pl.BlockSpec((tm * pitch, LANES), lambda i: (i, 0)),
                   pl.BlockSpec((tm, D), lambda i: (i, 0))],
        out_shape=[jax.ShapeDtypeStruct((TOP_K, T), I32),
                   jax.ShapeDtypeStruct((TOP_K, T), I32),
                   jax.ShapeDtypeStruct((TOP_K, T), F32),
                   jax.ShapeDtypeStruct((E, 1), F32),
                   jax.ShapeDtypeStruct((T, D), F32),
                   jax.ShapeDtypeStruct((T * pitch, LANES), I32),
                   jax.ShapeDtypeStruct((T, D), BF16)],
        scratch_shapes=[pltpu.VMEM((E, 1), F32)],
        compiler_params=_params(("arbitrary",)),
        name="layernorm_router",
    )(z, ln_w.astype(F32).reshape(1, -1), ln_b.astype(F32).reshape(1, -1), wh, wl,
      router_bias.astype(F32).reshape(E, 1))


def _slot_kernel(start_ref, idx_ref, rank_ref, slot_ref):
    idx = idx_ref[...]
    acc = rank_ref[...]
    for e in range(N_EXPERTS):
        acc = acc + jnp.where(idx == e, start_ref[e], 0)
    slot_ref[...] = acc


def _slots(pad_start, idx, rank, tm=2048):
    T = idx.shape[1]
    tm = _tile(T, tm)
    return pl.pallas_call(
        _slot_kernel,
        grid=(T // tm,),
        in_specs=[pl.BlockSpec(memory_space=pltpu.SMEM),
                  pl.BlockSpec((TOP_K, tm), lambda i: (0, i)),
                  pl.BlockSpec((TOP_K, tm), lambda i: (0, i))],
        out_specs=pl.BlockSpec((TOP_K, tm), lambda i: (0, i)),
        out_shape=jax.ShapeDtypeStruct((TOP_K, T), I32),
        compiler_params=_params(("arbitrary",)),
        name="slot_index",
    )(pad_start, idx, rank)


def _zero_fill_plan(npad, start, te, fn):
    pos = start
    bit = te // 2
    while bit >= 1:
        take = (npad & bit) != 0

        @pl.when(take)
        def _(pos=pos, bit=bit):
            fn(pos, bit)

        pos = pos + jnp.where(take, bit, 0)
        bit //= 2


def _dispatch_kernel(fill_start_ref, fill_len_ref, slot_ref, x_ref, xs_ref, zero_ref, sem, zsem, *, te, rows, pitch):
    i = pl.program_id(0)
    tm = x_ref.shape[0] // pitch

    def start_token(t, carry):
        src = x_ref.at[pl.ds(pl.multiple_of(t * pitch, SUBLANES), rows)]
        for k in range(TOP_K):
            dst = xs_ref.at[pl.ds(pl.multiple_of(slot_ref[t * TOP_K + k] * pitch, SUBLANES), rows)]
            pltpu.make_async_copy(src, dst, sem).start(priority=k % 2)
        return carry

    lax.fori_loop(0, tm, start_token, 0)

    @pl.when(i == 0)
    def _():
        zero_ref[...] = jnp.zeros_like(zero_ref)

        def zero_copy(pos, n):
            return pltpu.make_async_copy(zero_ref.at[pl.ds(0, n * pitch)],
                                         xs_ref.at[pl.ds(pl.multiple_of(pos * pitch, SUBLANES), n * pitch)], zsem)

        def start_fill(e, carry):
            _zero_fill_plan(fill_len_ref[e], fill_start_ref[e], te, lambda pos, n: zero_copy(pos, n).start())
            return carry

        def wait_fill(e, carry):
            _zero_fill_plan(fill_len_ref[e], fill_start_ref[e], te, lambda pos, n: zero_copy(pos, n).wait())
            return carry

        lax.fori_loop(0, N_EXPERTS, start_fill, 0)
        lax.fori_loop(0, N_EXPERTS, wait_fill, 0)

    def wait_token(t, carry):
        for k in range(TOP_K):
            pltpu.make_async_copy(x_ref.at[pl.ds(0, rows)], xs_ref.at[pl.ds(0, rows)], sem).wait()
        return carry

    lax.fori_loop(0, tm, wait_token, 0)


def _dispatch(x_slab, slot_flat, fill_start, fill_len, n_slots, te, rows, pitch, tm=256):
    T = x_slab.shape[0] // pitch
    tm = _tile(T, tm)
    return pl.pallas_call(
        functools.partial(_dispatch_kernel, te=te, rows=rows, pitch=pitch),
        grid_spec=pltpu.PrefetchScalarGridSpec(
            num_scalar_prefetch=2,
            grid=(T // tm,),
            in_specs=[pl.BlockSpec((tm * TOP_K,), lambda i, fs, fl: (i,), memory_space=pltpu.SMEM),
                      pl.BlockSpec((tm * pitch, LANES), lambda i, fs, fl: (i, 0))],
            out_specs=pl.BlockSpec(memory_space=pl.ANY),
            scratch_shapes=[pltpu.VMEM((te // 2 * pitch, LANES), I32),
                            pltpu.SemaphoreType.DMA(()),
                            pltpu.SemaphoreType.DMA(())]),
        out_shape=jax.ShapeDtypeStruct((n_slots * pitch, LANES), I32),
        compiler_params=_params(("arbitrary",)),
        name="moe_dispatch",
    )(fill_start, fill_len, slot_flat, x_slab)


def _ffn_partial(xb, wg, wu, wd):
    hid = _silu(jnp.dot(xb, wg, preferred_element_type=F32)) * jnp.dot(xb, wu, preferred_element_type=F32)
    return jnp.dot(hid.astype(BF16), wd, preferred_element_type=F32)


def _expert_up_kernel(blk_ref, chunk_ref, first_ref, exp_ref, e1_ref, c1_ref, e2_ref, c2_ref, e3_ref, c3_ref, nsteps_ref,
                      x_ref, wg0_ref, wg1_ref, wu0_ref, wu1_ref, h_ref, wgb_ref, wub_ref, *, rows, pitch):
    s = pl.program_id(0)
    te = h_ref.shape[0]

    @pl.when(s < nsteps_ref[0])
    def _():
        @pl.when(first_ref[s] == 1)
        def _():
            kh = wg0_ref.shape[0]
            wgb_ref[0:kh, :] = wg0_ref[...].astype(BF16)
            wgb_ref[kh:, :] = wg1_ref[...].astype(BF16)
            wub_ref[0:kh, :] = wu0_ref[...].astype(BF16)
            wub_ref[kh:, :] = wu1_ref[...].astype(BF16)

        pairs = [_unpack_pair(x_ref[pl.ds(r, te, stride=pitch), :]) for r in range(rows)]
        xb = jnp.concatenate([lo.astype(BF16) for lo, _ in pairs] + [hi.astype(BF16) for _, hi in pairs], axis=1)
        hid = (_silu(jnp.dot(xb, wgb_ref[...], preferred_element_type=F32))
               * jnp.dot(xb, wub_ref[...], preferred_element_type=F32))
        h_ref[...] = hid.astype(h_ref.dtype)


def _expert_down_kernel(exp_ref, exp_look_ref, first_ref, nvalid_ref, h_ref, wda_ref, wdb_half_ref, o_ref, wdb_ref, *,
                        rows, pitch):
    b = pl.program_id(0)
    te = h_ref.shape[0]
    half = wdb_ref.shape[1] // 2
    khalf = wda_ref.shape[0]
    group = min(4, rows)

    @pl.when(b < nvalid_ref[0])
    def _():
        @pl.when(first_ref[b] == 1)
        def _():
            wdb_ref[0:khalf, :] = wda_ref[...].astype(BF16)
            wdb_ref[khalf:, :] = wdb_half_ref[...].astype(BF16)

        hid = h_ref[...]
        for g0 in range(0, rows, group):
            width = group * LANES
            ylo = jnp.dot(hid, wdb_ref[:, g0 * LANES:g0 * LANES + width], preferred_element_type=F32)
            yhi = jnp.dot(hid, wdb_ref[:, half + g0 * LANES:half + g0 * LANES + width], preferred_element_type=F32)
            for r in range(group):
                cols = slice(r * LANES, (r + 1) * LANES)
                o_ref[pl.ds(g0 + r, te, stride=pitch), :] = _pack_pair(ylo[:, cols], yhi[:, cols])
        for r in range(rows, pitch):
            o_ref[pl.ds(r, te, stride=pitch), :] = jnp.zeros((te, LANES), I32)


def _routed_ffn(xs, blk_e, nvalid, pad_start, padded, wg, wu, wd, te, rows, pitch, tj=512):
    P = xs.shape[0] // pitch
    E, D, DE = wg.shape
    tj = _tile(DE, tj)
    nj = DE // tj
    nblk = P // te
    nsteps_max = nblk * nj

    b = jnp.arange(nblk, dtype=I32)
    first_blk = (pad_start // te).astype(I32)[blk_e]
    n_blk = (padded // te).astype(I32)[blk_e]
    step = nj * first_blk[None, :] + jnp.arange(nj, dtype=I32)[:, None] * n_blk[None, :] + (b - first_blk)[None, :]
    step = jnp.where((b < nvalid[0])[None, :], step, nsteps_max)
    nsteps = nvalid * nj
    blk_of = jnp.full((nsteps_max,), nvalid[0] - 1, I32).at[step].set(jnp.broadcast_to(b[None, :], step.shape),
                                                                      mode='drop')
    chunk_of = jnp.full((nsteps_max,), nj - 1, I32).at[step].set(
        jnp.broadcast_to(jnp.arange(nj, dtype=I32)[:, None], step.shape), mode='drop')
    first_of = jnp.zeros((nsteps_max,), I32).at[step].set(
        jnp.broadcast_to((b == first_blk).astype(I32)[None, :], step.shape), mode='drop')

    exp_of = blk_e[blk_of]
    sidx = jnp.arange(nsteps_max, dtype=I32)
    first_pos = lax.cummin(jnp.where(first_of == 1, sidx, nsteps_max), axis=0, reverse=True)
    next_first = jnp.concatenate([first_pos[1:], jnp.full((1,), nsteps_max, I32)])

    def look_ahead(tab, k):
        tgt = jnp.minimum(jnp.minimum(sidx + k, next_first), nsteps_max - 1)
        return jnp.where(first_of == 1, tab, tab[tgt])

    looks = []
    for k in (1, 2, 3):
        looks += [look_ahead(exp_of, k), look_ahead(chunk_of, k)]

    def w_spec(k, row_half):
        def imap(s, *pre):
            exp_tab, chunk_tab = (pre[3], pre[1]) if k == 0 else (pre[2 + 2 * k], pre[3 + 2 * k])
            return (exp_tab[s], row_half, chunk_tab[s])
        return pl.BlockSpec((None, D // 2, tj), imap)

    hid = pl.pallas_call(
        functools.partial(_expert_up_kernel, rows=rows, pitch=pitch),
        grid_spec=pltpu.PrefetchScalarGridSpec(
            num_scalar_prefetch=11,
            grid=(nsteps_max,),
            in_specs=[pl.BlockSpec((te * pitch, LANES), lambda s, *pre: (pre[0][s], 0)),
                      w_spec(0, 0), w_spec(1, 1), w_spec(2, 0), w_spec(3, 1)],
            out_specs=pl.BlockSpec((te, tj), lambda s, *pre: (pre[0][s], pre[1][s])),
            scratch_shapes=[pltpu.VMEM((D, tj), BF16), pltpu.VMEM((D, tj), BF16)]),
        out_shape=jax.ShapeDtypeStruct((P, DE), BF16),
        compiler_params=_params(("arbitrary",), V7X_VMEM_EXPERT_BYTES),
        name="routed_experts_up",
    )(blk_of, chunk_of, first_of, exp_of, *looks, nsteps, xs, wg, wg, wu, wu)

    def bclamp(b, nv):
        return jnp.minimum(b, nv[0] - 1)

    bc = jnp.minimum(b, nvalid[0] - 1)
    exp_d = blk_e[bc]
    first_down = jnp.concatenate([jnp.ones((1,), I32), (exp_d[1:] != exp_d[:-1]).astype(I32)])
    exp_d_look = jnp.where(first_down == 1, exp_d, jnp.concatenate([exp_d[1:], exp_d[-1:]]))
    return pl.pallas_call(
        functools.partial(_expert_down_kernel, rows=rows, pitch=pitch),
        grid_spec=pltpu.PrefetchScalarGridSpec(
            num_scalar_prefetch=4,
            grid=(nblk,),
            in_specs=[pl.BlockSpec((te, DE), lambda b, ed, el, fd, nv: (bclamp(b, nv), 0)),
                      pl.BlockSpec((None, DE // 2, D), lambda b, ed, el, fd, nv: (ed[b], 0, 0)),
                      pl.BlockSpec((None, DE // 2, D), lambda b, ed, el, fd, nv: (el[b], 1, 0))],
            out_specs=pl.BlockSpec((te * pitch, LANES), lambda b, ed, el, fd, nv: (bclamp(b, nv), 0)),
            scratch_shapes=[pltpu.VMEM((DE, D), BF16)]),
        out_shape=jax.ShapeDtypeStruct((P * pitch, LANES), I32),
        compiler_params=_params(("arbitrary",), V7X_VMEM_EXPERT_BYTES),
        name="routed_experts_down",
    )(exp_d, exp_d_look, first_down, nvalid, hid, wd, wd)


def _shared_kernel(x_ref, wg_ref, wu_ref, wd_ref, o_ref):
    @pl.when(pl.program_id(1) == 0)
    def _():
        o_ref[...] = jnp.zeros_like(o_ref)

    o_ref[...] += _ffn_partial(x_ref[...], wg_ref[...], wu_ref[...], wd_ref[...])


def _shared_ffn(xb, wg, wu, wd, tm=512, tj=512):
    T, D = xb.shape
    DS = wg.shape[1]
    tm, tj = _tile(T, tm), _tile(DS, tj)
    return pl.pallas_call(
        _shared_kernel,
        grid=(T // tm, DS // tj),
        in_specs=[pl.BlockSpec((tm, D), lambda i, j: (i, 0)),
                  pl.BlockSpec((D, tj), lambda i, j: (0, j)),
                  pl.BlockSpec((D, tj), lambda i, j: (0, j)),
                  pl.BlockSpec((tj, D), lambda i, j: (j, 0))],
        out_specs=pl.BlockSpec((tm, D), lambda i, j: (i, 0)),
        out_shape=jax.ShapeDtypeStruct((T, D), F32),
        compiler_params=_params(("arbitrary", "arbitrary")),
        name="shared_expert",
    )(xb, wg, wu, wd)


def _combine_kernel(slot_ref, slot_next_ref, gate_ref, x_ref, sh_ref, lw_ref, lb_ref, ys_ref, o1_ref, o2_ref,
                    buf_ref, z_ref, sem, *, rows, pitch, steps1):
    i = pl.program_id(0)
    nsteps = pl.num_programs(0)
    tm, d_model = x_ref.shape
    half = d_model // 2
    cur = i % 2

    def start_all(srefs, buf):
        def body(t, carry):
            for kk in range(TOP_K):
                src = ys_ref.at[pl.ds(pl.multiple_of(srefs[t * TOP_K + kk] * pitch, SUBLANES), rows)]
                dst = buf_ref.at[buf, kk, pl.ds(pl.multiple_of(t * pitch, SUBLANES), rows)]
                pltpu.make_async_copy(src, dst, sem.at[buf]).start(priority=kk % 2)
            return carry
        lax.fori_loop(0, tm, body, 0)

    @pl.when(i == 0)
    def _():
        start_all(slot_ref, 0)

    @pl.when(i + 1 < nsteps)
    def _():
        start_all(slot_next_ref, 1 - cur)

    def wait_body(t, carry):
        for kk in range(TOP_K):
            pltpu.make_async_copy(ys_ref.at[pl.ds(0, rows)], buf_ref.at[cur, 0, pl.ds(0, rows)], sem.at[cur]).wait()
        return carry

    lax.fori_loop(0, tm, wait_body, 0)

    gates = gate_ref[...]
    gk = [gates[:, kk:kk + 1] for kk in range(TOP_K)]
    for r in range(rows):
        lo_acc = hi_acc = None
        for kk in range(TOP_K):
            lo, hi = _unpack_pair(buf_ref[cur, kk, pl.ds(r, tm, stride=pitch), :])
            lo_acc = gk[kk] * lo if lo_acc is None else lo_acc + gk[kk] * lo
            hi_acc = gk[kk] * hi if hi_acc is None else hi_acc + gk[kk] * hi
        for c0, acc in ((r * LANES, lo_acc), (half + r * LANES, hi_acc)):
            cols = slice(c0, c0 + LANES)
            z_ref[:, cols] = ALPHA * x_ref[:, cols] + (acc + sh_ref[:, cols])
    y = _layernorm_rows(z_ref[...], lw_ref[...], lb_ref[...])

    @pl.when(i < steps1)
    def _():
        o1_ref[...] = y

    @pl.when(i >= steps1)
    def _():
        o2_ref[...] = y


def _combine(ys, slot_flat, gate_tk, x1, shared, ln_w, ln_b, n1, rows, pitch, tm=128):
    T, D = x1.shape
    tm = _tile(math.gcd(n1, T - n1), tm)
    nsteps = T // tm
    steps1 = n1 // tm
    return pl.pallas_call(
        functools.partial(_combine_kernel, rows=rows, pitch=pitch, steps1=steps1),
        grid=(nsteps,),
        in_specs=[pl.BlockSpec((tm * TOP_K,), lambda i: (i,), memory_space=pltpu.SMEM),
                  pl.BlockSpec((tm * TOP_K,), lambda i: (jnp.minimum(i + 1, nsteps - 1),), memory_space=pltpu.SMEM),
                  pl.BlockSpec((tm, TOP_K), lambda i: (i, 0)),
                  pl.BlockSpec((tm, D), lambda i: (i, 0)),
                  pl.BlockSpec((tm, D), lambda i: (i, 0)),
                  pl.BlockSpec((1, D), lambda i: (0, 0)),
                  pl.BlockSpec((1, D), lambda i: (0, 0)),
                  pl.BlockSpec(memory_space=pl.ANY)],
        out_specs=[pl.BlockSpec((tm, D), lambda i: (jnp.minimum(i, steps1 - 1), 0)),
                   pl.BlockSpec((tm, D), lambda i: (jnp.maximum(i - steps1, 0), 0))],
        out_shape=[jax.ShapeDtypeStruct((n1, D), F32), jax.ShapeDtypeStruct((T - n1, D), F32)],
        scratch_shapes=[pltpu.VMEM((2, TOP_K, tm * pitch, LANES), I32), pltpu.VMEM((tm, D), F32),
                        pltpu.SemaphoreType.DMA((2,))],
        compiler_params=_params(("arbitrary",)),
        name="moe_combine_layernorm",
    )(slot_flat, slot_flat, gate_tk, x1, shared, ln_w.astype(F32).reshape(1, -1), ln_b.astype(F32).reshape(1, -1),
      ys)


def _seq_tables(groups, blk):
    first, last, pos = [], [], []
    for nseq, slen in groups:
        nb = slen // blk
        for _ in range(nseq):
            for b in range(nb):
                first.append(int(b == 0))
                last.append(int(b == nb - 1))
                pos.append(b)
    return (jnp.asarray(np.array(first, np.int32)), jnp.asarray(np.array(last, np.int32)),
            jnp.asarray(np.array(pos, np.int32)))


def _expert_block_rows(d_model):
    return 512 if d_model >= 4096 else 128


def kernel(x_prompt, x_sample, rel_table, w_in, attn_sink, ret_decay, ret_gn_w, w_att_out, w_ret_out, w_o, ln1_w,
           ln1_b, w_router, router_bias, w_exp_gate, w_exp_up, w_exp_down, w_sh_gate, w_sh_up, w_sh_down, ln2_w,
           ln2_b):
    assert DEPTH == 1
    B1, S1, D = x_prompt.shape
    B2, S2, _ = x_sample.shape
    assert S1 % BLOCK == 0 and S2 % BLOCK == 0 and BLOCK == RET_CHUNK
    xa, xb = x_prompt.reshape(B1 * S1, D), x_sample.reshape(B2 * S2, D)
    n1 = B1 * S1
    T = n1 + B2 * S2
    groups = ((B1, S1), (B2, S2))
    first, last, _ = _seq_tables(groups, BLOCK)
    ret_chunks = math.gcd(8, math.gcd(S1 // RET_CHUNK, S2 // RET_CHUNK))
    rfirst, rlast, rpos = _seq_tables(groups, ret_chunks * RET_CHUNK)

    att_q, att_kv = ATT_HEADS * HEAD_DIM, ATT_KV_HEADS * HEAD_DIM
    ret_qk, ret_v = RET_HEADS * RET_DK, RET_HEADS * RET_DV
    cols = np.concatenate([[0], np.cumsum([att_q, att_kv, att_kv, ret_qk, ret_qk, ret_v, ret_v, D, D])])
    c_aq, c_ak, c_av, c_rq, c_rk, c_rv, c_rg, c_ga, c_gr = (int(c) for c in cols[:-1])
    assert int(cols[-1]) == w_in.shape[2]
    rows = _slab_rows(D)
    pitch = _slab_pitch(rows)

    proj = _matmul(_concat_cast(xa, xb, BF16), w_in[0].astype(BF16), F32)
    att = _attention(proj, first, last, attn_sink[0], rel_table, c_aq, c_ak, c_av)
    ret = _retention(proj, rfirst, rlast, rpos, ret_chunks, ret_decay[0], ret_gn_w[0], max(S1, S2),
                     c_rq, c_rk, c_rv, c_rg)
    merged = _merge(att, ret, proj, w_att_out[0].astype(BF16), w_ret_out[0].astype(BF16), c_ga, c_gr)
    z1 = _residual_matmul(merged, w_o[0].astype(BF16), xa, xb)

    idx, rank, gate, counts, x1, x1_slab, x1b = _ln_router(z1, ln1_w[0], ln1_b[0], w_router[0], router_bias[0],
                                                           rows, pitch)
    te = _expert_block_rows(D)
    cnt = counts[:, 0].astype(I32)
    padded = (cnt + te - 1) // te * te
    pad_end = jnp.cumsum(padded)
    pad_start = pad_end - padded
    nblk = (T * TOP_K) // te + N_EXPERTS
    blk_e = jnp.minimum(jnp.searchsorted(pad_end, jnp.arange(nblk, dtype=I32) * te, side='right'),
                        N_EXPERTS - 1).astype(I32)
    nvalid = (pad_end[-1:] // te).astype(I32)
    slot = _slots(pad_start.astype(I32), idx, rank)
    slot_flat = slot.T.reshape(-1)

    xs = _dispatch(x1_slab, slot_flat, (pad_start + cnt).astype(I32), (padded - cnt).astype(I32), nblk * te, te,
                   rows, pitch)
    ys = _routed_ffn(xs, blk_e, nvalid, pad_start, padded, w_exp_gate[0], w_exp_up[0], w_exp_down[0],
                     te, rows, pitch)
    shared = _shared_ffn(x1b, w_sh_gate[0].astype(BF16), w_sh_up[0].astype(BF16), w_sh_down[0].astype(BF16))
    y1, y2 = _combine(ys, slot_flat, gate.T, x1, shared, ln2_w[0], ln2_b[0], n1, rows, pitch)
    return (y1.reshape(B1, S1, D), y2.reshape(B2, S2, D))
```
